```python
import math
import jax, jax.numpy as jnp
from jax import lax
import numpy as np

D_MODEL = 1024
BATCH = 2
SEQ = 8192
DEPTH = 2
DEC_BATCH = 32
DEC_SEQ = 1
PAST_LEN = 8192
PAGE_SIZE = 128

S5_WIDTH = D_MODEL // 2
S5_GROUP = 16
S5_GROUPS = S5_WIDTH // S5_GROUP
S5_STATE = 64
MLSTM_WIDTH = D_MODEL - S5_WIDTH
MLSTM_HEAD_DIM = 128
MLSTM_HEADS = MLSTM_WIDTH // MLSTM_HEAD_DIM
MLSTM_CHUNK = 64
ATT_HEAD_DIM = 128
ATT_HEADS = D_MODEL // ATT_HEAD_DIM
ATT_KV_HEADS = 2
ATT_GROUP = ATT_HEADS // ATT_KV_HEADS
ATT_WIDTH = ATT_HEADS * ATT_HEAD_DIM
IDX_HEADS = 8
IDX_DIM = 64
TOPK_MAX = 256
Q_BLOCK = 128
D_FF = 4 * D_MODEL
EPS = 1e-6

IN0_SIZES = (S5_WIDTH, MLSTM_WIDTH, MLSTM_WIDTH, MLSTM_WIDTH, MLSTM_WIDTH, MLSTM_HEADS, MLSTM_HEADS)
IN1_SIZES = (ATT_WIDTH, ATT_KV_HEADS * ATT_HEAD_DIM, ATT_KV_HEADS * ATT_HEAD_DIM, IDX_HEADS * IDX_DIM, IDX_DIM, IDX_HEADS)
IN0_WIDTH = sum(IN0_SIZES)
IN1_WIDTH = sum(IN1_SIZES)

kernel_name = "hybrid_s5_mlstm_dsa_decoder_step"


def split_cols(z, sizes):
    out, start = [], 0
    for s in sizes:
        out.append(z[..., start:start + s])
        start += s
    return out


def rms_norm(x, g):
    xf = x.astype(jnp.float32)
    y = xf * lax.rsqrt(jnp.mean(xf * xf, axis=-1, keepdims=True) + EPS)
    return (y * g.astype(jnp.float32)).astype(x.dtype)


def sq_relu_mlp(x, w_up, w_down):
    h = jax.nn.relu(x @ w_up)
    return (h * h) @ w_down


def s5_scan(u, h_re, h_im, lam_re, lam_im, log_dt, b_re, b_im, c_re, c_im, d_skip, w_glu, b_glu):
    f32 = jnp.float32
    bsz, t, _ = u.shape
    uf = u.astype(f32)
    lam = lax.complex(lam_re.astype(f32), lam_im.astype(f32))
    dt = jnp.exp(log_dt.astype(f32))[:, None]
    lam_bar = jnp.exp(lam * dt)
    b_bar = ((lam_bar - 1.0) / lam)[..., None] * lax.complex(b_re.astype(f32), b_im.astype(f32))
    ug = uf.reshape(bsz, t, S5_GROUPS, S5_GROUP).astype(jnp.complex64)
    bu = jnp.einsum('gpc,btgc->btgp', b_bar, ug)
    h0 = lax.complex(h_re.astype(f32), h_im.astype(f32))
    bu = bu.at[:, 0].add(lam_bar * h0)
    a = jnp.broadcast_to(lam_bar, bu.shape)

    def combine(l, r):
        a_l, b_l = l
        a_r, b_r = r
        return a_l * a_r, a_r * b_l + b_r

    _, hs = lax.associative_scan(combine, (a, bu), axis=1)
    c = lax.complex(c_re.astype(f32), c_im.astype(f32))
    y = jnp.einsum('gcp,btgp->btgc', c, hs).real.reshape(bsz, t, S5_WIDTH) + d_skip.astype(f32) * uf
    y = jax.nn.gelu(y)
    y = y * jax.nn.sigmoid(y @ w_glu.astype(f32) + b_glu.astype(f32))
    h_last = hs[:, -1]
    return y.astype(u.dtype), jnp.real(h_last).astype(h_re.dtype), jnp.imag(h_last).astype(h_im.dtype)


def mlstm_chunkwise(q, k, v, i_pre, f_pre, c0, n0, m0):
    f32 = jnp.float32
    bsz, t, nh, dh = q.shape
    ch = MLSTM_CHUNK if t % MLSTM_CHUNK == 0 else t
    nc = t // ch

    def to_chunks(z):
        z = z.astype(f32).reshape((bsz, nc, ch) + z.shape[2:])
        return jnp.swapaxes(jnp.moveaxis(z, 1, 0), 2, 3)

    qc, kc, vc = to_chunks(q), to_chunks(k) * (dh ** -0.5), to_chunks(v)
    ic = to_chunks(i_pre)
    lfc = jax.nn.log_sigmoid(to_chunks(f_pre))
    causal = jnp.tril(jnp.ones((ch, ch), dtype=bool))

    def step(carry, inp):
        c, n, m = carry
        qj, kj, vj, ij, lfj = inp
        b = jnp.cumsum(lfj, axis=-1)
        dmat = jnp.where(causal, b[..., :, None] - b[..., None, :] + ij[..., None, :], -jnp.inf)
        a = b + m[..., None]
        mj = jnp.maximum(a, jnp.max(dmat, axis=-1))
        w_intra = jnp.exp(dmat - mj[..., None])
        w_inter = jnp.exp(a - mj)
        s = jnp.einsum('bhtd,bhsd->bhts', qj, kj) * w_intra
        num = jnp.einsum('bhts,bhsd->bhtd', s, vj) + w_inter[..., None] * jnp.einsum('bhtk,bhkv->bhtv', qj, c)
        den = jnp.sum(s, axis=-1) + w_inter * jnp.einsum('bhtk,bhk->bht', qj, n)
        h = num / jnp.maximum(jnp.abs(den), jnp.exp(-mj))[..., None]
        m_new = mj[..., -1]
        w_end = jnp.exp(b[..., -1:] - b + ij - m_new[..., None])
        decay = jnp.exp(b[..., -1] + m - m_new)
        c_new = decay[..., None, None] * c + jnp.einsum('bhs,bhsk,bhsv->bhkv', w_end, kj, vj)
        n_new = decay[..., None] * n + jnp.einsum('bhs,bhsk->bhk', w_end, kj)
        return (c_new, n_new, m_new), h

    (c_f, n_f, m_f), hs = lax.scan(step, (c0.astype(f32), n0.astype(f32), m0.astype(f32)), (qc, kc, vc, ic, lfc))
    hs = jnp.swapaxes(jnp.moveaxis(hs, 0, 1), 2, 3).reshape(bsz, t, nh, dh)
    return hs, c_f.astype(c0.dtype), n_f.astype(n0.dtype), m_f.astype(m0.dtype)


def s5_mlstm_mixer(xn, h_re, h_im, c0, n0, m0, w_in, lam_re, lam_im, log_dt, b_re, b_im, c_re, c_im,
                   d_skip, w_glu, b_glu, b_igate, b_fgate, w_out):
    bsz, t, _ = xn.shape
    u, q, k, v, o, ig, fg = split_cols(xn @ w_in, IN0_SIZES)
    y_s5, h_re_new, h_im_new = s5_scan(u, h_re, h_im, lam_re, lam_im, log_dt, b_re, b_im, c_re, c_im,
                                       d_skip, w_glu, b_glu)
    heads = lambda z: z.reshape(bsz, t, MLSTM_HEADS, MLSTM_HEAD_DIM)
    h, c_new, n_new, m_new = mlstm_chunkwise(heads(q), heads(k), heads(v), ig + b_igate, fg + b_fgate, c0, n0, m0)
    y_ml = (jax.nn.sigmoid(o.astype(jnp.float32)) * h.reshape(bsz, t, MLSTM_WIDTH)).astype(xn.dtype)
    y = jnp.concatenate([y_s5, y_ml], axis=-1) @ w_out
    return y, (h_re_new, h_im_new, c_new, n_new, m_new)


def dsa_project(xn, w_in):
    bsz, t, _ = xn.shape
    q, k, v, qi, ki, wi = split_cols(xn @ w_in, IN1_SIZES)
    q = q.reshape(bsz, t, ATT_HEADS, ATT_HEAD_DIM)
    k = k.reshape(bsz, t, ATT_KV_HEADS, ATT_HEAD_DIM)
    v = v.reshape(bsz, t, ATT_KV_HEADS, ATT_HEAD_DIM)
    qi = qi.reshape(bsz, t, IDX_HEADS, IDX_DIM)
    return q, k, v, qi, ki, wi


def dsa_block(q, qi, wi, qpos, ki_all, gather_kv):
    f32 = jnp.float32
    bsz, nq = q.shape[0], q.shape[1]
    s_len = ki_all.shape[1]
    topk = min(TOPK_MAX, s_len // 4)
    logits = jnp.einsum('bqhd,bsd->bqhs', qi.astype(f32), ki_all.astype(f32)) * (IDX_DIM ** -0.5)
    score = jnp.einsum('bqhs,bqh->bqs', jax.nn.relu(logits), wi.astype(f32) * (IDX_HEADS ** -0.5))
    visible = jnp.arange(s_len)[None, :] <= qpos[:, None]
    score = jnp.where(visible[None], score, -jnp.inf)
    _, idx = lax.top_k(score, topk)
    valid = idx <= qpos[None, :, None]
    k_sel, v_sel = gather_kv(idx)
    qg = q.reshape(bsz, nq, ATT_KV_HEADS, ATT_GROUP, ATT_HEAD_DIM).astype(f32)
    att = jnp.einsum('bqgrd,bqkgd->bqgrk', qg, k_sel.astype(f32)) * (ATT_HEAD_DIM ** -0.5)
    att = jnp.where(valid[:, :, None, None, :], att, -jnp.inf)
    p = jax.nn.softmax(att, axis=-1)
    o = jnp.einsum('bqgrk,bqkgd->bqgrd', p, v_sel.astype(f32))
    return o.reshape(bsz, nq, ATT_WIDTH).astype(q.dtype)


def dsa_attend(q, qi, wi, qpos, ki_all, gather_kv, w_out):
    bsz, t = q.shape[0], q.shape[1]
    fn = lambda qb, qib, wib, pb: dsa_block(qb, qib, wib, pb, ki_all, gather_kv)
    if t % Q_BLOCK != 0 or t <= Q_BLOCK:
        out = fn(q, qi, wi, qpos)
    else:
        nb = t // Q_BLOCK
        split = lambda z: jnp.moveaxis(z.reshape((bsz, nb, Q_BLOCK) + z.shape[2:]), 1, 0)
        out = lax.map(lambda a: fn(*a), (split(q), split(qi), split(wi), qpos.reshape(nb, Q_BLOCK)))
        out = jnp.moveaxis(out, 0, 1).reshape(bsz, t, ATT_WIDTH)
    return out @ w_out


def gather_rows(rows, idx):
    return jax.vmap(lambda r, i: r[i])(rows, idx)


def gather_paged(pool, page_table, new_rows, idx):
    past_len = page_table.shape[1] * PAGE_SIZE
    in_past = idx < past_len
    pidx = jnp.minimum(idx, past_len - 1)
    phys = jax.vmap(lambda pt, i: pt[i])(page_table, pidx // PAGE_SIZE)
    from_past = pool[phys, pidx % PAGE_SIZE]
    from_new = gather_rows(new_rows, jnp.clip(idx - past_len, 0, new_rows.shape[1] - 1))
    mask = in_past.reshape(in_past.shape + (1,) * (from_past.ndim - in_past.ndim))
    return jnp.where(mask, from_past, from_new.astype(from_past.dtype))


def setup_inputs(seed: int = 0) -> dict:
    key = jax.random.key(seed)
    ks = iter(jax.random.split(key, 48))
    f32 = jnp.float32
    nrm = lambda shape, scale: jax.random.normal(next(ks), shape, f32) * scale
    n_pages = PAST_LEN // PAGE_SIZE
    n_used = DEC_BATCH * n_pages
    n_phys = n_used + max(1, n_used // 4)
    page_table = jax.random.permutation(next(ks), n_phys)[:n_used].reshape(DEC_BATCH, n_pages).astype(jnp.int32)
    return {
        "x_prompt": nrm((BATCH, SEQ, D_MODEL), 1.0),
        "x_sample": nrm((DEC_BATCH, DEC_SEQ, D_MODEL), 1.0),
        "state_s5_re": nrm((DEC_BATCH, S5_GROUPS, S5_STATE), 0.1),
        "state_s5_im": nrm((DEC_BATCH, S5_GROUPS, S5_STATE), 0.1),
        "state_mlstm_c": nrm((DEC_BATCH, MLSTM_HEADS, MLSTM_HEAD_DIM, MLSTM_HEAD_DIM), MLSTM_HEAD_DIM ** -0.5),
        "state_mlstm_n": nrm((DEC_BATCH, MLSTM_HEADS, MLSTM_HEAD_DIM), MLSTM_HEAD_DIM ** -0.5),
        "state_mlstm_m": nrm((DEC_BATCH, MLSTM_HEADS), 0.5),
        "cache_k": nrm((n_phys, PAGE_SIZE, ATT_KV_HEADS, ATT_HEAD_DIM), 1.0),
        "cache_v": nrm((n_phys, PAGE_SIZE, ATT_KV_HEADS, ATT_HEAD_DIM), 1.0),
        "cache_kidx": nrm((n_phys, PAGE_SIZE, IDX_DIM), 1.0),
        "page_table": page_table,
        "norm_mix": 1.0 + nrm((DEPTH, D_MODEL), 0.01),
        "norm_mlp": 1.0 + nrm((DEPTH, D_MODEL), 0.01),
        "norm_final": 1.0 + nrm((D_MODEL,), 0.01),
        "w_in0": nrm((D_MODEL, IN0_WIDTH), D_MODEL ** -0.5),
        "s5_lam_re": -0.5 + nrm((S5_GROUPS, S5_STATE), 0.01),
        "s5_lam_im": math.pi * jnp.arange(S5_STATE, dtype=f32)[None, :] + nrm((S5_GROUPS, S5_STATE), 0.01),
        "s5_log_dt": jax.random.uniform(next(ks), (S5_GROUPS,), f32, math.log(1e-3), math.log(1e-1)),
        "s5_b_re": nrm((S5_GROUPS, S5_STATE, S5_GROUP), (2 * S5_GROUP) ** -0.5),
        "s5_b_im": nrm((S5_GROUPS, S5_STATE, S5_GROUP), (2 * S5_GROUP) ** -0.5),
        "s5_c_re": nrm((S5_GROUPS, S5_GROUP, S5_STATE), (2 * S5_STATE) ** -0.5),
        "s5_c_im": nrm((S5_GROUPS, S5_GROUP, S5_STATE), (2 * S5_STATE) ** -0.5),
        "s5_d": nrm((S5_WIDTH,), 1.0),
        "w_glu": nrm((S5_WIDTH, S5_WIDTH), S5_WIDTH ** -0.5),
        "b_glu": nrm((S5_WIDTH,), 0.01),
        "b_igate": nrm((MLSTM_HEADS,), 0.1),
        "b_fgate": jnp.linspace(3.0, 6.0, MLSTM_HEADS, dtype=f32) + nrm((MLSTM_HEADS,), 0.01),
        "w_out0": nrm((S5_WIDTH + MLSTM_WIDTH, D_MODEL), (S5_WIDTH + MLSTM_WIDTH) ** -0.5),
        "w_in1": nrm((D_MODEL, IN1_WIDTH), D_MODEL ** -0.5),
        "w_out1": nrm((ATT_WIDTH, D_MODEL), ATT_WIDTH ** -0.5),
        "w_up": nrm((DEPTH, D_MODEL, D_FF), D_MODEL ** -0.5),
        "w_down": nrm((DEPTH, D_FF, D_MODEL), D_FF ** -0.5),
    }


def reference(x_prompt, x_sample, state_s5_re, state_s5_im, state_mlstm_c, state_mlstm_n, state_mlstm_m,
              cache_k, cache_v, cache_kidx, page_table, norm_mix, norm_mlp, norm_final, w_in0, s5_lam_re,
              s5_lam_im, s5_log_dt, s5_b_re, s5_b_im, s5_c_re, s5_c_im, s5_d, w_glu, b_glu, b_igate, b_fgate,
              w_out0, w_in1, w_out1, w_up, w_down):
    bp, tp, _ = x_prompt.shape
    db, ts, _ = x_sample.shape
    past_len = page_table.shape[1] * PAGE_SIZE
    zeros = lambda *shape: jnp.zeros(shape, x_prompt.dtype)
    even_w = (w_in0, s5_lam_re, s5_lam_im, s5_log_dt, s5_b_re, s5_b_im, s5_c_re, s5_c_im, s5_d, w_glu, b_glu,
              b_igate, b_fgate, w_out0)
    hp, hs = x_prompt, x_sample
    for layer in range(DEPTH):
        xp = rms_norm(hp, norm_mix[layer])
        xs = rms_norm(hs, norm_mix[layer])
        if layer % 2 == 0:
            yp, (p_s5_re, p_s5_im, p_c, p_n, p_m) = s5_mlstm_mixer(
                xp, zeros(bp, S5_GROUPS, S5_STATE), zeros(bp, S5_GROUPS, S5_STATE),
                zeros(bp, MLSTM_HEADS, MLSTM_HEAD_DIM, MLSTM_HEAD_DIM), zeros(bp, MLSTM_HEADS, MLSTM_HEAD_DIM),
                zeros(bp, MLSTM_HEADS), *even_w)
            ys, (s_s5_re, s_s5_im, s_c, s_n, s_m) = s5_mlstm_mixer(
                xs, state_s5_re, state_s5_im, state_mlstm_c, state_mlstm_n, state_mlstm_m, *even_w)
        else:
            q, p_k, p_v, qi, p_kidx, wi = dsa_project(xp, w_in1)
            yp = dsa_attend(q, qi, wi, jnp.arange(tp), p_kidx,
                            lambda idx: (gather_rows(p_k, idx), gather_rows(p_v, idx)), w_out1)
            q, s_k, s_v, qi, s_kidx, wi = dsa_project(xs, w_in1)
            ki_past = cache_kidx[page_table].reshape(db, past_len, IDX_DIM)
            ki_all = jnp.concatenate([ki_past, s_kidx.astype(ki_past.dtype)], axis=1)
            ys = dsa_attend(q, qi, wi, past_len + jnp.arange(ts), ki_all,
                            lambda idx: (gather_paged(cache_k, page_table, s_k, idx),
                                         gather_paged(cache_v, page_table, s_v, idx)), w_out1)
        hp = hp + yp
        hs = hs + ys
        hp = hp + sq_relu_mlp(rms_norm(hp, norm_mlp[layer]), w_up[layer], w_down[layer])
        hs = hs + sq_relu_mlp(rms_norm(hs, norm_mlp[layer]), w_up[layer], w_down[layer])
    y_prompt = rms_norm(hp, norm_final)
    y_sample = rms_norm(hs, norm_final)
    return (y_prompt, y_sample, p_s5_re, p_s5_im, p_c, p_n, p_m, p_k, p_v, p_kidx,
            s_s5_re, s_s5_im, s_c, s_n, s_m, s_k, s_v, s_kidx)
```

```python
import functools
import math

import jax
import jax.numpy as jnp
from jax import lax
from jax.experimental import pallas as pl
from jax.experimental.pallas import tpu as pltpu

F32, BF16, I32 = jnp.float32, jnp.bfloat16, jnp.int32

EPS = 1e-6
LANES = 128
PAGE = 128
S5_GROUP = 16
S5_STATE = 64
HEAD_DIM = 128
IDX_DIM = 64
IDX_HEADS = 8
TOPK = 256
Q_BLOCK = 128
KEY_CHUNK = 512
MLSTM_CHUNK = 128
S5_CHUNK = 256
ROW_TILE = 512
FF_CHUNK = 1024
PAGES_PER_STEP = 8
INT_MIN = -2 ** 31
NEG_BIG = -1e30
VMEM_LIMIT = 56 * 1024 * 1024


def _cparams(*sem):
    return pltpu.CompilerParams(dimension_semantics=sem, vmem_limit_bytes=VMEM_LIMIT)


def _rms(x, g):
    return x * lax.rsqrt(jnp.mean(x * x, axis=-1, keepdims=True) + EPS) * g


def _dot(a, b):
    return jnp.dot(a, b, preferred_element_type=F32)


def _dot_nt(a, b):
    return lax.dot_general(a, b, (((1,), (1,)), ((), ())), preferred_element_type=F32)


def _dot_tn(a, b):
    return lax.dot_general(a, b, (((0,), (0,)), ((), ())), preferred_element_type=F32)


def _proj0_kernel(x_ref, g_ref, w_ref, o_ref):
    xn = _rms(x_ref[...], g_ref[...]).astype(BF16)
    o_ref[...] = _dot(xn, w_ref[...])


def _proj0(x, g, w):
    n, d = x.shape
    tm = min(ROW_TILE, n)
    wtot = w.shape[1]
    return pl.pallas_call(
        _proj0_kernel,
        grid=(n // tm,),
        in_specs=[pl.BlockSpec((tm, d), lambda i: (i, 0)),
                  pl.BlockSpec((1, d), lambda i: (0, 0)),
                  pl.BlockSpec((d, wtot), lambda i: (0, 0))],
        out_specs=pl.BlockSpec((tm, wtot), lambda i: (i, 0)),
        out_shape=jax.ShapeDtypeStruct((n, wtot), F32),
        compiler_params=_cparams("parallel"),
        name="proj0",
    )(x, g, w)


_Q1, _K1, _V1, _QI1, _KI1, _WI1, _END1 = 0, 1024, 1280, 1536, 2048, 2176, 2304


def _proj1_kernel(x_ref, g_ref, w_ref, wkit_ref, q_ref, k_ref, v_ref, kbf_ref, vbf_ref, qi_ref,
                  kidx_ref, wi_ref, kit_ref):
    xn = _rms(x_ref[...], g_ref[...]).astype(BF16)
    z = _dot(xn, w_ref[...])
    q_ref[...] = (z[:, _Q1:_K1] * (HEAD_DIM ** -0.5)).astype(BF16)
    k = z[:, _K1:_V1]
    v = z[:, _V1:_QI1]
    k_ref[...] = k
    v_ref[...] = v
    kbf_ref[...] = k.astype(BF16)
    vbf_ref[...] = v.astype(BF16)
    qi_ref[...] = z[:, _QI1:_KI1].astype(BF16)
    kidx_ref[...] = z[:, _KI1:_KI1 + IDX_DIM]
    wi_ref[...] = z[:, _WI1:_END1] * ((IDX_DIM ** -0.5) * (IDX_HEADS ** -0.5))
    kit_ref[...] = _dot_nt(wkit_ref[...], xn).astype(BF16)


def _proj1(x, g, w, wkit):
    n, d = x.shape
    tm = min(ROW_TILE, n)
    row = lambda width: pl.BlockSpec((tm, width), lambda i: (i, 0))
    shp = lambda width, dt: jax.ShapeDtypeStruct((n, width), dt)
    return pl.pallas_call(
        _proj1_kernel,
        grid=(n // tm,),
        in_specs=[row(d),
                  pl.BlockSpec((1, d), lambda i: (0, 0)),
                  pl.BlockSpec((d, _END1), lambda i: (0, 0)),
                  pl.BlockSpec((IDX_DIM, d), lambda i: (0, 0))],
        out_specs=[row(1024), row(256), row(256), row(256), row(256), row(512), row(IDX_DIM), row(LANES),
                   pl.BlockSpec((IDX_DIM, tm), lambda i: (0, i))],
        out_shape=[shp(1024, BF16), shp(256, F32), shp(256, F32), shp(256, BF16), shp(256, BF16),
                   shp(512, BF16), shp(IDX_DIM, F32), shp(LANES, F32),
                   jax.ShapeDtypeStruct((IDX_DIM, n), BF16)],
        compiler_params=_cparams("parallel"),
        name="proj1",
    )(x, g, w, wkit)


def _post_kernel(h_ref, ya_ref, yb_ref, wo_ref, g_ref, wup_ref, wdn_ref, gf_ref, out_ref,
                 h1_s, xn_s, acc_s, *, final_norm):
    j = pl.program_id(1)
    half = ya_ref.shape[1]

    @pl.when(j == 0)
    def _():
        h1 = h_ref[...] + _dot(ya_ref[...], wo_ref[:half, :]) + _dot(yb_ref[...], wo_ref[half:, :])
        h1_s[...] = h1
        xn_s[...] = _rms(h1, g_ref[...]).astype(BF16)
        acc_s[...] = jnp.zeros_like(acc_s)

    r = jnp.maximum(_dot(xn_s[...], wup_ref[...]), 0.0)
    acc_s[...] += _dot((r * r).astype(BF16), wdn_ref[...])

    @pl.when(j == pl.num_programs(1) - 1)
    def _():
        o = h1_s[...] + acc_s[...]
        if final_norm:
            o = _rms(o, gf_ref[...])
        out_ref[...] = o


def _post(h, ya, yb_spec_arg, wo, g, wup, wdn, gf, *, final_norm):
    n, d = h.shape
    tm = min(ROW_TILE, n)
    yb, yb_col = yb_spec_arg
    half = d // 2
    dff = wup.shape[1]
    return pl.pallas_call(
        functools.partial(_post_kernel, final_norm=final_norm),
        grid=(n // tm, dff // FF_CHUNK),
        in_specs=[pl.BlockSpec((tm, d), lambda i, j: (i, 0)),
                  pl.BlockSpec((tm, half), lambda i, j: (i, 0)),
                  pl.BlockSpec((tm, half), lambda i, j: (i, yb_col)),
                  pl.BlockSpec((d, d), lambda i, j: (0, 0)),
                  pl.BlockSpec((1, d), lambda i, j: (0, 0)),
                  pl.BlockSpec((d, FF_CHUNK), lambda i, j: (0, j)),
                  pl.BlockSpec((FF_CHUNK, d), lambda i, j: (j, 0)),
                  pl.BlockSpec((1, d), lambda i, j: (0, 0))],
        out_specs=pl.BlockSpec((tm, d), lambda i, j: (i, 0)),
        out_shape=jax.ShapeDtypeStruct((n, d), F32),
        scratch_shapes=[pltpu.VMEM((tm, d), F32), pltpu.VMEM((tm, d), BF16), pltpu.VMEM((tm, d), F32)],
        compiler_params=_cparams("parallel", "arbitrary"),
        name="post_final" if final_norm else "post",
    )(h, ya, yb, wo, g, wup, wdn, gf)


def _s5_output(hre, him, u, wc_ref, d_ref, wg_ref, bg_ref):
    half = hre.shape[1]
    y = _dot(hre.astype(BF16), wc_ref[:half, :]) + _dot(him.astype(BF16), wc_ref[half:, :])
    y = jax.nn.gelu(y + d_ref[...] * u)
    gate = jax.nn.sigmoid(_dot(y.astype(BF16), wg_ref[...]) + bg_ref[...])
    return (y * gate).astype(BF16)


def _s5_seq_kernel(u_ref, h0_ref, lam_ref, wb_ref, wc_ref, d_ref, wg_ref, bg_ref, y_ref, ht_ref,
                   hre_s, him_s, carry_s):
    half = hre_s.shape[1]

    @pl.when(pl.program_id(1) == 0)
    def _():
        carry_s[...] = h0_ref[...]

    u = u_ref[...]
    bu = _dot(u.astype(BF16), wb_ref[...])
    hre_s[...] = bu[:, :half]
    him_s[...] = bu[:, half:]
    a_re = lam_ref[0:1, :]
    a_im = lam_ref[1:2, :]

    def step(t, carry):
        h_re, h_im = carry
        n_re = a_re * h_re - a_im * h_im + hre_s[pl.ds(t, 1), :]
        n_im = a_re * h_im + a_im * h_re + him_s[pl.ds(t, 1), :]
        hre_s[pl.ds(t, 1), :] = n_re
        him_s[pl.ds(t, 1), :] = n_im
        return n_re, n_im

    h_re, h_im = lax.fori_loop(0, hre_s.shape[0], step, (carry_s[:, :half], carry_s[:, half:]), unroll=8)
    carry_s[:, :half] = h_re
    carry_s[:, half:] = h_im
    ht_ref[...] = carry_s[...]
    y_ref[...] = _s5_output(hre_s[...], him_s[...], u, wc_ref, d_ref, wg_ref, bg_ref)


def _s5_step_kernel(u_ref, h0_ref, lam_ref, wb_ref, wc_ref, d_ref, wg_ref, bg_ref, y_ref, ht_ref):
    half = lam_ref.shape[1]
    u = u_ref[...]
    bu = _dot(u.astype(BF16), wb_ref[...])
    a_re = lam_ref[0:1, :]
    a_im = lam_ref[1:2, :]
    h_re = h0_ref[:, :half]
    h_im = h0_ref[:, half:]
    n_re = a_re * h_re - a_im * h_im + bu[:, :half]
    n_im = a_re * h_im + a_im * h_re + bu[:, half:]
    ht_ref[:, :half] = n_re
    ht_ref[:, half:] = n_im
    y_ref[...] = _s5_output(n_re, n_im, u, wc_ref, d_ref, wg_ref, bg_ref)


def _s5_param_specs(width, nstate, imap):
    return [pl.BlockSpec((2, nstate), imap),
            pl.BlockSpec((width, 2 * nstate), imap),
            pl.BlockSpec((2 * nstate, width), imap),
            pl.BlockSpec((1, width), imap),
            pl.BlockSpec((width, width), imap),
            pl.BlockSpec((1, width), imap)]


def _s5_seq(z0, h0, params, width):
    bsz, t, _ = z0.shape
    nstate = params[0].shape[1]
    ts = min(S5_CHUNK, t)
    const = lambda b, c: (0, 0)
    return pl.pallas_call(
        _s5_seq_kernel,
        grid=(bsz, t // ts),
        in_specs=[pl.BlockSpec((None, ts, width), lambda b, c: (b, c, 0)),
                  pl.BlockSpec((None, 1, 2 * nstate), lambda b, c: (b, 0, 0))]
                 + _s5_param_specs(width, nstate, const),
        out_specs=[pl.BlockSpec((None, ts, width), lambda b, c: (b, c, 0)),
                   pl.BlockSpec((None, 1, 2 * nstate), lambda b, c: (b, 0, 0))],
        out_shape=[jax.ShapeDtypeStruct((bsz, t, width), BF16),
                   jax.ShapeDtypeStruct((bsz, 1, 2 * nstate), F32)],
        scratch_shapes=[pltpu.VMEM((ts, nstate), F32), pltpu.VMEM((ts, nstate), F32),
                        pltpu.VMEM((1, 2 * nstate), F32)],
        compiler_params=_cparams("parallel", "arbitrary"),
        name="s5_scan",
    )(z0, h0, *params)


def _s5_step(z0, h0, params, width):
    rows = z0.shape[0]
    nstate = params[0].shape[1]
    const = lambda i: (0, 0)
    return pl.pallas_call(
        _s5_step_kernel,
        grid=(1,),
        in_specs=[pl.BlockSpec((rows, width), const), pl.BlockSpec((rows, 2 * nstate), const)]
                 + _s5_param_specs(width, nstate, const),
        out_specs=[pl.BlockSpec((rows, width), const), pl.BlockSpec((rows, 2 * nstate), const)],
        out_shape=[jax.ShapeDtypeStruct((rows, width), BF16),
                   jax.ShapeDtypeStruct((rows, 2 * nstate), F32)],
        compiler_params=_cparams("arbitrary"),
        name="s5_step",
    )(z0, h0, *params)


def _mlstm_chunk_kernel(q_ref, k_ref, v_ref, o_ref, gcol_ref, grow_ref, bcol_ref, brow_ref,
                        c0_ref, n0_ref, m0_ref, y_ref, c_ref, n_ref, m_ref):
    nh = c_ref.shape[0]
    ch = q_ref.shape[0]

    @pl.when(pl.program_id(1) == 0)
    def _():
        c_ref[...] = c0_ref[...]
        n_ref[...] = n0_ref[...]
        m_ref[...] = m0_ref[...]

    gcol = gcol_ref[...] + brow_ref[...]
    grow = grow_ref[...] + bcol_ref[...]
    t_idx = lax.broadcasted_iota(I32, (ch, ch), 0)
    s_idx = lax.broadcasted_iota(I32, (ch, ch), 1)
    causal = t_idx >= s_idx
    for h in range(nh):
        sl = slice(h * HEAD_DIM, (h + 1) * HEAD_DIM)
        q = q_ref[:, sl]
        k = k_ref[:, sl] * (HEAD_DIM ** -0.5)
        v = v_ref[:, sl]
        q_bf, k_bf, v_bf = q.astype(BF16), k.astype(BF16), v.astype(BF16)
        i_col = gcol[:, h:h + 1]
        i_row = grow[h:h + 1, :]
        lf_col = jax.nn.log_sigmoid(gcol[:, nh + h:nh + h + 1])
        lf_row = jax.nn.log_sigmoid(grow[nh + h:nh + h + 1, :])
        b_col = jnp.sum(jnp.where(causal, lf_row, 0.0), axis=1, keepdims=True)
        b_row = jnp.sum(jnp.where(causal, 0.0, lf_col) , axis=0, keepdims=True)
        b_row = b_row + lf_row
        m_prev = m_ref[h][:, 0:1]
        dmat = jnp.where(causal, b_col - b_row + i_row, -jnp.inf)
        a_col = b_col + m_prev
        mj = jnp.maximum(a_col, jnp.max(dmat, axis=1, keepdims=True))
        w_intra = jnp.exp(dmat - mj)
        w_inter = jnp.exp(a_col - mj)
        s = _dot_nt(q_bf, k_bf) * w_intra
        c_prev = c_ref[h]
        n_prev = n_ref[h]
        num = _dot(s.astype(BF16), v_bf) + w_inter * _dot(q_bf, c_prev.astype(BF16))
        den = jnp.sum(s, axis=1, keepdims=True) + w_inter * jnp.sum(q * n_prev, axis=1, keepdims=True)
        hout = num / jnp.maximum(jnp.abs(den), jnp.exp(-mj))
        y_ref[:, sl] = (jax.nn.sigmoid(o_ref[:, sl]) * hout).astype(BF16)
        m_new = mj[ch - 1:ch, :]
        b_last = b_col[ch - 1:ch, :]
        w_end = jnp.exp(b_last - b_col + i_col - m_new)
        decay = jnp.exp(b_last + m_prev - m_new)
        kw = k * w_end
        c_ref[h] = decay * c_prev + _dot_tn(kw.astype(BF16), v_bf)
        n_ref[h] = decay * n_prev + jnp.sum(kw, axis=0, keepdims=True)
        m_ref[h] = jnp.broadcast_to(m_new, (1, LANES))


def _mlstm_chunked(z0, gcol, grow, bias_col, bias_row, c0, n0, m0, nh):
    bsz, t, _ = z0.shape
    width = nh * HEAD_DIM
    ch = MLSTM_CHUNK
    zspec = lambda blk: pl.BlockSpec((None, ch, width), lambda b, c: (b, c, blk))
    state = lambda shape: pl.BlockSpec((None,) + shape, lambda b, c: (b,) + (0,) * len(shape))
    return pl.pallas_call(
        _mlstm_chunk_kernel,
        grid=(bsz, t // ch),
        in_specs=[zspec(1), zspec(2), zspec(3), zspec(4),
                  pl.BlockSpec((None, ch, 2 * nh), lambda b, c: (b, c, 0)),
                  pl.BlockSpec((None, 2 * nh, ch), lambda b, c: (b, 0, c)),
                  pl.BlockSpec((2 * nh, 1), lambda b, c: (0, 0)),
                  pl.BlockSpec((1, 2 * nh), lambda b, c: (0, 0)),
                  state((nh, HEAD_DIM, HEAD_DIM)), state((nh, 1, HEAD_DIM)), state((nh, 1, LANES))],
        out_specs=[pl.BlockSpec((None, ch, width), lambda b, c: (b, c, 0)),
                   state((nh, HEAD_DIM, HEAD_DIM)), state((nh, 1, HEAD_DIM)), state((nh, 1, LANES))],
        out_shape=[jax.ShapeDtypeStruct((bsz, t, width), BF16),
                   jax.ShapeDtypeStruct((bsz, nh, HEAD_DIM, HEAD_DIM), F32),
                   jax.ShapeDtypeStruct((bsz, nh, 1, HEAD_DIM), F32),
                   jax.ShapeDtypeStruct((bsz, nh, 1, LANES), F32)],
        compiler_params=_cparams("parallel", "arbitrary"),
        name="mlstm_chunk",
    )(z0, z0, z0, z0, gcol, grow, bias_col, bias_row, c0, n0, m0)


def _to_column(row):
    n = row.shape[1]
    eye = lax.broadcasted_iota(I32, (n, n), 0) == lax.broadcasted_iota(I32, (n, n), 1)
    return jnp.sum(jnp.where(eye, row, 0.0), axis=1, keepdims=True)


def _mlstm_step_kernel(q_ref, k_ref, v_ref, o_ref, g_ref, brow_ref, c0_ref, n0_ref, m0_ref,
                       y_ref, c_ref, n_ref, m_ref):
    nh = c_ref.shape[0]
    g = g_ref[...] + brow_ref[...]
    for h in range(nh):
        sl = slice(h * HEAD_DIM, (h + 1) * HEAD_DIM)
        q = q_ref[:, sl]
        k = k_ref[:, sl] * (HEAD_DIM ** -0.5)
        v = v_ref[:, sl]
        i_pre = g[:, h:h + 1]
        lf = jax.nn.log_sigmoid(g[:, nh + h:nh + h + 1])
        m_prev = m0_ref[h][:, 0:1]
        c_prev = c0_ref[h]
        n_prev = n0_ref[h]
        a = lf + m_prev
        mj = jnp.maximum(a, i_pre)
        w_intra = jnp.exp(i_pre - mj)
        w_inter = jnp.exp(a - mj)
        s = jnp.sum(q * k, axis=1, keepdims=True) * w_intra
        q_col = _to_column(q)
        k_col = _to_column(k)
        num = s * v + w_inter * jnp.sum(q_col * c_prev, axis=0, keepdims=True)
        den = s + w_inter * jnp.sum(q * n_prev, axis=1, keepdims=True)
        hout = num / jnp.maximum(jnp.abs(den), jnp.exp(-mj))
        y_ref[:, sl] = (jax.nn.sigmoid(o_ref[:, sl]) * hout).astype(BF16)
        w_end = jnp.exp(i_pre - mj)
        decay = jnp.exp(a - mj)
        c_ref[h] = decay * c_prev + (w_end * k_col) * v
        n_ref[h] = decay * n_prev + w_end * k
        m_ref[h] = jnp.broadcast_to(mj, (1, LANES))


def _mlstm_step(z0, g, bias_row, c0, n0, m0, nh):
    bsz = z0.shape[0]
    width = nh * HEAD_DIM
    zspec = lambda blk: pl.BlockSpec((None, 1, width), lambda b: (b, 0, blk))
    state = lambda shape: pl.BlockSpec((None,) + shape, lambda b: (b,) + (0,) * len(shape))
    return pl.pallas_call(
        _mlstm_step_kernel,
        grid=(bsz,),
        in_specs=[zspec(1), zspec(2), zspec(3), zspec(4),
                  pl.BlockSpec((None, 1, 2 * nh), lambda b: (b, 0, 0)),
                  pl.BlockSpec((1, 2 * nh), lambda b: (0, 0)),
                  state((nh, HEAD_DIM, HEAD_DIM)), state((nh, 1, HEAD_DIM)), state((nh, 1, LANES))],
        out_specs=[pl.BlockSpec((None, 1, width), lambda b: (b, 0, 0)),
                   state((nh, HEAD_DIM, HEAD_DIM)), state((nh, 1, HEAD_DIM)), state((nh, 1, LANES))],
        out_shape=[jax.ShapeDtypeStruct((bsz, 1, width), BF16),
                   jax.ShapeDtypeStruct((bsz, nh, HEAD_DIM, HEAD_DIM), F32),
                   jax.ShapeDtypeStruct((bsz, nh, 1, HEAD_DIM), F32),
                   jax.ShapeDtypeStruct((bsz, nh, 1, LANES), F32)],
        compiler_params=_cparams("parallel"),
        name="mlstm_step",
    )(z0, z0, z0, z0, g, bias_row, c0, n0, m0)


def _sort_key(score):
    bits = lax.bitcast_convert_type(jnp.where(score == 0.0, 0.0, score), I32)
    return bits ^ ((bits >> 31) & jnp.int32(0x7FFFFFFF))


def _count_rows(key_ref, ntiles, pred):
    rows = key_ref.shape[0]

    def body(j, acc):
        off = pl.multiple_of(j * LANES, LANES)
        return acc + jnp.where(pred(key_ref[:, pl.ds(off, LANES)], off), 1, 0)

    acc = lax.fori_loop(0, ntiles, body, jnp.zeros((rows, LANES), I32))
    return jnp.sum(acc, axis=1, keepdims=True)


def _select_threshold(key_ref, ntiles, j0_s):
    rows, width = key_ref.shape
    lane = lax.broadcasted_iota(I32, (rows, LANES), 1)

    def bit_step(it, thr):
        cand = thr + (jnp.int32(1) << (31 - it))
        cand_b = jnp.broadcast_to(cand, (rows, LANES))
        cnt = _count_rows(key_ref, ntiles, lambda tile, off: tile >= cand_b)
        return jnp.where(cnt >= TOPK, cand, thr)

    thr = lax.fori_loop(0, 32, bit_step, jnp.full((rows, 1), INT_MIN, I32))
    thr_b = jnp.broadcast_to(thr, (rows, LANES))
    need = TOPK - _count_rows(key_ref, ntiles, lambda tile, off: tile > thr_b)
    n_eq = _count_rows(key_ref, ntiles, lambda tile, off: tile == thr_b)
    has_thr = thr > INT_MIN
    j0_s[...] = jnp.where(has_thr, jnp.int32(width), jnp.int32(-1))
    surplus = jnp.max(jnp.where(has_thr & (n_eq > need), 1, 0))

    @pl.when(surplus > 0)
    def _():
        def idx_step(it, j0):
            cand = j0 | (jnp.int32(1) << (nbits - 1 - it))
            cand_b = jnp.broadcast_to(cand, (rows, LANES))
            cnt = _count_rows(key_ref, ntiles,
                              lambda tile, off: (tile == thr_b) & (lane + off < cand_b))
            return jnp.where(cnt < need, cand, j0)

        nbits = max(1, (width - 1).bit_length())
        j0 = lax.fori_loop(0, nbits, idx_step, jnp.zeros((rows, 1), I32))
        j0_s[...] = jnp.where(has_thr, j0, jnp.int32(-1))

    return thr, j0_s[...]


def _mask_bias(keys, cols, thr, j0):
    sel = (keys > thr) | ((keys == thr) & (cols <= j0))
    return jnp.where(sel, 0.0, NEG_BIG)


def _dsa_prompt_kernel(q_ref, qi_ref, wi_ref, kit_ref, k_ref, v_ref, o_ref,
                       key_s, j0_s, m_s, l_s, acc_s):
    qb = q_ref.shape[0]
    nheads = q_ref.shape[1] // HEAD_DIM
    nkv = k_ref.shape[1] // HEAD_DIM
    rep = nheads // nkv
    i = pl.program_id(1)
    nvis = (i + 1) * qb
    nchunks = (nvis + KEY_CHUNK - 1) // KEY_CHUNK
    qpos = i * qb + lax.broadcasted_iota(I32, (qb, 1), 0)
    lane_c = lax.broadcasted_iota(I32, (1, KEY_CHUNK), 1)

    qi = qi_ref[...]
    wi = wi_ref[...]
    qi_h = [qi[:, h * IDX_DIM:(h + 1) * IDX_DIM] for h in range(IDX_HEADS)]

    def score_chunk(c, _):
        off = pl.multiple_of(c * KEY_CHUNK, KEY_CHUNK)
        kt = kit_ref[:, pl.ds(off, KEY_CHUNK)]
        sc = jnp.zeros((qb, KEY_CHUNK), F32)
        for h in range(IDX_HEADS):
            sc = sc + jnp.maximum(_dot(qi_h[h], kt), 0.0) * wi[:, h:h + 1]
        key_s[:, pl.ds(off, KEY_CHUNK)] = jnp.where(lane_c + off <= qpos, _sort_key(sc), INT_MIN)
        return 0

    lax.fori_loop(0, nchunks, score_chunk, 0)

    thr, j0 = _select_threshold(key_s, nchunks * (KEY_CHUNK // LANES), j0_s)

    m_s[...] = jnp.full_like(m_s, NEG_BIG)
    l_s[...] = jnp.zeros_like(l_s)
    acc_s[...] = jnp.zeros_like(acc_s)
    q = q_ref[...]
    q_g = [jnp.concatenate([q[:, (g * rep + r) * HEAD_DIM:(g * rep + r + 1) * HEAD_DIM]
                            for r in range(rep)], axis=0) for g in range(nkv)]

    def attend_chunk(c, _):
        off = pl.multiple_of(c * KEY_CHUNK, KEY_CHUNK)
        bias = _mask_bias(key_s[:, pl.ds(off, KEY_CHUNK)], lane_c + off, thr, j0)
        bias = jnp.concatenate([bias] * rep, axis=0)
        for g in range(nkv):
            kc = k_ref[pl.ds(off, KEY_CHUNK), g * HEAD_DIM:(g + 1) * HEAD_DIM]
            vc = v_ref[pl.ds(off, KEY_CHUNK), g * HEAD_DIM:(g + 1) * HEAD_DIM]
            att = _dot_nt(q_g[g], kc) + bias
            m_old = m_s[g]
            m_new = jnp.maximum(m_old, jnp.max(att, axis=1, keepdims=True))
            alpha = jnp.exp(m_old - m_new)
            p = jnp.exp(att - m_new)
            l_s[g] = alpha * l_s[g] + jnp.sum(p, axis=1, keepdims=True)
            acc_s[g] = alpha * acc_s[g] + _dot(p.astype(BF16), vc)
            m_s[g] = m_new
        return 0

    lax.fori_loop(0, nchunks, attend_chunk, 0)

    for g in range(nkv):
        out = acc_s[g] / l_s[g]
        for r in range(rep):
            hd = g * rep + r
            o_ref[:, hd * HEAD_DIM:(hd + 1) * HEAD_DIM] = out[r * qb:(r + 1) * qb, :].astype(BF16)


def _dsa_prompt(q, qi, wi, kit, kbf, vbf):
    bsz, t, width = q.shape
    nkv = kbf.shape[2] // HEAD_DIM
    rep = width // HEAD_DIM // nkv
    qb = Q_BLOCK
    return pl.pallas_call(
        _dsa_prompt_kernel,
        grid=(bsz, t // qb),
        in_specs=[pl.BlockSpec((None, qb, width), lambda b, i: (b, i, 0)),
                  pl.BlockSpec((None, qb, qi.shape[2]), lambda b, i: (b, i, 0)),
                  pl.BlockSpec((None, qb, LANES), lambda b, i: (b, i, 0)),
                  pl.BlockSpec((IDX_DIM, t), lambda b, i: (0, b)),
                  pl.BlockSpec((None, t, kbf.shape[2]), lambda b, i: (b, 0, 0)),
                  pl.BlockSpec((None, t, vbf.shape[2]), lambda b, i: (b, 0, 0))],
        out_specs=pl.BlockSpec((None, qb, width), lambda b, i: (b, i, 0)),
        out_shape=jax.ShapeDtypeStruct((bsz, t, width), BF16),
        scratch_shapes=[pltpu.VMEM((qb, t), I32), pltpu.VMEM((qb, 1), I32),
                        pltpu.VMEM((nkv, rep * qb, 1), F32), pltpu.VMEM((nkv, rep * qb, 1), F32),
                        pltpu.VMEM((nkv, rep * qb, HEAD_DIM), F32)],
        compiler_params=_cparams("parallel", "arbitrary"),
        name="dsa_prompt",
    )(q, qi, wi, kit, kbf, vbf)


def _dsa_score_kernel(pt_ref, qi_ref, wi_ref, knew_ref, *rest):
    pages, (keys_ref, newkey_ref) = rest[:PAGES_PER_STEP], rest[PAGES_PER_STEP:]
    qi = qi_ref[...]
    wi = wi_ref[...]
    qi_bf = qi.astype(BF16)
    for p, page_ref in enumerate(pages):
        logits = _dot_nt(qi_bf, page_ref[...].astype(BF16))
        sc = jnp.sum(jnp.maximum(logits, 0.0) * wi, axis=0, keepdims=True)
        keys_ref[:, p * PAGE:(p + 1) * PAGE] = _sort_key(sc)
    logit_new = jnp.sum(qi * knew_ref[...], axis=1, keepdims=True)
    sc_new = jnp.sum(jnp.maximum(logit_new, 0.0) * wi, axis=0, keepdims=True)
    newkey_ref[...] = jnp.broadcast_to(_sort_key(sc_new), (1, LANES))


def _dsa_score(page_table, qi, wi, kidx_new, cache_kidx):
    bsz, npages = page_table.shape
    steps = npages // PAGES_PER_STEP
    page_spec = lambda p: pl.BlockSpec(
        (None, PAGE, IDX_DIM), lambda b, j, pt: (pt[b, j * PAGES_PER_STEP + p], 0, 0))
    per_seq = lambda shape: pl.BlockSpec((None,) + shape, lambda b, j, pt: (b, 0, 0))
    return pl.pallas_call(
        _dsa_score_kernel,
        grid_spec=pltpu.PrefetchScalarGridSpec(
            num_scalar_prefetch=1,
            grid=(bsz, steps),
            in_specs=[per_seq((IDX_HEADS, IDX_DIM)), per_seq((IDX_HEADS, 1)), per_seq((1, IDX_DIM))]
                     + [page_spec(p) for p in range(PAGES_PER_STEP)],
            out_specs=[pl.BlockSpec((None, 1, PAGES_PER_STEP * PAGE), lambda b, j, pt: (b, 0, j)),
                       per_seq((1, LANES))]),
        out_shape=[jax.ShapeDtypeStruct((bsz, 1, npages * PAGE), I32),
                   jax.ShapeDtypeStruct((bsz, 1, LANES), I32)],
        compiler_params=_cparams("parallel", "arbitrary"),
        name="dsa_decode_score",
    )(page_table, qi, wi, kidx_new, *([cache_kidx] * PAGES_PER_STEP))


def _dsa_select_kernel(keys_ref, thr_ref, j0_ref, j0_s):
    thr, j0 = _select_threshold(keys_ref, keys_ref.shape[1] // LANES, j0_s)
    thr_ref[...] = jnp.broadcast_to(thr, thr_ref.shape)
    j0_ref[...] = jnp.broadcast_to(j0, j0_ref.shape)


def _dsa_select(keys):
    rows, width = keys.shape
    const = lambda i: (0, 0)
    return pl.pallas_call(
        _dsa_select_kernel,
        grid=(1,),
        in_specs=[pl.BlockSpec((rows, width), const)],
        out_specs=[pl.BlockSpec((rows, LANES), const), pl.BlockSpec((rows, LANES), const)],
        out_shape=[jax.ShapeDtypeStruct((rows, LANES), I32), jax.ShapeDtypeStruct((rows, LANES), I32)],
        scratch_shapes=[pltpu.VMEM((rows, 1), I32)],
        compiler_params=_cparams("arbitrary"),
        name="dsa_decode_select",
    )(keys)


def _dsa_decode_kernel(pt_ref, q_ref, keys_ref, tail_ref, thr_ref, j0_ref, knew_ref, vnew_ref, *rest,
                       n_past):
    kpages = rest[:PAGES_PER_STEP]
    vpages = rest[PAGES_PER_STEP:2 * PAGES_PER_STEP]
    o_ref, m_s, l_s, acc_s = rest[2 * PAGES_PER_STEP:]
    j = pl.program_id(1)
    nheads, kvw = q_ref.shape
    nkv = kvw // HEAD_DIM
    rep = nheads // nkv

    @pl.when(j == 0)
    def _():
        m_s[...] = jnp.full_like(m_s, NEG_BIG)
        l_s[...] = jnp.zeros_like(l_s)
        acc_s[...] = jnp.zeros_like(acc_s)

    q = q_ref[...]
    q_bf = q.astype(BF16)
    thr = thr_ref[:, 0:1]
    j0 = j0_ref[:, 0:1]
    lane = lax.broadcasted_iota(I32, (1, PAGE), 1)

    def update(att, value_fn):
        m_old = m_s[...]
        m_new = jnp.maximum(m_old, jnp.max(att, axis=1, keepdims=True))
        alpha = jnp.exp(m_old - m_new)
        p = jnp.exp(att - m_new)
        l_s[...] = alpha * l_s[...] + jnp.sum(p, axis=1, keepdims=True)
        acc_s[...] = alpha * acc_s[...] + value_fn(p)
        m_s[...] = m_new

    for p in range(PAGES_PER_STEP):
        col0 = (j * PAGES_PER_STEP + p) * PAGE
        bias = _mask_bias(keys_ref[:, p * PAGE:(p + 1) * PAGE], lane + col0, thr, j0)
        att = _dot_nt(q_bf, kpages[p][...].astype(BF16)) + bias
        vpage = vpages[p][...].astype(BF16)
        update(att, lambda pr, vpage=vpage: _dot(pr.astype(BF16), vpage))

    @pl.when(j == pl.num_programs(1) - 1)
    def _():
        key_new = tail_ref[:, 0:1]
        bias_new = _mask_bias(key_new, jnp.int32(n_past), thr, j0)
        att_new = jnp.sum(q * knew_ref[...], axis=1, keepdims=True) + bias_new
        update(att_new, lambda pr: pr * vnew_ref[...])
        out = acc_s[...] / l_s[...]
        head_kv = lax.broadcasted_iota(I32, (nheads, HEAD_DIM), 0) // rep
        res = jnp.zeros((nheads, HEAD_DIM), F32)
        for g in range(nkv):
            res = jnp.where(head_kv == g, out[:, g * HEAD_DIM:(g + 1) * HEAD_DIM], res)
        o_ref[...] = res.astype(BF16)


def _dsa_decode(page_table, q_bd, keys, thr, j0, k_new, v_new, cache_k, cache_v):
    bsz, npages = page_table.shape
    nheads, kvw = q_bd.shape[1:]
    steps = npages // PAGES_PER_STEP
    n_past = npages * PAGE
    page_spec = lambda p: pl.BlockSpec(
        (None, PAGE, kvw), lambda b, j, pt: (pt[b, j * PAGES_PER_STEP + p], 0, 0))
    per_seq = lambda shape: pl.BlockSpec((None,) + shape, lambda b, j, pt: (b, 0, 0))
    keys_spec = pl.BlockSpec((None, 1, PAGES_PER_STEP * PAGE), lambda b, j, pt: (b, 0, j))
    tail_spec = pl.BlockSpec((None, 1, LANES), lambda b, j, pt: (b, 0, npages))
    return pl.pallas_call(
        functools.partial(_dsa_decode_kernel, n_past=n_past),
        grid_spec=pltpu.PrefetchScalarGridSpec(
            num_scalar_prefetch=1,
            grid=(bsz, steps),
            in_specs=[per_seq((nheads, kvw)), keys_spec, tail_spec, per_seq((1, LANES)), per_seq((1, LANES)),
                      per_seq((1, kvw)), per_seq((1, kvw))]
                     + [page_spec(p) for p in range(PAGES_PER_STEP)] * 2,
            out_specs=per_seq((nheads, HEAD_DIM)),
            scratch_shapes=[pltpu.VMEM((nheads, 1), F32), pltpu.VMEM((nheads, 1), F32),
                            pltpu.VMEM((nheads, kvw), F32)]),
        out_shape=jax.ShapeDtypeStruct((bsz, nheads, HEAD_DIM), BF16),
        compiler_params=_cparams("parallel", "arbitrary"),
        name="dsa_decode_attend",
    )(page_table, q_bd, keys, keys, thr, j0, k_new, v_new,
      *([cache_k] * PAGES_PER_STEP), *([cache_v] * PAGES_PER_STEP))


def _pad_cols(w, width):
    return jnp.pad(w, ((0, 0), (0, width - w.shape[1])))


def _block_diag(blocks):
    g, r, c = blocks.shape
    eye = jnp.eye(g, dtype=blocks.dtype)
    return (blocks[:, :, None, :] * eye[:, None, :, None]).reshape(g * r, g * c)


def _s5_params(lam_re, lam_im, log_dt, b_re, b_im, c_re, c_im, d_skip, w_glu, b_glu):
    lam = lax.complex(lam_re, lam_im)
    dt = jnp.exp(log_dt)[:, None]
    lam_bar = jnp.exp(lam * dt)
    b_bar = ((lam_bar - 1.0) / lam)[..., None] * lax.complex(b_re, b_im)
    lam_rows = jnp.stack([jnp.real(lam_bar).reshape(-1), jnp.imag(lam_bar).reshape(-1)])
    to_in = lambda z: _block_diag(jnp.swapaxes(z, 1, 2))
    wb = jnp.concatenate([to_in(jnp.real(b_bar)), to_in(jnp.imag(b_bar))], axis=1).astype(BF16)
    to_out = lambda z: _block_diag(jnp.swapaxes(z, 1, 2))
    wc = jnp.concatenate([to_out(c_re), to_out(-c_im)], axis=0).astype(BF16)
    return (lam_rows, wb, wc, d_skip[None, :], w_glu.astype(BF16), b_glu[None, :])


def kernel(x_prompt, x_sample, state_s5_re, state_s5_im, state_mlstm_c, state_mlstm_n, state_mlstm_m,
           cache_k, cache_v, cache_kidx, page_table, norm_mix, norm_mlp, norm_final, w_in0, s5_lam_re,
           s5_lam_im, s5_log_dt, s5_b_re, s5_b_im, s5_c_re, s5_c_im, s5_d, w_glu, b_glu, b_igate,
           b_fgate, w_out0, w_in1, w_out1, w_up, w_down):
    bp, tp, d = x_prompt.shape
    db, ts, _ = x_sample.shape
    assert ts == 1, "the decode path handles one new token per sequence"
    s5_groups, s5_state = s5_lam_re.shape
    s5_width = s5_groups * S5_GROUP
    nstate = s5_groups * s5_state
    nh = b_igate.shape[0]
    ml_width = nh * HEAD_DIM
    assert s5_width == ml_width == 512 and d == 1024
    n_past = page_table.shape[1] * PAGE
    kvw = cache_k.shape[2] * cache_k.shape[3]
    nheads = w_out1.shape[0] // HEAD_DIM

    gate_cols = s5_width + 4 * ml_width
    w0 = jnp.concatenate([w_in0[:, :gate_cols], _pad_cols(w_in0[:, gate_cols:], LANES)], axis=1).astype(BF16)
    s5p = _s5_params(s5_lam_re, s5_lam_im, s5_log_dt, s5_b_re, s5_b_im, s5_c_re, s5_c_im, s5_d, w_glu, b_glu)
    gate_bias = jnp.concatenate([b_igate, b_fgate])
    bias_row, bias_col = gate_bias[None, :], gate_bias[:, None]
    ki0 = 1024 + 2 * kvw + IDX_HEADS * IDX_DIM
    w1 = jnp.concatenate([w_in1[:, :ki0], _pad_cols(w_in1[:, ki0:ki0 + IDX_DIM], LANES),
                          _pad_cols(w_in1[:, ki0 + IDX_DIM:], LANES)], axis=1).astype(BF16)
    wkit = w_in1[:, ki0:ki0 + IDX_DIM].T.astype(BF16)
    wo0, wo1 = w_out0.astype(BF16), w_out1.astype(BF16)
    wup, wdn = w_up.astype(BF16), w_down.astype(BF16)
    g_mix, g_mlp, g_fin = norm_mix[:, None, :], norm_mlp[:, None, :], norm_final[None, :]

    def trunk(x2d, bsz, t, s5_h0, c0, n0, m0, attend):
        n = bsz * t
        z0 = _proj0(x2d, g_mix[0], w0)
        gates = z0[:, gate_cols:gate_cols + 2 * nh]
        m0b = jnp.broadcast_to(m0[:, :, None, None], (bsz, nh, 1, LANES))
        n0r = n0[:, :, None, :]
        if t == 1:
            y_s5, h_t = _s5_step(z0, s5_h0, s5p, s5_width)
            y_ml, c_t, n_t, m_t = _mlstm_step(z0.reshape(bsz, 1, -1), gates.reshape(bsz, 1, 2 * nh),
                                              bias_row, c0, n0r, m0b, nh)
        else:
            y_s5, h_t = _s5_seq(z0.reshape(bsz, t, -1), s5_h0.reshape(bsz, 1, -1), s5p, s5_width)
            g3 = gates.reshape(bsz, t, 2 * nh)
            y_ml, c_t, n_t, m_t = _mlstm_chunked(z0.reshape(bsz, t, -1), g3, jnp.swapaxes(g3, 1, 2),
                                                 bias_col, bias_row, c0, n0r, m0b, nh)
        h_t = h_t.reshape(bsz, 2, s5_groups, s5_state)
        states = (h_t[:, 0], h_t[:, 1], c_t, n_t[:, :, 0, :], m_t[:, :, 0, 0])
        h1 = _post(x2d, y_s5.reshape(n, s5_width), (y_ml.reshape(n, ml_width), 0), wo0, g_mlp[0],
                   wup[0], wdn[0], g_fin, final_norm=False)
        q, k, v, kbf, vbf, qi, kidx, wi, kit = _proj1(h1, g_mix[1], w1, wkit)
        o = attend(q, k, v, kbf, vbf, qi, kidx, wi, kit)
        y = _post(h1, o, (o, 1), wo1, g_mlp[1], wup[1], wdn[1], g_fin, final_norm=True)
        rows = (k.reshape(bsz, t, -1, HEAD_DIM), v.reshape(bsz, t, -1, HEAD_DIM), kidx.reshape(bsz, t, IDX_DIM))
        return y.reshape(bsz, t, d), states, rows

    def attend_prompt(q, k, v, kbf, vbf, qi, kidx, wi, kit):
        r3 = lambda z: z.reshape(bp, tp, -1)
        return _dsa_prompt(r3(q), r3(qi), r3(wi), kit, r3(kbf), r3(vbf)).reshape(bp * tp, -1)

    def attend_decode(q, k, v, kbf, vbf, qi, kidx, wi, kit):
        qi3 = qi.astype(F32).reshape(db, IDX_HEADS, IDX_DIM)
        wi3 = wi[:, :IDX_HEADS].reshape(db, IDX_HEADS, 1)
        keys, newkey = _dsa_score(page_table, qi3, wi3, kidx.reshape(db, 1, IDX_DIM), cache_kidx)
        tail = jnp.where(lax.broadcasted_iota(I32, (db, LANES), 1) == 0, newkey[:, 0, :], INT_MIN)
        keys = jnp.concatenate([keys[:, 0, :], tail], axis=1)
        thr, j0 = _dsa_select(keys)
        rep = nheads // (kvw // HEAD_DIM)
        q3 = q.astype(F32).reshape(db, nheads, HEAD_DIM)
        head_kv = jnp.arange(nheads) // rep
        q_bd = jnp.concatenate([jnp.where((head_kv == g)[None, :, None], q3, 0.0)
                                for g in range(kvw // HEAD_DIM)], axis=2)
        o = _dsa_decode(page_table, q_bd, keys[:, None, :], thr[:, None, :], j0[:, None, :],
                        k[:, None, :], v[:, None, :],
                        cache_k.reshape(-1, PAGE, kvw), cache_v.reshape(-1, PAGE, kvw))
        return o.reshape(db, nheads * HEAD_DIM)

    zeros = lambda *shape: jnp.zeros(shape, F32)
    y_p, st_p, rows_p = trunk(x_prompt.reshape(bp * tp, d), bp, tp, zeros(bp, 2 * nstate),
                              zeros(bp, nh, HEAD_DIM, HEAD_DIM), zeros(bp, nh, HEAD_DIM), zeros(bp, nh),
                              attend_prompt)
    s5_h0 = jnp.concatenate([state_s5_re.reshape(db, nstate), state_s5_im.reshape(db, nstate)], axis=1)
    y_s, st_s, rows_s = trunk(x_sample.reshape(db, d), db, 1, s5_h0, state_mlstm_c, state_mlstm_n,
                              state_mlstm_m, attend_decode)
    return (y_p, y_s) + st_p + rows_p + st_s + rows_s
```

```python
import functools
import math

import jax
import jax.numpy as jnp
from jax import lax
from jax.experimental import pallas as pl
from jax.experimental.pallas import tpu as pltpu

F32, BF16, I32 = jnp.float32, jnp.bfloat16, jnp.int32

EPS = 1e-6
LANES = 128
PAGE = 128
S5_GROUP = 16
S5_STATE = 64
HEAD_DIM = 128
IDX_DIM = 64
IDX_HEADS = 8
TOPK = 256
Q_BLOCK = 128
KEY_CHUNK = 512
MLSTM_CHUNK = 128
S5_CHUNK = 256
ROW_TILE = 512
FF_CHUNK = 1024
PAGES_PER_STEP = 8
INT_MIN = -2 ** 31
NEG_BIG = -1e30
VMEM_LIMIT = 56 * 1024 * 1024


def _cparams(*sem):
    return pltpu.CompilerParams(dimension_semantics=sem, vmem_limit_bytes=VMEM_LIMIT)


def _rms(x, g):
    return x * lax.rsqrt(jnp.mean(x * x, axis=-1, keepdims=True) + EPS) * g


def _dot(a, b):
    return jnp.dot(a, b, preferred_element_type=F32)


def _dot_nt(a, b):
    return lax.dot_general(a, b, (((1,), (1,)), ((), ())), preferred_element_type=F32)


def _dot_tn(a, b):
    return lax.dot_general(a, b, (((0,), (0,)), ((), ())), preferred_element_type=F32)


def _proj0_kernel(x_ref, g_ref, w_ref, o_ref):
    xn = _rms(x_ref[...], g_ref[...]).astype(BF16)
    o_ref[...] = _dot(xn, w_ref[...])


def _proj0(x, g, w):
    n, d = x.shape
    tm = min(ROW_TILE, n)
    wtot = w.shape[1]
    return pl.pallas_call(
        _proj0_kernel,
        grid=(n // tm,),
        in_specs=[pl.BlockSpec((tm, d), lambda i: (i, 0)),
                  pl.BlockSpec((1, d), lambda i: (0, 0)),
                  pl.BlockSpec((d, wtot), lambda i: (0, 0))],
        out_specs=pl.BlockSpec((tm, wtot), lambda i: (i, 0)),
        out_shape=jax.ShapeDtypeStruct((n, wtot), F32),
        compiler_params=_cparams("parallel"),
        name="proj0",
    )(x, g, w)


_Q1, _K1, _V1, _QI1, _KI1, _END1 = 0, 1024, 1280, 1536, 2048, 2176
LOG2E = 1.4426950408889634


def _proj1_kernel(x_ref, g_ref, w_ref, wvt_ref, wwit_ref, q_ref, k_ref, v_ref, kbf_ref, qi_ref,
                  kidx_ref, kidxbf_ref, vt_ref, wit_ref):
    xn = _rms(x_ref[...], g_ref[...]).astype(BF16)
    z = _dot(xn, w_ref[...])
    q_ref[...] = (z[:, _Q1:_K1] * (HEAD_DIM ** -0.5 * LOG2E)).astype(BF16)
    k = z[:, _K1:_V1]
    k_ref[...] = k
    v_ref[...] = z[:, _V1:_QI1]
    kbf_ref[...] = k.astype(BF16)
    qi_ref[...] = z[:, _QI1:_KI1].astype(BF16)
    kidx = z[:, _KI1:_KI1 + IDX_DIM]
    kidx_ref[...] = kidx
    kidxbf_ref[...] = kidx.astype(BF16)
    vt_ref[...] = _dot_nt(wvt_ref[...], xn).astype(BF16)
    wit_ref[...] = _dot_nt(wwit_ref[...], xn) * ((IDX_DIM ** -0.5) * (IDX_HEADS ** -0.5))


def _proj1(x, g, w, wvt, wwit):
    n, d = x.shape
    tm = min(ROW_TILE, n)
    kvw = wvt.shape[0]
    row = lambda width: pl.BlockSpec((tm, width), lambda i: (i, 0))
    col = lambda height: pl.BlockSpec((height, tm), lambda i: (0, i))
    full = lambda a: pl.BlockSpec(a.shape, lambda i: (0, 0))
    shp = lambda width, dt: jax.ShapeDtypeStruct((n, width), dt)
    return pl.pallas_call(
        _proj1_kernel,
        grid=(n // tm,),
        in_specs=[row(d), pl.BlockSpec((1, d), lambda i: (0, 0)), full(w), full(wvt), full(wwit)],
        out_specs=[row(1024), row(kvw), row(kvw), row(kvw), row(512), row(IDX_DIM), row(IDX_DIM),
                   col(kvw), col(IDX_HEADS)],
        out_shape=[shp(1024, BF16), shp(kvw, F32), shp(kvw, F32), shp(kvw, BF16), shp(512, BF16),
                   shp(IDX_DIM, F32), shp(IDX_DIM, BF16),
                   jax.ShapeDtypeStruct((kvw, n), BF16), jax.ShapeDtypeStruct((IDX_HEADS, n), F32)],
        compiler_params=_cparams("parallel"),
        name="proj1",
    )(x, g, w, wvt, wwit)


def _post_kernel(h_ref, ya_ref, yb_ref, wo_ref, g_ref, wup_ref, wdn_ref, gf_ref, out_ref,
                 h1_s, xn_s, acc_s, *, final_norm):
    j = pl.program_id(1)
    half = ya_ref.shape[1]

    @pl.when(j == 0)
    def _():
        h1 = h_ref[...] + _dot(ya_ref[...], wo_ref[:half, :]) + _dot(yb_ref[...], wo_ref[half:, :])
        h1_s[...] = h1
        xn_s[...] = _rms(h1, g_ref[...]).astype(BF16)
        acc_s[...] = jnp.zeros_like(acc_s)

    r = jnp.maximum(_dot(xn_s[...], wup_ref[...]), 0.0)
    acc_s[...] += _dot((r * r).astype(BF16), wdn_ref[...])

    @pl.when(j == pl.num_programs(1) - 1)
    def _():
        o = h1_s[...] + acc_s[...]
        if final_norm:
            o = _rms(o, gf_ref[...])
        out_ref[...] = o


def _post(h, ya, yb_spec_arg, wo, g, wup, wdn, gf, *, final_norm):
    n, d = h.shape
    tm = min(ROW_TILE, n)
    yb, yb_col = yb_spec_arg
    half = d // 2
    dff = wup.shape[1]
    return pl.pallas_call(
        functools.partial(_post_kernel, final_norm=final_norm),
        grid=(n // tm, dff // FF_CHUNK),
        in_specs=[pl.BlockSpec((tm, d), lambda i, j: (i, 0)),
                  pl.BlockSpec((tm, half), lambda i, j: (i, 0)),
                  pl.BlockSpec((tm, half), lambda i, j: (i, yb_col)),
                  pl.BlockSpec((d, d), lambda i, j: (0, 0)),
                  pl.BlockSpec((1, d), lambda i, j: (0, 0)),
                  pl.BlockSpec((d, FF_CHUNK), lambda i, j: (0, j)),
                  pl.BlockSpec((FF_CHUNK, d), lambda i, j: (j, 0)),
                  pl.BlockSpec((1, d), lambda i, j: (0, 0))],
        out_specs=pl.BlockSpec((tm, d), lambda i, j: (i, 0)),
        out_shape=jax.ShapeDtypeStruct((n, d), F32),
        scratch_shapes=[pltpu.VMEM((tm, d), F32), pltpu.VMEM((tm, d), BF16), pltpu.VMEM((tm, d), F32)],
        compiler_params=_cparams("parallel", "arbitrary"),
        name="post_final" if final_norm else "post",
    )(h, ya, yb, wo, g, wup, wdn, gf)


def _s5_output(hre, him, u, wc_ref, d_ref, wg_ref, bg_ref):
    half = hre.shape[1]
    y = _dot(hre.astype(BF16), wc_ref[:half, :]) + _dot(him.astype(BF16), wc_ref[half:, :])
    y = jax.nn.gelu(y + d_ref[...] * u)
    gate = jax.nn.sigmoid(_dot(y.astype(BF16), wg_ref[...]) + bg_ref[...])
    return (y * gate).astype(BF16)


def _s5_seq_kernel(u_ref, h0_ref, lam_ref, wb_ref, wc_ref, d_ref, wg_ref, bg_ref, y_ref, ht_ref,
                   hre_s, him_s, carry_s):
    half = hre_s.shape[1]

    @pl.when(pl.program_id(1) == 0)
    def _():
        carry_s[...] = h0_ref[...]

    u = u_ref[...]
    bu = _dot(u.astype(BF16), wb_ref[...])
    hre_s[...] = bu[:, :half]
    him_s[...] = bu[:, half:]
    a_re = lam_ref[0:1, :]
    a_im = lam_ref[1:2, :]

    def step(t, carry):
        h_re, h_im = carry
        n_re = a_re * h_re - a_im * h_im + hre_s[pl.ds(t, 1), :]
        n_im = a_re * h_im + a_im * h_re + him_s[pl.ds(t, 1), :]
        hre_s[pl.ds(t, 1), :] = n_re
        him_s[pl.ds(t, 1), :] = n_im
        return n_re, n_im

    h_re, h_im = lax.fori_loop(0, hre_s.shape[0], step, (carry_s[:, :half], carry_s[:, half:]), unroll=8)
    carry_s[:, :half] = h_re
    carry_s[:, half:] = h_im
    ht_ref[...] = carry_s[...]
    y_ref[...] = _s5_output(hre_s[...], him_s[...], u, wc_ref, d_ref, wg_ref, bg_ref)


def _s5_step_kernel(u_ref, h0_ref, lam_ref, wb_ref, wc_ref, d_ref, wg_ref, bg_ref, y_ref, ht_ref):
    half = lam_ref.shape[1]
    u = u_ref[...]
    bu = _dot(u.astype(BF16), wb_ref[...])
    a_re = lam_ref[0:1, :]
    a_im = lam_ref[1:2, :]
    h_re = h0_ref[:, :half]
    h_im = h0_ref[:, half:]
    n_re = a_re * h_re - a_im * h_im + bu[:, :half]
    n_im = a_re * h_im + a_im * h_re + bu[:, half:]
    ht_ref[:, :half] = n_re
    ht_ref[:, half:] = n_im
    y_ref[...] = _s5_output(n_re, n_im, u, wc_ref, d_ref, wg_ref, bg_ref)


def _s5_param_specs(width, nstate, imap):
    return [pl.BlockSpec((2, nstate), imap),
            pl.BlockSpec((width, 2 * nstate), imap),
            pl.BlockSpec((2 * nstate, width), imap),
            pl.BlockSpec((1, width), imap),
            pl.BlockSpec((width, width), imap),
            pl.BlockSpec((1, width), imap)]


def _s5_seq(z0, h0, params, width):
    bsz, t, _ = z0.shape
    nstate = params[0].shape[1]
    ts = min(S5_CHUNK, t)
    const = lambda b, c: (0, 0)
    return pl.pallas_call(
        _s5_seq_kernel,
        grid=(bsz, t // ts),
        in_specs=[pl.BlockSpec((None, ts, width), lambda b, c: (b, c, 0)),
                  pl.BlockSpec((None, 1, 2 * nstate), lambda b, c: (b, 0, 0))]
                 + _s5_param_specs(width, nstate, const),
        out_specs=[pl.BlockSpec((None, ts, width), lambda b, c: (b, c, 0)),
                   pl.BlockSpec((None, 1, 2 * nstate), lambda b, c: (b, 0, 0))],
        out_shape=[jax.ShapeDtypeStruct((bsz, t, width), BF16),
                   jax.ShapeDtypeStruct((bsz, 1, 2 * nstate), F32)],
        scratch_shapes=[pltpu.VMEM((ts, nstate), F32), pltpu.VMEM((ts, nstate), F32),
                        pltpu.VMEM((1, 2 * nstate), F32)],
        compiler_params=_cparams("parallel", "arbitrary"),
        name="s5_scan",
    )(z0, h0, *params)


def _s5_step(z0, h0, params, width):
    rows = z0.shape[0]
    nstate = params[0].shape[1]
    const = lambda i: (0, 0)
    return pl.pallas_call(
        _s5_step_kernel,
        grid=(1,),
        in_specs=[pl.BlockSpec((rows, width), const), pl.BlockSpec((rows, 2 * nstate), const)]
                 + _s5_param_specs(width, nstate, const),
        out_specs=[pl.BlockSpec((rows, width), const), pl.BlockSpec((rows, 2 * nstate), const)],
        out_shape=[jax.ShapeDtypeStruct((rows, width), BF16),
                   jax.ShapeDtypeStruct((rows, 2 * nstate), F32)],
        compiler_params=_cparams("arbitrary"),
        name="s5_step",
    )(z0, h0, *params)


def _mlstm_chunk_kernel(q_ref, k_ref, v_ref, o_ref, gcol_ref, grow_ref, bcol_ref, brow_ref,
                        c0_ref, n0_ref, m0_ref, y_ref, c_ref, n_ref, m_ref):
    nh = c_ref.shape[0]
    ch = q_ref.shape[0]

    @pl.when(pl.program_id(1) == 0)
    def _():
        c_ref[...] = c0_ref[...]
        n_ref[...] = n0_ref[...]
        m_ref[...] = m0_ref[...]

    gcol = gcol_ref[...] + brow_ref[...]
    grow = grow_ref[...] + bcol_ref[...]
    t_idx = lax.broadcasted_iota(I32, (ch, ch), 0)
    s_idx = lax.broadcasted_iota(I32, (ch, ch), 1)
    causal = t_idx >= s_idx
    for h in range(nh):
        sl = slice(h * HEAD_DIM, (h + 1) * HEAD_DIM)
        q = q_ref[:, sl]
        k = k_ref[:, sl] * (HEAD_DIM ** -0.5)
        v = v_ref[:, sl]
        q_bf, k_bf, v_bf = q.astype(BF16), k.astype(BF16), v.astype(BF16)
        i_col = gcol[:, h:h + 1]
        i_row = grow[h:h + 1, :]
        lf_col = jax.nn.log_sigmoid(gcol[:, nh + h:nh + h + 1])
        lf_row = jax.nn.log_sigmoid(grow[nh + h:nh + h + 1, :])
        b_col = jnp.sum(jnp.where(causal, lf_row, 0.0), axis=1, keepdims=True)
        b_row = jnp.sum(jnp.where(causal, 0.0, lf_col) , axis=0, keepdims=True)
        b_row = b_row + lf_row
        m_prev = m_ref[h][:, 0:1]
        dmat = jnp.where(causal, b_col - b_row + i_row, -jnp.inf)
        a_col = b_col + m_prev
        mj = jnp.maximum(a_col, jnp.max(dmat, axis=1, keepdims=True))
        w_intra = jnp.exp(dmat - mj)
        w_inter = jnp.exp(a_col - mj)
        s = _dot_nt(q_bf, k_bf) * w_intra
        c_prev = c_ref[h]
        n_prev = n_ref[h]
        num = _dot(s.astype(BF16), v_bf) + w_inter * _dot(q_bf, c_prev.astype(BF16))
        den = jnp.sum(s, axis=1, keepdims=True) + w_inter * jnp.sum(q * n_prev, axis=1, keepdims=True)
        hout = num / jnp.maximum(jnp.abs(den), jnp.exp(-mj))
        y_ref[:, sl] = (jax.nn.sigmoid(o_ref[:, sl]) * hout).astype(BF16)
        m_new = mj[ch - 1:ch, :]
        b_last = b_col[ch - 1:ch, :]
        w_end = jnp.exp(b_last - b_col + i_col - m_new)
        decay = jnp.exp(b_last + m_prev - m_new)
        kw = k * w_end
        c_ref[h] = decay * c_prev + _dot_tn(kw.astype(BF16), v_bf)
        n_ref[h] = decay * n_prev + jnp.sum(kw, axis=0, keepdims=True)
        m_ref[h] = jnp.broadcast_to(m_new, (1, LANES))


def _mlstm_chunked(z0, gcol, grow, bias_col, bias_row, c0, n0, m0, nh):
    bsz, t, _ = z0.shape
    width = nh * HEAD_DIM
    ch = MLSTM_CHUNK
    zspec = lambda blk: pl.BlockSpec((None, ch, width), lambda b, c: (b, c, blk))
    state = lambda shape: pl.BlockSpec((None,) + shape, lambda b, c: (b,) + (0,) * len(shape))
    return pl.pallas_call(
        _mlstm_chunk_kernel,
        grid=(bsz, t // ch),
        in_specs=[zspec(1), zspec(2), zspec(3), zspec(4),
                  pl.BlockSpec((None, ch, 2 * nh), lambda b, c: (b, c, 0)),
                  pl.BlockSpec((None, 2 * nh, ch), lambda b, c: (b, 0, c)),
                  pl.BlockSpec((2 * nh, 1), lambda b, c: (0, 0)),
                  pl.BlockSpec((1, 2 * nh), lambda b, c: (0, 0)),
                  state((nh, HEAD_DIM, HEAD_DIM)), state((nh, 1, HEAD_DIM)), state((nh, 1, LANES))],
        out_specs=[pl.BlockSpec((None, ch, width), lambda b, c: (b, c, 0)),
                   state((nh, HEAD_DIM, HEAD_DIM)), state((nh, 1, HEAD_DIM)), state((nh, 1, LANES))],
        out_shape=[jax.ShapeDtypeStruct((bsz, t, width), BF16),
                   jax.ShapeDtypeStruct((bsz, nh, HEAD_DIM, HEAD_DIM), F32),
                   jax.ShapeDtypeStruct((bsz, nh, 1, HEAD_DIM), F32),
                   jax.ShapeDtypeStruct((bsz, nh, 1, LANES), F32)],
        compiler_params=_cparams("parallel", "arbitrary"),
        name="mlstm_chunk",
    )(z0, z0, z0, z0, gcol, grow, bias_col, bias_row, c0, n0, m0)


def _to_column(row):
    n = row.shape[1]
    eye = lax.broadcasted_iota(I32, (n, n), 0) == lax.broadcasted_iota(I32, (n, n), 1)
    return jnp.sum(jnp.where(eye, row, 0.0), axis=1, keepdims=True)


def _mlstm_step_kernel(q_ref, k_ref, v_ref, o_ref, g_ref, brow_ref, c0_ref, n0_ref, m0_ref,
                       y_ref, c_ref, n_ref, m_ref):
    nh = c_ref.shape[0]
    g = g_ref[...] + brow_ref[...]
    for h in range(nh):
        sl = slice(h * HEAD_DIM, (h + 1) * HEAD_DIM)
        q = q_ref[:, sl]
        k = k_ref[:, sl] * (HEAD_DIM ** -0.5)
        v = v_ref[:, sl]
        i_pre = g[:, h:h + 1]
        lf = jax.nn.log_sigmoid(g[:, nh + h:nh + h + 1])
        m_prev = m0_ref[h][:, 0:1]
        c_prev = c0_ref[h]
        n_prev = n0_ref[h]
        a = lf + m_prev
        mj = jnp.maximum(a, i_pre)
        w_intra = jnp.exp(i_pre - mj)
        w_inter = jnp.exp(a - mj)
        s = jnp.sum(q * k, axis=1, keepdims=True) * w_intra
        q_col = _to_column(q)
        k_col = _to_column(k)
        num = s * v + w_inter * jnp.sum(q_col * c_prev, axis=0, keepdims=True)
        den = s + w_inter * jnp.sum(q * n_prev, axis=1, keepdims=True)
        hout = num / jnp.maximum(jnp.abs(den), jnp.exp(-mj))
        y_ref[:, sl] = (jax.nn.sigmoid(o_ref[:, sl]) * hout).astype(BF16)
        w_end = jnp.exp(i_pre - mj)
        decay = jnp.exp(a - mj)
        c_ref[h] = decay * c_prev + (w_end * k_col) * v
        n_ref[h] = decay * n_prev + w_end * k
        m_ref[h] = jnp.broadcast_to(mj, (1, LANES))


def _mlstm_step(z0, g, bias_row, c0, n0, m0, nh):
    bsz = z0.shape[0]
    width = nh * HEAD_DIM
    zspec = lambda blk: pl.BlockSpec((None, 1, width), lambda b: (b, 0, blk))
    state = lambda shape: pl.BlockSpec((None,) + shape, lambda b: (b,) + (0,) * len(shape))
    return pl.pallas_call(
        _mlstm_step_kernel,
        grid=(bsz,),
        in_specs=[zspec(1), zspec(2), zspec(3), zspec(4),
                  pl.BlockSpec((None, 1, 2 * nh), lambda b: (b, 0, 0)),
                  pl.BlockSpec((1, 2 * nh), lambda b: (0, 0)),
                  state((nh, HEAD_DIM, HEAD_DIM)), state((nh, 1, HEAD_DIM)), state((nh, 1, LANES))],
        out_specs=[pl.BlockSpec((None, 1, width), lambda b: (b, 0, 0)),
                   state((nh, HEAD_DIM, HEAD_DIM)), state((nh, 1, HEAD_DIM)), state((nh, 1, LANES))],
        out_shape=[jax.ShapeDtypeStruct((bsz, 1, width), BF16),
                   jax.ShapeDtypeStruct((bsz, nh, HEAD_DIM, HEAD_DIM), F32),
                   jax.ShapeDtypeStruct((bsz, nh, 1, HEAD_DIM), F32),
                   jax.ShapeDtypeStruct((bsz, nh, 1, LANES), F32)],
        compiler_params=_cparams("parallel"),
        name="mlstm_step",
    )(z0, z0, z0, z0, g, bias_row, c0, n0, m0)


def _key_to_float(key):
    bits = key ^ ((key >> 31) & jnp.int32(0x7FFFFFFF))
    return lax.bitcast_convert_type(bits, F32)


def _select_threshold(count, shape, width, j0_s):
    def bit_step(it, key):
        cand = key + (jnp.int32(1) << (31 - it))
        cand_f = _key_to_float(cand)
        cnt = count(lambda tile, col: tile >= cand_f)
        return jnp.where(cnt >= TOPK, cand, key)

    key = lax.fori_loop(0, 32, bit_step, jnp.full(shape, INT_MIN, I32))
    has_thr = key > INT_MIN
    thr = jnp.where(has_thr, _key_to_float(jnp.where(has_thr, key, 0)), -jnp.inf)
    need = TOPK - count(lambda tile, col: tile > thr)
    n_eq = count(lambda tile, col: tile == thr)
    j0_s[...] = jnp.where(has_thr, jnp.int32(width), jnp.int32(-1))
    surplus = jnp.max(jnp.where(has_thr & (n_eq > need), 1, 0))
    nbits = max(1, (width - 1).bit_length())

    @pl.when(surplus > 0)
    def _():
        def idx_step(it, j0):
            cand = j0 | (jnp.int32(1) << (nbits - 1 - it))
            cnt = count(lambda tile, col: (tile == thr) & (col < cand))
            return jnp.where(cnt < need, cand, j0)

        j0 = lax.fori_loop(0, nbits, idx_step, jnp.zeros(shape, I32))
        j0_s[...] = jnp.where(has_thr, j0, jnp.int32(-1))

    return thr, j0_s[...]


def _mask_bias(scores, cols, thr, j0):
    sel = (scores > thr) | ((scores == thr) & (cols <= j0))
    return jnp.where(sel, 0.0, NEG_BIG)


def _dsa_prompt_kernel(q_ref, qi_ref, wit_ref, ki_ref, k_ref, vt_ref, o_ref, sc_s, j0_s, m_s, acc_s):
    qb = q_ref.shape[0]
    kc_len = KEY_CHUNK
    nheads = q_ref.shape[1] // HEAD_DIM
    nkv = k_ref.shape[1] // HEAD_DIM
    rep = nheads // nkv
    i = pl.program_id(1)
    nchunks = ((i + 1) * qb + kc_len - 1) // kc_len
    qpos = i * qb + lax.broadcasted_iota(I32, (1, qb), 1)
    sub = lax.broadcasted_iota(I32, (kc_len, qb), 0)

    qi = qi_ref[...]
    wit = wit_ref[...]
    qi_h = [qi[:, h * IDX_DIM:(h + 1) * IDX_DIM] for h in range(IDX_HEADS)]

    def score_chunk(c, _):
        off = pl.multiple_of(c * kc_len, kc_len)
        ki = ki_ref[pl.ds(off, kc_len), :]
        sc = jnp.zeros((kc_len, qb), F32)
        for h in range(IDX_HEADS):
            sc = sc + jnp.maximum(_dot_nt(ki, qi_h[h]), 0.0) * wit[h:h + 1, :]
        sc_s[pl.ds(off, kc_len), :] = jnp.where(sub + off <= qpos, sc, -jnp.inf)
        return 0

    lax.fori_loop(0, nchunks, score_chunk, 0)

    def count(pred):
        def body(c, acc):
            off = pl.multiple_of(c * kc_len, kc_len)
            ind = jnp.where(pred(sc_s[pl.ds(off, kc_len), :], sub + off), 1, 0)
            return acc + jnp.sum(ind.reshape(kc_len // 8, 8, qb), axis=0)

        acc = lax.fori_loop(0, nchunks, body, jnp.zeros((8, qb), I32))
        return jnp.sum(acc, axis=0, keepdims=True)

    thr, j0 = _select_threshold(count, (1, qb), sc_s.shape[0], j0_s)

    m_s[...] = jnp.full_like(m_s, NEG_BIG)
    acc_s[...] = jnp.zeros_like(acc_s)
    q = q_ref[...]
    q_g = [jnp.concatenate([q[:, (g * rep + r) * HEAD_DIM:(g * rep + r + 1) * HEAD_DIM]
                            for r in range(rep)], axis=0) for g in range(nkv)]
    ones_rows = jnp.ones((acc_s.shape[1] - HEAD_DIM, kc_len), BF16)

    def attend_chunk(c, _):
        off = pl.multiple_of(c * kc_len, kc_len)
        bias = _mask_bias(sc_s[pl.ds(off, kc_len), :], sub + off, thr, j0)
        bias = jnp.concatenate([bias] * rep, axis=1)
        for g in range(nkv):
            kc = k_ref[pl.ds(off, kc_len), g * HEAD_DIM:(g + 1) * HEAD_DIM]
            att = _dot_nt(kc, q_g[g]) + bias
            m_old = m_s[g]
            m_new = jnp.maximum(m_old, jnp.max(att, axis=0, keepdims=True))
            p = jnp.exp2(att - m_new).astype(BF16)
            vt = jnp.concatenate([vt_ref[g * HEAD_DIM:(g + 1) * HEAD_DIM, pl.ds(off, kc_len)], ones_rows],
                                 axis=0)
            acc_s[g] = jnp.exp2(m_old - m_new) * acc_s[g] + _dot(vt, p)
            m_s[g] = m_new
        return 0

    lax.fori_loop(0, nchunks, attend_chunk, 0)

    for g in range(nkv):
        acc = acc_s[g]
        out = acc[:HEAD_DIM, :] / acc[HEAD_DIM:HEAD_DIM + 1, :]
        for r in range(rep):
            hd = g * rep + r
            o_ref[:, hd * HEAD_DIM:(hd + 1) * HEAD_DIM] = out[:, r * qb:(r + 1) * qb].T.astype(BF16)


def _dsa_prompt(q, qi, wit, kidx, kbf, vt):
    bsz, t, width = q.shape
    kvw = kbf.shape[2]
    nkv = kvw // HEAD_DIM
    rep = width // HEAD_DIM // nkv
    qb = Q_BLOCK
    nq = t // qb
    ones_rows = 16
    return pl.pallas_call(
        _dsa_prompt_kernel,
        grid=(bsz, nq),
        in_specs=[pl.BlockSpec((None, qb, width), lambda b, i: (b, i, 0)),
                  pl.BlockSpec((None, qb, qi.shape[2]), lambda b, i: (b, i, 0)),
                  pl.BlockSpec((IDX_HEADS, qb), lambda b, i: (0, b * nq + i)),
                  pl.BlockSpec((None, t, IDX_DIM), lambda b, i: (b, 0, 0)),
                  pl.BlockSpec((None, t, kvw), lambda b, i: (b, 0, 0)),
                  pl.BlockSpec((kvw, t), lambda b, i: (0, b))],
        out_specs=pl.BlockSpec((None, qb, width), lambda b, i: (b, i, 0)),
        out_shape=jax.ShapeDtypeStruct((bsz, t, width), BF16),
        scratch_shapes=[pltpu.VMEM((t, qb), F32), pltpu.VMEM((1, qb), I32),
                        pltpu.VMEM((nkv, 1, rep * qb), F32),
                        pltpu.VMEM((nkv, HEAD_DIM + ones_rows, rep * qb), F32)],
        compiler_params=_cparams("parallel", "arbitrary"),
        name="dsa_prompt",
    )(q, qi, wit, kidx, kbf, vt)


def _dsa_score_kernel(pt_ref, qi_ref, wi_ref, knew_ref, *rest):
    pages, (keys_ref, newkey_ref) = rest[:PAGES_PER_STEP], rest[PAGES_PER_STEP:]
    qi = qi_ref[...]
    wi = wi_ref[...]
    qi_bf = qi.astype(BF16)
    for p, page_ref in enumerate(pages):
        logits = _dot_nt(qi_bf, page_ref[...].astype(BF16))
        sc = jnp.sum(jnp.maximum(logits, 0.0) * wi, axis=0, keepdims=True)
        keys_ref[:, p * PAGE:(p + 1) * PAGE] = sc
    logit_new = jnp.sum(qi * knew_ref[...], axis=1, keepdims=True)
    sc_new = jnp.sum(jnp.maximum(logit_new, 0.0) * wi, axis=0, keepdims=True)
    newkey_ref[...] = jnp.broadcast_to(sc_new, (1, LANES))


def _dsa_score(page_table, qi, wi, kidx_new, cache_kidx):
    bsz, npages = page_table.shape
    steps = npages // PAGES_PER_STEP
    page_spec = lambda p: pl.BlockSpec(
        (None, PAGE, IDX_DIM), lambda b, j, pt: (pt[b, j * PAGES_PER_STEP + p], 0, 0))
    per_seq = lambda shape: pl.BlockSpec((None,) + shape, lambda b, j, pt: (b, 0, 0))
    return pl.pallas_call(
        _dsa_score_kernel,
        grid_spec=pltpu.PrefetchScalarGridSpec(
            num_scalar_prefetch=1,
            grid=(bsz, steps),
            in_specs=[per_seq((IDX_HEADS, IDX_DIM)), per_seq((IDX_HEADS, 1)), per_seq((1, IDX_DIM))]
                     + [page_spec(p) for p in range(PAGES_PER_STEP)],
            out_specs=[pl.BlockSpec((None, 1, PAGES_PER_STEP * PAGE), lambda b, j, pt: (b, 0, j)),
                       per_seq((1, LANES))]),
        out_shape=[jax.ShapeDtypeStruct((bsz, 1, npages * PAGE), F32),
                   jax.ShapeDtypeStruct((bsz, 1, LANES), F32)],
        compiler_params=_cparams("parallel", "arbitrary"),
        name="dsa_decode_score",
    )(page_table, qi, wi, kidx_new, *([cache_kidx] * PAGES_PER_STEP))


def _dsa_select_kernel(keys_ref, thr_ref, j0_ref, j0_s):
    rows, width = keys_ref.shape
    lane = lax.broadcasted_iota(I32, (rows, LANES), 1)

    def count(pred):
        def body(t, acc):
            off = pl.multiple_of(t * LANES, LANES)
            return acc + jnp.where(pred(keys_ref[:, pl.ds(off, LANES)], lane + off), 1, 0)

        acc = lax.fori_loop(0, width // LANES, body, jnp.zeros((rows, LANES), I32))
        return jnp.sum(acc, axis=1, keepdims=True)

    thr, j0 = _select_threshold(count, (rows, 1), width, j0_s)
    thr_ref[...] = jnp.broadcast_to(thr, thr_ref.shape)
    j0_ref[...] = jnp.broadcast_to(j0, j0_ref.shape)


def _dsa_select(keys):
    rows, width = keys.shape
    const = lambda i: (0, 0)
    return pl.pallas_call(
        _dsa_select_kernel,
        grid=(1,),
        in_specs=[pl.BlockSpec((rows, width), const)],
        out_specs=[pl.BlockSpec((rows, LANES), const), pl.BlockSpec((rows, LANES), const)],
        out_shape=[jax.ShapeDtypeStruct((rows, LANES), F32), jax.ShapeDtypeStruct((rows, LANES), I32)],
        scratch_shapes=[pltpu.VMEM((rows, 1), I32)],
        compiler_params=_cparams("arbitrary"),
        name="dsa_decode_select",
    )(keys)


def _dsa_decode_kernel(pt_ref, q_ref, keys_ref, tail_ref, thr_ref, j0_ref, knew_ref, vnew_ref, *rest,
                       n_past):
    kpages = rest[:PAGES_PER_STEP]
    vpages = rest[PAGES_PER_STEP:2 * PAGES_PER_STEP]
    o_ref, m_s, l_s, acc_s = rest[2 * PAGES_PER_STEP:]
    j = pl.program_id(1)
    nheads, kvw = q_ref.shape
    nkv = kvw // HEAD_DIM
    rep = nheads // nkv

    @pl.when(j == 0)
    def _():
        m_s[...] = jnp.full_like(m_s, NEG_BIG)
        l_s[...] = jnp.zeros_like(l_s)
        acc_s[...] = jnp.zeros_like(acc_s)

    q = q_ref[...]
    q_bf = q.astype(BF16)
    thr = thr_ref[:, 0:1]
    j0 = j0_ref[:, 0:1]
    lane = lax.broadcasted_iota(I32, (1, PAGE), 1)

    def update(att, value_fn):
        m_old = m_s[...]
        m_new = jnp.maximum(m_old, jnp.max(att, axis=1, keepdims=True))
        alpha = jnp.exp2(m_old - m_new)
        p = jnp.exp2(att - m_new)
        l_s[...] = alpha * l_s[...] + jnp.sum(p, axis=1, keepdims=True)
        acc_s[...] = alpha * acc_s[...] + value_fn(p)
        m_s[...] = m_new

    def page_rows(page_ref):
        return jnp.concatenate([page_ref[:, g, :] for g in range(nkv)], axis=1).astype(BF16)

    for p in range(PAGES_PER_STEP):
        col0 = (j * PAGES_PER_STEP + p) * PAGE
        bias = _mask_bias(keys_ref[:, p * PAGE:(p + 1) * PAGE], lane + col0, thr, j0)
        att = _dot_nt(q_bf, page_rows(kpages[p])) + bias
        vpage = page_rows(vpages[p])
        update(att, lambda pr, vpage=vpage: _dot(pr.astype(BF16), vpage))

    @pl.when(j == pl.num_programs(1) - 1)
    def _():
        key_new = tail_ref[:, 0:1]
        bias_new = _mask_bias(key_new, jnp.int32(n_past), thr, j0)
        att_new = jnp.sum(q * knew_ref[...], axis=1, keepdims=True) + bias_new
        update(att_new, lambda pr: pr * vnew_ref[...])
        out = acc_s[...] / l_s[...]
        head_kv = lax.broadcasted_iota(I32, (nheads, HEAD_DIM), 0) // rep
        res = jnp.zeros((nheads, HEAD_DIM), F32)
        for g in range(nkv):
            res = jnp.where(head_kv == g, out[:, g * HEAD_DIM:(g + 1) * HEAD_DIM], res)
        o_ref[...] = res.astype(BF16)


def _dsa_decode(page_table, q_bd, keys, thr, j0, k_new, v_new, cache_k, cache_v):
    bsz, npages = page_table.shape
    nheads, kvw = q_bd.shape[1:]
    steps = npages // PAGES_PER_STEP
    n_past = npages * PAGE
    page_spec = lambda p: pl.BlockSpec(
        (None, PAGE, kvw // HEAD_DIM, HEAD_DIM), lambda b, j, pt: (pt[b, j * PAGES_PER_STEP + p], 0, 0, 0))
    per_seq = lambda shape: pl.BlockSpec((None,) + shape, lambda b, j, pt: (b, 0, 0))
    keys_spec = pl.BlockSpec((None, 1, PAGES_PER_STEP * PAGE), lambda b, j, pt: (b, 0, j))
    tail_spec = pl.BlockSpec((None, 1, LANES), lambda b, j, pt: (b, 0, npages))
    return pl.pallas_call(
        functools.partial(_dsa_decode_kernel, n_past=n_past),
        grid_spec=pltpu.PrefetchScalarGridSpec(
            num_scalar_prefetch=1,
            grid=(bsz, steps),
            in_specs=[per_seq((nheads, kvw)), keys_spec, tail_spec, per_seq((1, LANES)), per_seq((1, LANES)),
                      per_seq((1, kvw)), per_seq((1, kvw))]
                     + [page_spec(p) for p in range(PAGES_PER_STEP)] * 2,
            out_specs=per_seq((nheads, HEAD_DIM)),
            scratch_shapes=[pltpu.VMEM((nheads, 1), F32), pltpu.VMEM((nheads, 1), F32),
                            pltpu.VMEM((nheads, kvw), F32)]),
        out_shape=jax.ShapeDtypeStruct((bsz, nheads, HEAD_DIM), BF16),
        compiler_params=_cparams("parallel", "arbitrary"),
        name="dsa_decode_attend",
    )(page_table, q_bd, keys, keys, thr, j0, k_new, v_new,
      *([cache_k] * PAGES_PER_STEP), *([cache_v] * PAGES_PER_STEP))


def _pad_cols(w, width):
    return jnp.pad(w, ((0, 0), (0, width - w.shape[1])))


def _block_diag(blocks):
    g, r, c = blocks.shape
    eye = jnp.eye(g, dtype=blocks.dtype)
    return (blocks[:, :, None, :] * eye[:, None, :, None]).reshape(g * r, g * c)


def _s5_params(lam_re, lam_im, log_dt, b_re, b_im, c_re, c_im, d_skip, w_glu, b_glu):
    dt = jnp.exp(log_dt)[:, None]
    mag = jnp.exp(lam_re * dt)
    bar_re = mag * jnp.cos(lam_im * dt)
    bar_im = mag * jnp.sin(lam_im * dt)
    inv = 1.0 / (lam_re * lam_re + lam_im * lam_im)
    coef_re = (((bar_re - 1.0) * lam_re + bar_im * lam_im) * inv)[..., None]
    coef_im = ((bar_im * lam_re - (bar_re - 1.0) * lam_im) * inv)[..., None]
    bb_re = coef_re * b_re - coef_im * b_im
    bb_im = coef_re * b_im + coef_im * b_re
    lam_rows = jnp.stack([bar_re.reshape(-1), bar_im.reshape(-1)])
    to_in = lambda z: _block_diag(jnp.swapaxes(z, 1, 2))
    wb = jnp.concatenate([to_in(bb_re), to_in(bb_im)], axis=1).astype(BF16)
    to_out = lambda z: _block_diag(jnp.swapaxes(z, 1, 2))
    wc = jnp.concatenate([to_out(c_re), to_out(-c_im)], axis=0).astype(BF16)
    return (lam_rows, wb, wc, d_skip[None, :], w_glu.astype(BF16), b_glu[None, :])


def kernel(x_prompt, x_sample, state_s5_re, state_s5_im, state_mlstm_c, state_mlstm_n, state_mlstm_m,
           cache_k, cache_v, cache_kidx, page_table, norm_mix, norm_mlp, norm_final, w_in0, s5_lam_re,
           s5_lam_im, s5_log_dt, s5_b_re, s5_b_im, s5_c_re, s5_c_im, s5_d, w_glu, b_glu, b_igate,
           b_fgate, w_out0, w_in1, w_out1, w_up, w_down):
    bp, tp, d = x_prompt.shape
    db, ts, _ = x_sample.shape
    assert ts == 1, "the decode path handles one new token per sequence"
    s5_groups, s5_state = s5_lam_re.shape
    s5_width = s5_groups * S5_GROUP
    nstate = s5_groups * s5_state
    nh = b_igate.shape[0]
    ml_width = nh * HEAD_DIM
    assert s5_width == ml_width == 512 and d == 1024
    n_past = page_table.shape[1] * PAGE
    kvw = cache_k.shape[2] * cache_k.shape[3]
    nheads = w_out1.shape[0] // HEAD_DIM

    gate_cols = s5_width + 4 * ml_width
    w0 = jnp.concatenate([w_in0[:, :gate_cols], _pad_cols(w_in0[:, gate_cols:], LANES)], axis=1).astype(BF16)
    s5p = _s5_params(s5_lam_re, s5_lam_im, s5_log_dt, s5_b_re, s5_b_im, s5_c_re, s5_c_im, s5_d, w_glu, b_glu)
    gate_bias = jnp.concatenate([b_igate, b_fgate])
    bias_row, bias_col = gate_bias[None, :], gate_bias[:, None]
    ki0 = 1024 + 2 * kvw + IDX_HEADS * IDX_DIM
    w1 = jnp.concatenate([w_in1[:, :ki0], _pad_cols(w_in1[:, ki0:ki0 + IDX_DIM], LANES)], axis=1).astype(BF16)
    wvt = w_in1[:, 1024 + kvw:1024 + 2 * kvw].T.astype(BF16)
    wwit = w_in1[:, ki0 + IDX_DIM:].T.astype(BF16)
    wo0, wo1 = w_out0.astype(BF16), w_out1.astype(BF16)
    wup, wdn = w_up.astype(BF16), w_down.astype(BF16)
    g_mix, g_mlp, g_fin = norm_mix[:, None, :], norm_mlp[:, None, :], norm_final[None, :]

    def trunk(x2d, bsz, t, s5_h0, c0, n0, m0, attend):
        n = bsz * t
        z0 = _proj0(x2d, g_mix[0], w0)
        gates = z0[:, gate_cols:gate_cols + 2 * nh]
        m0b = jnp.broadcast_to(m0[:, :, None, None], (bsz, nh, 1, LANES))
        n0r = n0[:, :, None, :]
        if t == 1:
            y_s5, h_t = _s5_step(z0, s5_h0, s5p, s5_width)
            y_ml, c_t, n_t, m_t = _mlstm_step(z0.reshape(bsz, 1, -1), gates.reshape(bsz, 1, 2 * nh),
                                              bias_row, c0, n0r, m0b, nh)
        else:
            y_s5, h_t = _s5_seq(z0.reshape(bsz, t, -1), s5_h0.reshape(bsz, 1, -1), s5p, s5_width)
            g3 = gates.reshape(bsz, t, 2 * nh)
            y_ml, c_t, n_t, m_t = _mlstm_chunked(z0.reshape(bsz, t, -1), g3, jnp.swapaxes(g3, 1, 2),
                                                 bias_col, bias_row, c0, n0r, m0b, nh)
        h_t = h_t.reshape(bsz, 2, s5_groups, s5_state)
        states = (h_t[:, 0], h_t[:, 1], c_t, n_t[:, :, 0, :], m_t[:, :, 0, 0])
        h1 = _post(x2d, y_s5.reshape(n, s5_width), (y_ml.reshape(n, ml_width), 0), wo0, g_mlp[0],
                   wup[0], wdn[0], g_fin, final_norm=False)
        q, k, v, kbf, qi, kidx, kidxbf, vt, wit = _proj1(h1, g_mix[1], w1, wvt, wwit)
        o = attend(q, k, v, kbf, qi, kidx, kidxbf, vt, wit)
        y = _post(h1, o, (o, 1), wo1, g_mlp[1], wup[1], wdn[1], g_fin, final_norm=True)
        rows = (k.reshape(bsz, t, -1, HEAD_DIM), v.reshape(bsz, t, -1, HEAD_DIM), kidx.reshape(bsz, t, IDX_DIM))
        return y.reshape(bsz, t, d), states, rows

    def attend_prompt(q, k, v, kbf, qi, kidx, kidxbf, vt, wit):
        r3 = lambda z: z.reshape(bp, tp, -1)
        return _dsa_prompt(r3(q), r3(qi), wit, r3(kidxbf), r3(kbf), vt).reshape(bp * tp, -1)

    def attend_decode(q, k, v, kbf, qi, kidx, kidxbf, vt, wit):
        qi3 = qi.astype(F32).reshape(db, IDX_HEADS, IDX_DIM)
        wi3 = wit.T.reshape(db, IDX_HEADS, 1)
        keys, newkey = _dsa_score(page_table, qi3, wi3, kidx.reshape(db, 1, IDX_DIM), cache_kidx)
        tail = jnp.where(lax.broadcasted_iota(I32, (db, LANES), 1) == 0, newkey[:, 0, :], -jnp.inf)
        keys = jnp.concatenate([keys[:, 0, :], tail], axis=1)
        thr, j0 = _dsa_select(keys)
        rep = nheads // (kvw // HEAD_DIM)
        q3 = q.astype(F32).reshape(db, nheads, HEAD_DIM)
        head_kv = jnp.arange(nheads) // rep
        q_bd = jnp.concatenate([jnp.where((head_kv == g)[None, :, None], q3, 0.0)
                                for g in range(kvw // HEAD_DIM)], axis=2)
        o = _dsa_decode(page_table, q_bd, keys[:, None, :], thr[:, None, :], j0[:, None, :],
                        k[:, None, :], v[:, None, :], cache_k, cache_v)
        return o.reshape(db, nheads * HEAD_DIM)

    zeros = lambda *shape: jnp.zeros(shape, F32)
    y_p, st_p, rows_p = trunk(x_prompt.reshape(bp * tp, d), bp, tp, zeros(bp, 2 * nstate),
                              zeros(bp, nh, HEAD_DIM, HEAD_DIM), zeros(bp, nh, HEAD_DIM), zeros(bp, nh),
                              attend_prompt)
    s5_h0 = jnp.concatenate([state_s5_re.reshape(db, nstate), state_s5_im.reshape(db, nstate)], axis=1)
    y_s, st_s, rows_s = trunk(x_sample.reshape(db, d), db, 1, s5_h0, state_mlstm_c, state_mlstm_n,
                              state_mlstm_m, attend_decode)
    return (y_p, y_s) + st_p + rows_p + st_s + rows_s
```

```python
import functools
import math

import jax
import jax.numpy as jnp
from jax import lax
from jax.experimental import pallas as pl
from jax.experimental.pallas import tpu as pltpu

F32, BF16, I32 = jnp.float32, jnp.bfloat16, jnp.int32

EPS = 1e-6
LANES = 128
PAGE = 128
S5_GROUP = 16
S5_STATE = 64
HEAD_DIM = 128
IDX_DIM = 64
IDX_HEADS = 8
TOPK = 256
Q_BLOCK = 128
KEY_CHUNK = 1024
COUNT_CHUNK = 512
MLSTM_CHUNK = 128
S5_CHUNK = 256
ROW_TILE = 512
FF_CHUNK = 1024
PAGES_PER_STEP = 16
INT_MIN = -2 ** 31
NEG_BIG = -1e30
VMEM_LIMIT = 56 * 1024 * 1024


def _cparams(*sem):
    return pltpu.CompilerParams(dimension_semantics=sem, vmem_limit_bytes=VMEM_LIMIT)


def _rms(x, g):
    return x * lax.rsqrt(jnp.mean(x * x, axis=-1, keepdims=True) + EPS) * g


def _dot(a, b):
    return jnp.dot(a, b, preferred_element_type=F32)


def _dot_nt(a, b):
    return lax.dot_general(a, b, (((1,), (1,)), ((), ())), preferred_element_type=F32)


def _dot_tn(a, b):
    return lax.dot_general(a, b, (((0,), (0,)), ((), ())), preferred_element_type=F32)


def _proj0_kernel(x_ref, g_ref, w_ref, o_ref):
    xn = _rms(x_ref[...], g_ref[...]).astype(BF16)
    o_ref[...] = _dot(xn, w_ref[...])


def _proj0(x, g, w):
    n, d = x.shape
    tm = min(ROW_TILE, n)
    wtot = w.shape[1]
    return pl.pallas_call(
        _proj0_kernel,
        grid=(n // tm,),
        in_specs=[pl.BlockSpec((tm, d), lambda i: (i, 0)),
                  pl.BlockSpec((1, d), lambda i: (0, 0)),
                  pl.BlockSpec((d, wtot), lambda i: (0, 0))],
        out_specs=pl.BlockSpec((tm, wtot), lambda i: (i, 0)),
        out_shape=jax.ShapeDtypeStruct((n, wtot), F32),
        compiler_params=_cparams("parallel"),
        name="proj0",
    )(x, g, w)


_Q1, _K1, _V1, _QI1, _KI1, _END1 = 0, 1024, 1280, 1536, 2048, 2176
LOG2E = 1.4426950408889634


def _proj1_kernel(x_ref, g_ref, w_ref, wvt_ref, wwit_ref, q_ref, k_ref, v_ref, kbf_ref, qi_ref,
                  kidx_ref, kidxbf_ref, vt_ref, wit_ref):
    xn = _rms(x_ref[...], g_ref[...]).astype(BF16)
    z = _dot(xn, w_ref[...])
    q_ref[...] = (z[:, _Q1:_K1] * (HEAD_DIM ** -0.5 * LOG2E)).astype(BF16)
    k = z[:, _K1:_V1]
    k_ref[...] = k
    v_ref[...] = z[:, _V1:_QI1]
    kbf_ref[...] = k.astype(BF16)
    qi_ref[...] = z[:, _QI1:_KI1].astype(BF16)
    kidx = z[:, _KI1:_KI1 + IDX_DIM]
    kidx_ref[...] = kidx
    kidxbf_ref[...] = kidx.astype(BF16)
    vt_ref[...] = _dot_nt(wvt_ref[...], xn).astype(BF16)
    wit_ref[...] = _dot_nt(wwit_ref[...], xn) * ((IDX_DIM ** -0.5) * (IDX_HEADS ** -0.5))


def _proj1(x, g, w, wvt, wwit):
    n, d = x.shape
    tm = min(ROW_TILE, n)
    kvw = wvt.shape[0]
    row = lambda width: pl.BlockSpec((tm, width), lambda i: (i, 0))
    col = lambda height: pl.BlockSpec((height, tm), lambda i: (0, i))
    full = lambda a: pl.BlockSpec(a.shape, lambda i: (0, 0))
    shp = lambda width, dt: jax.ShapeDtypeStruct((n, width), dt)
    return pl.pallas_call(
        _proj1_kernel,
        grid=(n // tm,),
        in_specs=[row(d), pl.BlockSpec((1, d), lambda i: (0, 0)), full(w), full(wvt), full(wwit)],
        out_specs=[row(1024), row(kvw), row(kvw), row(kvw), row(512), row(IDX_DIM), row(IDX_DIM),
                   col(kvw), col(IDX_HEADS)],
        out_shape=[shp(1024, BF16), shp(kvw, F32), shp(kvw, F32), shp(kvw, BF16), shp(512, BF16),
                   shp(IDX_DIM, F32), shp(IDX_DIM, BF16),
                   jax.ShapeDtypeStruct((kvw, n), BF16), jax.ShapeDtypeStruct((IDX_HEADS, n), F32)],
        compiler_params=_cparams("parallel"),
        name="proj1",
    )(x, g, w, wvt, wwit)


def _post_kernel(h_ref, ya_ref, yb_ref, wo_ref, g_ref, wup_ref, wdn_ref, gf_ref, out_ref,
                 h1_s, xn_s, acc_s, *, final_norm):
    j = pl.program_id(1)
    half = ya_ref.shape[1]

    @pl.when(j == 0)
    def _():
        h1 = h_ref[...] + _dot(ya_ref[...], wo_ref[:half, :]) + _dot(yb_ref[...], wo_ref[half:, :])
        h1_s[...] = h1
        xn_s[...] = _rms(h1, g_ref[...]).astype(BF16)
        acc_s[...] = jnp.zeros_like(acc_s)

    r = jnp.maximum(_dot(xn_s[...], wup_ref[...]), 0.0)
    acc_s[...] += _dot((r * r).astype(BF16), wdn_ref[...])

    @pl.when(j == pl.num_programs(1) - 1)
    def _():
        o = h1_s[...] + acc_s[...]
        if final_norm:
            o = _rms(o, gf_ref[...])
        out_ref[...] = o


def _post(h, ya, yb_spec_arg, wo, g, wup, wdn, gf, *, final_norm):
    n, d = h.shape
    tm = min(ROW_TILE, n)
    yb, yb_col = yb_spec_arg
    half = d // 2
    dff = wup.shape[1]
    return pl.pallas_call(
        functools.partial(_post_kernel, final_norm=final_norm),
        grid=(n // tm, dff // FF_CHUNK),
        in_specs=[pl.BlockSpec((tm, d), lambda i, j: (i, 0)),
                  pl.BlockSpec((tm, half), lambda i, j: (i, 0)),
                  pl.BlockSpec((tm, half), lambda i, j: (i, yb_col)),
                  pl.BlockSpec((d, d), lambda i, j: (0, 0)),
                  pl.BlockSpec((1, d), lambda i, j: (0, 0)),
                  pl.BlockSpec((d, FF_CHUNK), lambda i, j: (0, j)),
                  pl.BlockSpec((FF_CHUNK, d), lambda i, j: (j, 0)),
                  pl.BlockSpec((1, d), lambda i, j: (0, 0))],
        out_specs=pl.BlockSpec((tm, d), lambda i, j: (i, 0)),
        out_shape=jax.ShapeDtypeStruct((n, d), F32),
        scratch_shapes=[pltpu.VMEM((tm, d), F32), pltpu.VMEM((tm, d), BF16), pltpu.VMEM((tm, d), F32)],
        compiler_params=_cparams("parallel", "arbitrary"),
        name="post_final" if final_norm else "post",
    )(h, ya, yb, wo, g, wup, wdn, gf)


def _s5_output(hre, him, u, wc_ref, d_ref, wg_ref, bg_ref):
    half = hre.shape[1]
    y = _dot(hre.astype(BF16), wc_ref[:half, :]) + _dot(him.astype(BF16), wc_ref[half:, :])
    y = jax.nn.gelu(y + d_ref[...] * u)
    gate = jax.nn.sigmoid(_dot(y.astype(BF16), wg_ref[...]) + bg_ref[...])
    return (y * gate).astype(BF16)


def _s5_seq_kernel(u_ref, h0_ref, lam_ref, wb_ref, wc_ref, d_ref, wg_ref, bg_ref, y_ref, ht_ref,
                   hre_s, him_s, carry_s):
    half = hre_s.shape[1]

    @pl.when(pl.program_id(1) == 0)
    def _():
        carry_s[...] = h0_ref[...]

    u = u_ref[...]
    bu = _dot(u.astype(BF16), wb_ref[...])
    hre_s[...] = bu[:, :half]
    him_s[...] = bu[:, half:]
    a_re = lam_ref[0:1, :]
    a_im = lam_ref[1:2, :]

    def step(t, carry):
        h_re, h_im = carry
        n_re = a_re * h_re - a_im * h_im + hre_s[pl.ds(t, 1), :]
        n_im = a_re * h_im + a_im * h_re + him_s[pl.ds(t, 1), :]
        hre_s[pl.ds(t, 1), :] = n_re
        him_s[pl.ds(t, 1), :] = n_im
        return n_re, n_im

    h_re, h_im = lax.fori_loop(0, hre_s.shape[0], step, (carry_s[:, :half], carry_s[:, half:]), unroll=8)
    carry_s[:, :half] = h_re
    carry_s[:, half:] = h_im
    ht_ref[...] = carry_s[...]
    y_ref[...] = _s5_output(hre_s[...], him_s[...], u, wc_ref, d_ref, wg_ref, bg_ref)


def _s5_step_kernel(u_ref, h0_ref, lam_ref, wb_ref, wc_ref, d_ref, wg_ref, bg_ref, y_ref, ht_ref):
    half = lam_ref.shape[1]
    u = u_ref[...]
    bu = _dot(u.astype(BF16), wb_ref[...])
    a_re = lam_ref[0:1, :]
    a_im = lam_ref[1:2, :]
    h_re = h0_ref[:, :half]
    h_im = h0_ref[:, half:]
    n_re = a_re * h_re - a_im * h_im + bu[:, :half]
    n_im = a_re * h_im + a_im * h_re + bu[:, half:]
    ht_ref[:, :half] = n_re
    ht_ref[:, half:] = n_im
    y_ref[...] = _s5_output(n_re, n_im, u, wc_ref, d_ref, wg_ref, bg_ref)


def _s5_param_specs(width, nstate, imap):
    return [pl.BlockSpec((2, nstate), imap),
            pl.BlockSpec((width, 2 * nstate), imap),
            pl.BlockSpec((2 * nstate, width), imap),
            pl.BlockSpec((1, width), imap),
            pl.BlockSpec((width, width), imap),
            pl.BlockSpec((1, width), imap)]


def _s5_seq(z0, h0, params, width):
    bsz, t, _ = z0.shape
    nstate = params[0].shape[1]
    ts = min(S5_CHUNK, t)
    const = lambda b, c: (0, 0)
    return pl.pallas_call(
        _s5_seq_kernel,
        grid=(bsz, t // ts),
        in_specs=[pl.BlockSpec((None, ts, width), lambda b, c: (b, c, 0)),
                  pl.BlockSpec((None, 1, 2 * nstate), lambda b, c: (b, 0, 0))]
                 + _s5_param_specs(width, nstate, const),
        out_specs=[pl.BlockSpec((None, ts, width), lambda b, c: (b, c, 0)),
                   pl.BlockSpec((None, 1, 2 * nstate), lambda b, c: (b, 0, 0))],
        out_shape=[jax.ShapeDtypeStruct((bsz, t, width), BF16),
                   jax.ShapeDtypeStruct((bsz, 1, 2 * nstate), F32)],
        scratch_shapes=[pltpu.VMEM((ts, nstate), F32), pltpu.VMEM((ts, nstate), F32),
                        pltpu.VMEM((1, 2 * nstate), F32)],
        compiler_params=_cparams("parallel", "arbitrary"),
        name="s5_scan",
    )(z0, h0, *params)


def _s5_step(z0, h0, params, width):
    rows = z0.shape[0]
    nstate = params[0].shape[1]
    const = lambda i: (0, 0)
    return pl.pallas_call(
        _s5_step_kernel,
        grid=(1,),
        in_specs=[pl.BlockSpec((rows, width), const), pl.BlockSpec((rows, 2 * nstate), const)]
                 + _s5_param_specs(width, nstate, const),
        out_specs=[pl.BlockSpec((rows, width), const), pl.BlockSpec((rows, 2 * nstate), const)],
        out_shape=[jax.ShapeDtypeStruct((rows, width), BF16),
                   jax.ShapeDtypeStruct((rows, 2 * nstate), F32)],
        compiler_params=_cparams("arbitrary"),
        name="s5_step",
    )(z0, h0, *params)


def _mlstm_chunk_kernel(q_ref, k_ref, v_ref, o_ref, gcol_ref, grow_ref, bcol_ref, brow_ref,
                        c0_ref, n0_ref, m0_ref, y_ref, c_ref, n_ref, m_ref):
    nh = c_ref.shape[0]
    ch = q_ref.shape[0]

    @pl.when(pl.program_id(1) == 0)
    def _():
        c_ref[...] = c0_ref[...]
        n_ref[...] = n0_ref[...]
        m_ref[...] = m0_ref[...]

    gcol = gcol_ref[...] + brow_ref[...]
    grow = grow_ref[...] + bcol_ref[...]
    t_idx = lax.broadcasted_iota(I32, (ch, ch), 0)
    s_idx = lax.broadcasted_iota(I32, (ch, ch), 1)
    causal = t_idx >= s_idx
    for h in range(nh):
        sl = slice(h * HEAD_DIM, (h + 1) * HEAD_DIM)
        q = q_ref[:, sl]
        k = k_ref[:, sl] * (HEAD_DIM ** -0.5)
        v = v_ref[:, sl]
        q_bf, k_bf, v_bf = q.astype(BF16), k.astype(BF16), v.astype(BF16)
        i_col = gcol[:, h:h + 1]
        i_row = grow[h:h + 1, :]
        lf_col = jax.nn.log_sigmoid(gcol[:, nh + h:nh + h + 1])
        lf_row = jax.nn.log_sigmoid(grow[nh + h:nh + h + 1, :])
        b_col = jnp.sum(jnp.where(causal, lf_row, 0.0), axis=1, keepdims=True)
        b_row = jnp.sum(jnp.where(causal, 0.0, lf_col) , axis=0, keepdims=True)
        b_row = b_row + lf_row
        m_prev = m_ref[h][:, 0:1]
        dmat = jnp.where(causal, b_col - b_row + i_row, -jnp.inf)
        a_col = b_col + m_prev
        mj = jnp.maximum(a_col, jnp.max(dmat, axis=1, keepdims=True))
        w_intra = jnp.exp(dmat - mj)
        w_inter = jnp.exp(a_col - mj)
        s = _dot_nt(q_bf, k_bf) * w_intra
        c_prev = c_ref[h]
        n_prev = n_ref[h]
        num = _dot(s.astype(BF16), v_bf) + w_inter * _dot(q_bf, c_prev.astype(BF16))
        den = jnp.sum(s, axis=1, keepdims=True) + w_inter * jnp.sum(q * n_prev, axis=1, keepdims=True)
        hout = num / jnp.maximum(jnp.abs(den), jnp.exp(-mj))
        y_ref[:, sl] = (jax.nn.sigmoid(o_ref[:, sl]) * hout).astype(BF16)
        m_new = mj[ch - 1:ch, :]
        b_last = b_col[ch - 1:ch, :]
        w_end = jnp.exp(b_last - b_col + i_col - m_new)
        decay = jnp.exp(b_last + m_prev - m_new)
        kw = k * w_end
        c_ref[h] = decay * c_prev + _dot_tn(kw.astype(BF16), v_bf)
        n_ref[h] = decay * n_prev + jnp.sum(kw, axis=0, keepdims=True)
        m_ref[h] = jnp.broadcast_to(m_new, (1, LANES))


def _mlstm_chunked(z0, gcol, grow, bias_col, bias_row, c0, n0, m0, nh):
    bsz, t, _ = z0.shape
    width = nh * HEAD_DIM
    ch = MLSTM_CHUNK
    zspec = lambda blk: pl.BlockSpec((None, ch, width), lambda b, c: (b, c, blk))
    state = lambda shape: pl.BlockSpec((None,) + shape, lambda b, c: (b,) + (0,) * len(shape))
    return pl.pallas_call(
        _mlstm_chunk_kernel,
        grid=(bsz, t // ch),
        in_specs=[zspec(1), zspec(2), zspec(3), zspec(4),
                  pl.BlockSpec((None, ch, 2 * nh), lambda b, c: (b, c, 0)),
                  pl.BlockSpec((None, 2 * nh, ch), lambda b, c: (b, 0, c)),
                  pl.BlockSpec((2 * nh, 1), lambda b, c: (0, 0)),
                  pl.BlockSpec((1, 2 * nh), lambda b, c: (0, 0)),
                  state((nh, HEAD_DIM, HEAD_DIM)), state((nh, 1, HEAD_DIM)), state((nh, 1, LANES))],
        out_specs=[pl.BlockSpec((None, ch, width), lambda b, c: (b, c, 0)),
                   state((nh, HEAD_DIM, HEAD_DIM)), state((nh, 1, HEAD_DIM)), state((nh, 1, LANES))],
        out_shape=[jax.ShapeDtypeStruct((bsz, t, width), BF16),
                   jax.ShapeDtypeStruct((bsz, nh, HEAD_DIM, HEAD_DIM), F32),
                   jax.ShapeDtypeStruct((bsz, nh, 1, HEAD_DIM), F32),
                   jax.ShapeDtypeStruct((bsz, nh, 1, LANES), F32)],
        compiler_params=_cparams("parallel", "arbitrary"),
        name="mlstm_chunk",
    )(z0, z0, z0, z0, gcol, grow, bias_col, bias_row, c0, n0, m0)


def _to_column(row):
    n = row.shape[1]
    eye = lax.broadcasted_iota(I32, (n, n), 0) == lax.broadcasted_iota(I32, (n, n), 1)
    return jnp.sum(jnp.where(eye, row, 0.0), axis=1, keepdims=True)


def _mlstm_step_kernel(q_ref, k_ref, v_ref, o_ref, g_ref, brow_ref, c0_ref, n0_ref, m0_ref,
                       y_ref, c_ref, n_ref, m_ref):
    nh = c_ref.shape[0]
    g = g_ref[...] + brow_ref[...]
    for h in range(nh):
        sl = slice(h * HEAD_DIM, (h + 1) * HEAD_DIM)
        q = q_ref[:, sl]
        k = k_ref[:, sl] * (HEAD_DIM ** -0.5)
        v = v_ref[:, sl]
        i_pre = g[:, h:h + 1]
        lf = jax.nn.log_sigmoid(g[:, nh + h:nh + h + 1])
        m_prev = m0_ref[h][:, 0:1]
        c_prev = c0_ref[h]
        n_prev = n0_ref[h]
        a = lf + m_prev
        mj = jnp.maximum(a, i_pre)
        w_intra = jnp.exp(i_pre - mj)
        w_inter = jnp.exp(a - mj)
        s = jnp.sum(q * k, axis=1, keepdims=True) * w_intra
        q_col = _to_column(q)
        k_col = _to_column(k)
        num = s * v + w_inter * jnp.sum(q_col * c_prev, axis=0, keepdims=True)
        den = s + w_inter * jnp.sum(q * n_prev, axis=1, keepdims=True)
        hout = num / jnp.maximum(jnp.abs(den), jnp.exp(-mj))
        y_ref[:, sl] = (jax.nn.sigmoid(o_ref[:, sl]) * hout).astype(BF16)
        w_end = jnp.exp(i_pre - mj)
        decay = jnp.exp(a - mj)
        c_ref[h] = decay * c_prev + (w_end * k_col) * v
        n_ref[h] = decay * n_prev + w_end * k
        m_ref[h] = jnp.broadcast_to(mj, (1, LANES))


def _mlstm_step(z0, g, bias_row, c0, n0, m0, nh):
    bsz = z0.shape[0]
    width = nh * HEAD_DIM
    zspec = lambda blk: pl.BlockSpec((None, 1, width), lambda b: (b, 0, blk))
    state = lambda shape: pl.BlockSpec((None,) + shape, lambda b: (b,) + (0,) * len(shape))
    return pl.pallas_call(
        _mlstm_step_kernel,
        grid=(bsz,),
        in_specs=[zspec(1), zspec(2), zspec(3), zspec(4),
                  pl.BlockSpec((None, 1, 2 * nh), lambda b: (b, 0, 0)),
                  pl.BlockSpec((1, 2 * nh), lambda b: (0, 0)),
                  state((nh, HEAD_DIM, HEAD_DIM)), state((nh, 1, HEAD_DIM)), state((nh, 1, LANES))],
        out_specs=[pl.BlockSpec((None, 1, width), lambda b: (b, 0, 0)),
                   state((nh, HEAD_DIM, HEAD_DIM)), state((nh, 1, HEAD_DIM)), state((nh, 1, LANES))],
        out_shape=[jax.ShapeDtypeStruct((bsz, 1, width), BF16),
                   jax.ShapeDtypeStruct((bsz, nh, HEAD_DIM, HEAD_DIM), F32),
                   jax.ShapeDtypeStruct((bsz, nh, 1, HEAD_DIM), F32),
                   jax.ShapeDtypeStruct((bsz, nh, 1, LANES), F32)],
        compiler_params=_cparams("parallel"),
        name="mlstm_step",
    )(z0, z0, z0, z0, g, bias_row, c0, n0, m0)


def _key_to_float(key):
    bits = key ^ ((key >> 31) & jnp.int32(0x7FFFFFFF))
    return lax.bitcast_convert_type(bits, F32)


def _select_threshold(count, shape, width, j0_s):
    def bit_step(it, key):
        cand = key + (jnp.int32(1) << (31 - it))
        cand_f = _key_to_float(cand)
        cnt = count(lambda tile, col: tile >= cand_f)
        return jnp.where(cnt >= TOPK, cand, key)

    key = lax.fori_loop(0, 32, bit_step, jnp.full(shape, INT_MIN, I32))
    has_thr = key > INT_MIN
    thr = jnp.where(has_thr, _key_to_float(jnp.where(has_thr, key, 0)), -jnp.inf)
    need = TOPK - count(lambda tile, col: tile > thr)
    n_eq = count(lambda tile, col: tile == thr)
    j0_s[...] = jnp.where(has_thr, jnp.int32(width), jnp.int32(-1))
    surplus = jnp.max(jnp.where(has_thr & (n_eq > need), 1, 0))
    nbits = max(1, (width - 1).bit_length())

    @pl.when(surplus > 0)
    def _():
        def idx_step(it, j0):
            cand = j0 | (jnp.int32(1) << (nbits - 1 - it))
            cnt = count(lambda tile, col: (tile == thr) & (col < cand))
            return jnp.where(cnt < need, cand, j0)

        j0 = lax.fori_loop(0, nbits, idx_step, jnp.zeros(shape, I32))
        j0_s[...] = jnp.where(has_thr, j0, jnp.int32(-1))

    return thr, j0_s[...]


def _mask_bias(scores, cols, thr, j0):
    sel = (scores > thr) | ((scores == thr) & (cols <= j0))
    return jnp.where(sel, 0.0, NEG_BIG)


def _dsa_prompt_kernel(q_ref, qi_ref, wit_ref, ki_ref, k_ref, vt_ref, o_ref, sc_s, j0_s, m_s, acc_s):
    qb = q_ref.shape[0]
    kc_len = KEY_CHUNK
    nheads = q_ref.shape[1] // HEAD_DIM
    nkv = k_ref.shape[1] // HEAD_DIM
    rep = nheads // nkv
    i = pl.program_id(1)
    nchunks = ((i + 1) * qb + kc_len - 1) // kc_len
    qpos = i * qb + lax.broadcasted_iota(I32, (1, qb), 1)
    sub = lax.broadcasted_iota(I32, (kc_len, qb), 0)

    qi = qi_ref[...]
    wit = wit_ref[...]
    qi_all = jnp.concatenate([qi[:, h * IDX_DIM:(h + 1) * IDX_DIM] for h in range(IDX_HEADS)], axis=0)

    def score_chunk(c, _):
        off = pl.multiple_of(c * kc_len, kc_len)
        ki = ki_ref[pl.ds(off, kc_len), :]
        logits = _dot_nt(ki, qi_all)
        sc = jnp.zeros((kc_len, qb), F32)
        for h in range(IDX_HEADS):
            sc = sc + jnp.maximum(logits[:, h * qb:(h + 1) * qb], 0.0) * wit[h:h + 1, :]
        sc_s[pl.ds(off, kc_len), :] = jnp.where(sub + off <= qpos, sc, -jnp.inf)
        return 0

    lax.fori_loop(0, nchunks, score_chunk, 0)

    def count(pred):
        def body(c, acc):
            off = pl.multiple_of(c * COUNT_CHUNK, COUNT_CHUNK)
            ind = jnp.where(pred(sc_s[pl.ds(off, COUNT_CHUNK), :], sub[:COUNT_CHUNK] + off), 1, 0)
            return acc + jnp.sum(ind.reshape(COUNT_CHUNK // 8, 8, qb), axis=0)

        ncount = ((i + 1) * qb + COUNT_CHUNK - 1) // COUNT_CHUNK
        acc = lax.fori_loop(0, ncount, body, jnp.zeros((8, qb), I32))
        return jnp.sum(acc, axis=0, keepdims=True)

    thr, j0 = _select_threshold(count, (1, qb), sc_s.shape[0], j0_s)

    m_s[...] = jnp.full_like(m_s, NEG_BIG)
    acc_s[...] = jnp.zeros_like(acc_s)
    q = q_ref[...]
    q_g = [jnp.concatenate([q[:, (g * rep + r) * HEAD_DIM:(g * rep + r + 1) * HEAD_DIM]
                            for r in range(rep)], axis=0) for g in range(nkv)]
    ones_rows = jnp.ones((acc_s.shape[1] - HEAD_DIM, kc_len), BF16)

    def attend_chunk(c, _):
        off = pl.multiple_of(c * kc_len, kc_len)
        bias = _mask_bias(sc_s[pl.ds(off, kc_len), :], sub + off, thr, j0)
        bias = jnp.concatenate([bias] * rep, axis=1)
        for g in range(nkv):
            kc = k_ref[pl.ds(off, kc_len), g * HEAD_DIM:(g + 1) * HEAD_DIM]
            att = _dot_nt(kc, q_g[g]) + bias
            m_old = m_s[g]
            m_new = jnp.maximum(m_old, jnp.max(att, axis=0, keepdims=True))
            p = jnp.exp2(att - m_new).astype(BF16)
            vt = jnp.concatenate([vt_ref[g * HEAD_DIM:(g + 1) * HEAD_DIM, pl.ds(off, kc_len)], ones_rows],
                                 axis=0)
            acc_s[g] = jnp.exp2(m_old - m_new) * acc_s[g] + _dot(vt, p)
            m_s[g] = m_new
        return 0

    lax.fori_loop(0, nchunks, attend_chunk, 0)

    for g in range(nkv):
        acc = acc_s[g]
        out = acc[:HEAD_DIM, :] / acc[HEAD_DIM:HEAD_DIM + 1, :]
        for r in range(rep):
            hd = g * rep + r
            o_ref[:, hd * HEAD_DIM:(hd + 1) * HEAD_DIM] = out[:, r * qb:(r + 1) * qb].T.astype(BF16)


def _dsa_prompt(q, qi, wit, kidx, kbf, vt):
    bsz, t, width = q.shape
    kvw = kbf.shape[2]
    nkv = kvw // HEAD_DIM
    rep = width // HEAD_DIM // nkv
    qb = Q_BLOCK
    nq = t // qb
    ones_rows = 16
    return pl.pallas_call(
        _dsa_prompt_kernel,
        grid=(bsz, nq),
        in_specs=[pl.BlockSpec((None, qb, width), lambda b, i: (b, i, 0)),
                  pl.BlockSpec((None, qb, qi.shape[2]), lambda b, i: (b, i, 0)),
                  pl.BlockSpec((IDX_HEADS, qb), lambda b, i: (0, b * nq + i)),
                  pl.BlockSpec((None, t, IDX_DIM), lambda b, i: (b, 0, 0)),
                  pl.BlockSpec((None, t, kvw), lambda b, i: (b, 0, 0)),
                  pl.BlockSpec((kvw, t), lambda b, i: (0, b))],
        out_specs=pl.BlockSpec((None, qb, width), lambda b, i: (b, i, 0)),
        out_shape=jax.ShapeDtypeStruct((bsz, t, width), BF16),
        scratch_shapes=[pltpu.VMEM((t, qb), F32), pltpu.VMEM((1, qb), I32),
                        pltpu.VMEM((nkv, 1, rep * qb), F32),
                        pltpu.VMEM((nkv, HEAD_DIM + ones_rows, rep * qb), F32)],
        compiler_params=_cparams("parallel", "arbitrary"),
        name="dsa_prompt",
    )(q, qi, wit, kidx, kbf, vt)


def _dsa_score_kernel(pt_ref, qi_ref, wi_ref, knew_ref, *rest):
    pages, (keys_ref, newkey_ref) = rest[:PAGES_PER_STEP], rest[PAGES_PER_STEP:]
    qi = qi_ref[...]
    wi = wi_ref[...]
    ki = jnp.concatenate([page_ref[...].astype(BF16) for page_ref in pages], axis=0)
    logits = _dot_nt(qi.astype(BF16), ki)
    keys_ref[...] = jnp.sum(jnp.maximum(logits, 0.0) * wi, axis=0, keepdims=True)
    logit_new = jnp.sum(qi * knew_ref[...], axis=1, keepdims=True)
    sc_new = jnp.sum(jnp.maximum(logit_new, 0.0) * wi, axis=0, keepdims=True)
    newkey_ref[...] = jnp.broadcast_to(sc_new, (1, LANES))


def _dsa_score(page_table, qi, wi, kidx_new, cache_kidx):
    bsz, npages = page_table.shape
    steps = npages // PAGES_PER_STEP
    page_spec = lambda p: pl.BlockSpec(
        (None, PAGE, IDX_DIM), lambda b, j, pt: (pt[b, j * PAGES_PER_STEP + p], 0, 0))
    per_seq = lambda shape: pl.BlockSpec((None,) + shape, lambda b, j, pt: (b, 0, 0))
    return pl.pallas_call(
        _dsa_score_kernel,
        grid_spec=pltpu.PrefetchScalarGridSpec(
            num_scalar_prefetch=1,
            grid=(bsz, steps),
            in_specs=[per_seq((IDX_HEADS, IDX_DIM)), per_seq((IDX_HEADS, 1)), per_seq((1, IDX_DIM))]
                     + [page_spec(p) for p in range(PAGES_PER_STEP)],
            out_specs=[pl.BlockSpec((None, 1, PAGES_PER_STEP * PAGE), lambda b, j, pt: (b, 0, j)),
                       per_seq((1, LANES))]),
        out_shape=[jax.ShapeDtypeStruct((bsz, 1, npages * PAGE), F32),
                   jax.ShapeDtypeStruct((bsz, 1, LANES), F32)],
        compiler_params=_cparams("parallel", "arbitrary"),
        name="dsa_decode_score",
    )(page_table, qi, wi, kidx_new, *([cache_kidx] * PAGES_PER_STEP))


def _dsa_select_kernel(keys_ref, thr_ref, j0_ref, j0_s):
    rows, width = keys_ref.shape
    lane = lax.broadcasted_iota(I32, (rows, LANES), 1)

    def count(pred):
        def body(t, acc):
            off = pl.multiple_of(t * LANES, LANES)
            return acc + jnp.where(pred(keys_ref[:, pl.ds(off, LANES)], lane + off), 1, 0)

        acc = lax.fori_loop(0, width // LANES, body, jnp.zeros((rows, LANES), I32))
        return jnp.sum(acc, axis=1, keepdims=True)

    thr, j0 = _select_threshold(count, (rows, 1), width, j0_s)
    thr_ref[...] = jnp.broadcast_to(thr, thr_ref.shape)
    j0_ref[...] = jnp.broadcast_to(j0, j0_ref.shape)


def _dsa_select(keys):
    rows, width = keys.shape
    const = lambda i: (0, 0)
    return pl.pallas_call(
        _dsa_select_kernel,
        grid=(1,),
        in_specs=[pl.BlockSpec((rows, width), const)],
        out_specs=[pl.BlockSpec((rows, LANES), const), pl.BlockSpec((rows, LANES), const)],
        out_shape=[jax.ShapeDtypeStruct((rows, LANES), F32), jax.ShapeDtypeStruct((rows, LANES), I32)],
        scratch_shapes=[pltpu.VMEM((rows, 1), I32)],
        compiler_params=_cparams("arbitrary"),
        name="dsa_decode_select",
    )(keys)


def _dsa_decode_kernel(pt_ref, q_ref, keys_ref, tail_ref, thr_ref, j0_ref, knew_ref, vnew_ref, *rest,
                       n_past, nkv):
    j = pl.program_id(1)
    nheads = q_ref.shape[0]
    rep = nheads // nkv
    kpages, vpages = rest[:PAGES_PER_STEP], rest[PAGES_PER_STEP:2 * PAGES_PER_STEP]
    o_ref, m_s, l_s, acc_s = rest[2 * PAGES_PER_STEP:]

    @pl.when(j == 0)
    def _():
        m_s[...] = jnp.full_like(m_s, NEG_BIG)
        l_s[...] = jnp.zeros_like(l_s)
        acc_s[...] = jnp.zeros_like(acc_s)

    q = q_ref[...]
    thr = thr_ref[:, 0:1]
    j0 = j0_ref[:, 0:1]
    width = PAGES_PER_STEP * PAGE * nkv
    row = lax.broadcasted_iota(I32, (1, width), 1) + j * width
    head_kv = lax.broadcasted_iota(I32, (nheads, 1), 0) // rep

    def update(att, value_fn):
        m_old = m_s[...]
        m_new = jnp.maximum(m_old, jnp.max(att, axis=1, keepdims=True))
        alpha = jnp.exp2(m_old - m_new)
        p = jnp.exp2(att - m_new)
        l_s[...] = alpha * l_s[...] + jnp.sum(p, axis=1, keepdims=True)
        acc_s[...] = alpha * acc_s[...] + value_fn(p)
        m_s[...] = m_new

    kcat = jnp.concatenate([r[...].astype(BF16) for r in kpages], axis=0)
    vcat = jnp.concatenate([r[...].astype(BF16) for r in vpages], axis=0)
    bias = _mask_bias(keys_ref[...], row // nkv, thr, j0)
    bias = jnp.where(row % nkv == head_kv, bias, NEG_BIG)
    update(_dot_nt(q.astype(BF16), kcat) + bias, lambda pr: _dot(pr.astype(BF16), vcat))

    @pl.when(j == pl.num_programs(1) - 1)
    def _():
        k_new = jnp.zeros_like(q)
        v_new = jnp.zeros_like(q)
        for g in range(nkv):
            k_new = jnp.where(head_kv == g, knew_ref[g:g + 1, :], k_new)
            v_new = jnp.where(head_kv == g, vnew_ref[g:g + 1, :], v_new)
        bias_new = _mask_bias(tail_ref[:, 0:1], jnp.int32(n_past), thr, j0)
        att_new = jnp.sum(q * k_new, axis=1, keepdims=True) + bias_new
        update(att_new, lambda pr: pr * v_new)
        o_ref[...] = (acc_s[...] / l_s[...]).astype(BF16)


def _dsa_decode(page_table, q, keys, tail, thr, j0, k_new, v_new, cache_k, cache_v):
    bsz, npages = page_table.shape
    nheads = q.shape[1]
    nkv = k_new.shape[1]
    steps = npages // PAGES_PER_STEP
    page_spec = lambda p: pl.BlockSpec(
        (PAGE * nkv, HEAD_DIM), lambda b, j, pt: (pt[b, j * PAGES_PER_STEP + p], 0))
    per_seq = lambda shape: pl.BlockSpec((None,) + shape, lambda b, j, pt: (b, 0, 0))
    keys_spec = pl.BlockSpec((None, 1, PAGES_PER_STEP * PAGE * nkv), lambda b, j, pt: (b, 0, j))
    return pl.pallas_call(
        functools.partial(_dsa_decode_kernel, n_past=npages * PAGE, nkv=nkv),
        grid_spec=pltpu.PrefetchScalarGridSpec(
            num_scalar_prefetch=1,
            grid=(bsz, steps),
            in_specs=[per_seq((nheads, HEAD_DIM)), keys_spec, per_seq((1, LANES)), per_seq((1, LANES)),
                      per_seq((1, LANES)), per_seq((nkv, HEAD_DIM)), per_seq((nkv, HEAD_DIM))]
                     + [page_spec(p) for p in range(PAGES_PER_STEP)] * 2,
            out_specs=per_seq((nheads, HEAD_DIM)),
            scratch_shapes=[pltpu.VMEM((nheads, 1), F32), pltpu.VMEM((nheads, 1), F32),
                            pltpu.VMEM((nheads, HEAD_DIM), F32)]),
        out_shape=jax.ShapeDtypeStruct((bsz, nheads, HEAD_DIM), BF16),
        compiler_params=_cparams("parallel", "arbitrary"),
        name="dsa_decode_attend",
    )(page_table, q, keys, tail, thr, j0, k_new, v_new,
      *([cache_k] * PAGES_PER_STEP), *([cache_v] * PAGES_PER_STEP))


def _pad_cols(w, width):
    return jnp.pad(w, ((0, 0), (0, width - w.shape[1])))


def _block_diag(blocks):
    g, r, c = blocks.shape
    eye = jnp.eye(g, dtype=blocks.dtype)
    return (blocks[:, :, None, :] * eye[:, None, :, None]).reshape(g * r, g * c)


def _s5_params(lam_re, lam_im, log_dt, b_re, b_im, c_re, c_im, d_skip, w_glu, b_glu):
    dt = jnp.exp(log_dt)[:, None]
    mag = jnp.exp(lam_re * dt)
    bar_re = mag * jnp.cos(lam_im * dt)
    bar_im = mag * jnp.sin(lam_im * dt)
    inv = 1.0 / (lam_re * lam_re + lam_im * lam_im)
    coef_re = (((bar_re - 1.0) * lam_re + bar_im * lam_im) * inv)[..., None]
    coef_im = ((bar_im * lam_re - (bar_re - 1.0) * lam_im) * inv)[..., None]
    bb_re = coef_re * b_re - coef_im * b_im
    bb_im = coef_re * b_im + coef_im * b_re
    lam_rows = jnp.stack([bar_re.reshape(-1), bar_im.reshape(-1)])
    to_in = lambda z: _block_diag(jnp.swapaxes(z, 1, 2))
    wb = jnp.concatenate([to_in(bb_re), to_in(bb_im)], axis=1).astype(BF16)
    to_out = lambda z: _block_diag(jnp.swapaxes(z, 1, 2))
    wc = jnp.concatenate([to_out(c_re), to_out(-c_im)], axis=0).astype(BF16)
    return (lam_rows, wb, wc, d_skip[None, :], w_glu.astype(BF16), b_glu[None, :])


def kernel(x_prompt, x_sample, state_s5_re, state_s5_im, state_mlstm_c, state_mlstm_n, state_mlstm_m,
           cache_k, cache_v, cache_kidx, page_table, norm_mix, norm_mlp, norm_final, w_in0, s5_lam_re,
           s5_lam_im, s5_log_dt, s5_b_re, s5_b_im, s5_c_re, s5_c_im, s5_d, w_glu, b_glu, b_igate,
           b_fgate, w_out0, w_in1, w_out1, w_up, w_down):
    bp, tp, d = x_prompt.shape
    db, ts, _ = x_sample.shape
    assert ts == 1, "the decode path handles one new token per sequence"
    s5_groups, s5_state = s5_lam_re.shape
    s5_width = s5_groups * S5_GROUP
    nstate = s5_groups * s5_state
    nh = b_igate.shape[0]
    ml_width = nh * HEAD_DIM
    assert s5_width == ml_width == 512 and d == 1024
    n_past = page_table.shape[1] * PAGE
    kvw = cache_k.shape[2] * cache_k.shape[3]
    nheads = w_out1.shape[0] // HEAD_DIM

    gate_cols = s5_width + 4 * ml_width
    w0 = jnp.concatenate([w_in0[:, :gate_cols], _pad_cols(w_in0[:, gate_cols:], LANES)], axis=1).astype(BF16)
    s5p = _s5_params(s5_lam_re, s5_lam_im, s5_log_dt, s5_b_re, s5_b_im, s5_c_re, s5_c_im, s5_d, w_glu, b_glu)
    gate_bias = jnp.concatenate([b_igate, b_fgate])
    bias_row, bias_col = gate_bias[None, :], gate_bias[:, None]
    ki0 = 1024 + 2 * kvw + IDX_HEADS * IDX_DIM
    w1 = jnp.concatenate([w_in1[:, :ki0], _pad_cols(w_in1[:, ki0:ki0 + IDX_DIM], LANES)], axis=1).astype(BF16)
    wvt = w_in1[:, 1024 + kvw:1024 + 2 * kvw].T.astype(BF16)
    wwit = w_in1[:, ki0 + IDX_DIM:].T.astype(BF16)
    wo0, wo1 = w_out0.astype(BF16), w_out1.astype(BF16)
    wup, wdn = w_up.astype(BF16), w_down.astype(BF16)
    g_mix, g_mlp, g_fin = norm_mix[:, None, :], norm_mlp[:, None, :], norm_final[None, :]

    def trunk(x2d, bsz, t, s5_h0, c0, n0, m0, attend):
        n = bsz * t
        z0 = _proj0(x2d, g_mix[0], w0)
        gates = z0[:, gate_cols:gate_cols + 2 * nh]
        m0b = jnp.broadcast_to(m0[:, :, None, None], (bsz, nh, 1, LANES))
        n0r = n0[:, :, None, :]
        if t == 1:
            y_s5, h_t = _s5_step(z0, s5_h0, s5p, s5_width)
            y_ml, c_t, n_t, m_t = _mlstm_step(z0.reshape(bsz, 1, -1), gates.reshape(bsz, 1, 2 * nh),
                                              bias_row, c0, n0r, m0b, nh)
        else:
            y_s5, h_t = _s5_seq(z0.reshape(bsz, t, -1), s5_h0.reshape(bsz, 1, -1), s5p, s5_width)
            g3 = gates.reshape(bsz, t, 2 * nh)
            y_ml, c_t, n_t, m_t = _mlstm_chunked(z0.reshape(bsz, t, -1), g3, jnp.swapaxes(g3, 1, 2),
                                                 bias_col, bias_row, c0, n0r, m0b, nh)
        h_t = h_t.reshape(bsz, 2, s5_groups, s5_state)
        states = (h_t[:, 0], h_t[:, 1], c_t, n_t[:, :, 0, :], m_t[:, :, 0, 0])
        h1 = _post(x2d, y_s5.reshape(n, s5_width), (y_ml.reshape(n, ml_width), 0), wo0, g_mlp[0],
                   wup[0], wdn[0], g_fin, final_norm=False)
        q, k, v, kbf, qi, kidx, kidxbf, vt, wit = _proj1(h1, g_mix[1], w1, wvt, wwit)
        o = attend(q, k, v, kbf, qi, kidx, kidxbf, vt, wit)
        y = _post(h1, o, (o, 1), wo1, g_mlp[1], wup[1], wdn[1], g_fin, final_norm=True)
        rows = (k.reshape(bsz, t, -1, HEAD_DIM), v.reshape(bsz, t, -1, HEAD_DIM), kidx.reshape(bsz, t, IDX_DIM))
        return y.reshape(bsz, t, d), states, rows

    def attend_prompt(q, k, v, kbf, qi, kidx, kidxbf, vt, wit):
        r3 = lambda z: z.reshape(bp, tp, -1)
        return _dsa_prompt(r3(q), r3(qi), wit, r3(kidxbf), r3(kbf), vt).reshape(bp * tp, -1)

    def attend_decode(q, k, v, kbf, qi, kidx, kidxbf, vt, wit):
        qi3 = qi.astype(F32).reshape(db, IDX_HEADS, IDX_DIM)
        wi3 = wit.T.reshape(db, IDX_HEADS, 1)
        keys, newkey = _dsa_score(page_table, qi3, wi3, kidx.reshape(db, 1, IDX_DIM), cache_kidx)
        tail = jnp.where(lax.broadcasted_iota(I32, (db, LANES), 1) == 0, newkey[:, 0, :], -jnp.inf)
        keys = jnp.concatenate([keys[:, 0, :], tail], axis=1)
        thr, j0 = _dsa_select(keys)
        nkv = kvw // HEAD_DIM
        keys_rows = jnp.repeat(keys[:, :n_past], nkv, axis=1)[:, None, :]
        o = _dsa_decode(page_table, q.astype(F32).reshape(db, nheads, HEAD_DIM), keys_rows,
                        keys[:, None, n_past:], thr[:, None, :], j0[:, None, :],
                        k.reshape(db, nkv, HEAD_DIM), v.reshape(db, nkv, HEAD_DIM),
                        cache_k.reshape(-1, HEAD_DIM), cache_v.reshape(-1, HEAD_DIM))
        return o.reshape(db, nheads * HEAD_DIM)

    zeros = lambda *shape: jnp.zeros(shape, F32)
    y_p, st_p, rows_p = trunk(x_prompt.reshape(bp * tp, d), bp, tp, zeros(bp, 2 * nstate),
                              zeros(bp, nh, HEAD_DIM, HEAD_DIM), zeros(bp, nh, HEAD_DIM), zeros(bp, nh),
                              attend_prompt)
    s5_h0 = jnp.concatenate([state_s5_re.reshape(db, nstate), state_s5_im.reshape(db, nstate)], axis=1)
    y_s, st_s, rows_s = trunk(x_sample.reshape(db, d), db, 1, s5_h0, state_mlstm_c, state_mlstm_n,
                              state_mlstm_m, attend_decode)
    return (y_p, y_s) + st_p + rows_p + st_s + rows_s
```

```python
import functools
import math

import jax
import jax.numpy as jnp
from jax import lax
from jax.experimental import pallas as pl
from jax.experimental.pallas import tpu as pltpu

F32, BF16, I32 = jnp.float32, jnp.bfloat16, jnp.int32

EPS = 1e-6
LANES = 128
PAGE = 128
S5_GROUP = 16
S5_STATE = 64
HEAD_DIM = 128
IDX_DIM = 64
IDX_HEADS = 8
TOPK = 256
Q_BLOCK = 128
KEY_CHUNK = 1024
COUNT_CHUNK = 512
MLSTM_CHUNK = 128
S5_CHUNK = 256
ROW_TILE = 512
FF_CHUNK = 1024
PAGES_PER_STEP = 16
INT_MIN = -2 ** 31
NEG_BIG = -1e30
VMEM_LIMIT = 56 * 1024 * 1024


def _cparams(*sem):
    return pltpu.CompilerParams(dimension_semantics=sem, vmem_limit_bytes=VMEM_LIMIT)


def _rms(x, g):
    return x * lax.rsqrt(jnp.mean(x * x, axis=-1, keepdims=True) + EPS) * g


def _dot(a, b):
    return jnp.dot(a, b, preferred_element_type=F32)


def _dot_nt(a, b):
    return lax.dot_general(a, b, (((1,), (1,)), ((), ())), preferred_element_type=F32)


def _dot_tn(a, b):
    return lax.dot_general(a, b, (((0,), (0,)), ((), ())), preferred_element_type=F32)


def _proj0_kernel(x_ref, g_ref, w_ref, o_ref):
    xn = _rms(x_ref[...], g_ref[...]).astype(BF16)
    o_ref[...] = _dot(xn, w_ref[...])


def _proj0(x, g, w):
    n, d = x.shape
    tm = min(ROW_TILE, n)
    wtot = w.shape[1]
    return pl.pallas_call(
        _proj0_kernel,
        grid=(n // tm,),
        in_specs=[pl.BlockSpec((tm, d), lambda i: (i, 0)),
                  pl.BlockSpec((1, d), lambda i: (0, 0)),
                  pl.BlockSpec((d, wtot), lambda i: (0, 0))],
        out_specs=pl.BlockSpec((tm, wtot), lambda i: (i, 0)),
        out_shape=jax.ShapeDtypeStruct((n, wtot), F32),
        compiler_params=_cparams("parallel"),
        name="proj0",
    )(x, g, w)


_Q1, _K1, _V1, _QI1, _KI1, _END1 = 0, 1024, 1280, 1536, 2048, 2176
LOG2E = 1.4426950408889634


def _proj1_kernel(x_ref, g_ref, w_ref, wvt_ref, wwit_ref, q_ref, k_ref, v_ref, kbf_ref, qi_ref,
                  kidx_ref, kidxbf_ref, vt_ref, wit_ref):
    xn = _rms(x_ref[...], g_ref[...]).astype(BF16)
    z = _dot(xn, w_ref[...])
    q_ref[...] = (z[:, _Q1:_K1] * (HEAD_DIM ** -0.5 * LOG2E)).astype(BF16)
    k = z[:, _K1:_V1]
    k_ref[...] = k
    v_ref[...] = z[:, _V1:_QI1]
    kbf_ref[...] = k.astype(BF16)
    qi_ref[...] = z[:, _QI1:_KI1].astype(BF16)
    kidx = z[:, _KI1:_KI1 + IDX_DIM]
    kidx_ref[...] = kidx
    kidxbf_ref[...] = kidx.astype(BF16)
    vt_ref[...] = _dot_nt(wvt_ref[...], xn).astype(BF16)
    wit_ref[...] = _dot_nt(wwit_ref[...], xn) * ((IDX_DIM ** -0.5) * (IDX_HEADS ** -0.5))


def _proj1(x, g, w, wvt, wwit):
    n, d = x.shape
    tm = min(ROW_TILE, n)
    kvw = wvt.shape[0]
    row = lambda width: pl.BlockSpec((tm, width), lambda i: (i, 0))
    col = lambda height: pl.BlockSpec((height, tm), lambda i: (0, i))
    full = lambda a: pl.BlockSpec(a.shape, lambda i: (0, 0))
    shp = lambda width, dt: jax.ShapeDtypeStruct((n, width), dt)
    return pl.pallas_call(
        _proj1_kernel,
        grid=(n // tm,),
        in_specs=[row(d), pl.BlockSpec((1, d), lambda i: (0, 0)), full(w), full(wvt), full(wwit)],
        out_specs=[row(1024), row(kvw), row(kvw), row(kvw), row(512), row(IDX_DIM), row(IDX_DIM),
                   col(kvw), col(IDX_HEADS)],
        out_shape=[shp(1024, BF16), shp(kvw, F32), shp(kvw, F32), shp(kvw, BF16), shp(512, BF16),
                   shp(IDX_DIM, F32), shp(IDX_DIM, BF16),
                   jax.ShapeDtypeStruct((kvw, n), BF16), jax.ShapeDtypeStruct((IDX_HEADS, n), F32)],
        compiler_params=_cparams("parallel"),
        name="proj1",
    )(x, g, w, wvt, wwit)


def _post_kernel(h_ref, ya_ref, yb_ref, wo_ref, g_ref, wup_ref, wdn_ref, gf_ref, out_ref,
                 h1_s, xn_s, acc_s, *, final_norm):
    j = pl.program_id(1)
    half = ya_ref.shape[1]

    @pl.when(j == 0)
    def _():
        h1 = h_ref[...] + _dot(ya_ref[...], wo_ref[:half, :]) + _dot(yb_ref[...], wo_ref[half:, :])
        h1_s[...] = h1
        xn_s[...] = _rms(h1, g_ref[...]).astype(BF16)
        acc_s[...] = jnp.zeros_like(acc_s)

    r = jnp.maximum(_dot(xn_s[...], wup_ref[...]), 0.0)
    acc_s[...] += _dot((r * r).astype(BF16), wdn_ref[...])

    @pl.when(j == pl.num_programs(1) - 1)
    def _():
        o = h1_s[...] + acc_s[...]
        if final_norm:
            o = _rms(o, gf_ref[...])
        out_ref[...] = o


def _post(h, ya, yb_spec_arg, wo, g, wup, wdn, gf, *, final_norm):
    n, d = h.shape
    tm = min(ROW_TILE, n)
    yb, yb_col = yb_spec_arg
    half = d // 2
    dff = wup.shape[1]
    return pl.pallas_call(
        functools.partial(_post_kernel, final_norm=final_norm),
        grid=(n // tm, dff // FF_CHUNK),
        in_specs=[pl.BlockSpec((tm, d), lambda i, j: (i, 0)),
                  pl.BlockSpec((tm, half), lambda i, j: (i, 0)),
                  pl.BlockSpec((tm, half), lambda i, j: (i, yb_col)),
                  pl.BlockSpec((d, d), lambda i, j: (0, 0)),
                  pl.BlockSpec((1, d), lambda i, j: (0, 0)),
                  pl.BlockSpec((d, FF_CHUNK), lambda i, j: (0, j)),
                  pl.BlockSpec((FF_CHUNK, d), lambda i, j: (j, 0)),
                  pl.BlockSpec((1, d), lambda i, j: (0, 0))],
        out_specs=pl.BlockSpec((tm, d), lambda i, j: (i, 0)),
        out_shape=jax.ShapeDtypeStruct((n, d), F32),
        scratch_shapes=[pltpu.VMEM((tm, d), F32), pltpu.VMEM((tm, d), BF16), pltpu.VMEM((tm, d), F32)],
        compiler_params=_cparams("parallel", "arbitrary"),
        name="post_final" if final_norm else "post",
    )(h, ya, yb, wo, g, wup, wdn, gf)


S5_SPLIT = 2


def _s5_input(u, wb_ref):
    width, nstate = wb_ref.shape[0], wb_ref.shape[1] // 2
    wi, ns = width // S5_SPLIT, nstate // S5_SPLIT
    u_bf = u.astype(BF16)
    part = lambda base: jnp.concatenate(
        [_dot(u_bf[:, r * wi:(r + 1) * wi], wb_ref[r * wi:(r + 1) * wi, base + r * ns:base + (r + 1) * ns])
         for r in range(S5_SPLIT)], axis=1)
    return part(0), part(nstate)


def _s5_output(hre, him, u, wc_ref, d_ref, wg_ref, bg_ref):
    nstate, width = hre.shape[1], wc_ref.shape[1]
    wi, ns = width // S5_SPLIT, nstate // S5_SPLIT
    hre_bf, him_bf = hre.astype(BF16), him.astype(BF16)
    y = jnp.concatenate(
        [_dot(hre_bf[:, r * ns:(r + 1) * ns], wc_ref[r * ns:(r + 1) * ns, r * wi:(r + 1) * wi])
         + _dot(him_bf[:, r * ns:(r + 1) * ns], wc_ref[nstate + r * ns:nstate + (r + 1) * ns, r * wi:(r + 1) * wi])
         for r in range(S5_SPLIT)], axis=1)
    y = jax.nn.gelu(y + d_ref[...] * u)
    gate = jax.nn.sigmoid(_dot(y.astype(BF16), wg_ref[...]) + bg_ref[...])
    return (y * gate).astype(BF16)


def _s5_seq_kernel(u_ref, h0_ref, lam_ref, wb_ref, wc_ref, d_ref, wg_ref, bg_ref, y_ref, ht_ref,
                   hre_s, him_s, carry_s):
    half = hre_s.shape[1]

    @pl.when(pl.program_id(1) == 0)
    def _():
        carry_s[...] = h0_ref[...]

    u = u_ref[...]
    hre_s[...], him_s[...] = _s5_input(u, wb_ref)
    a_re = lam_ref[0:1, :]
    a_im = lam_ref[1:2, :]

    def step(t, carry):
        h_re, h_im = carry
        n_re = a_re * h_re - a_im * h_im + hre_s[pl.ds(t, 1), :]
        n_im = a_re * h_im + a_im * h_re + him_s[pl.ds(t, 1), :]
        hre_s[pl.ds(t, 1), :] = n_re
        him_s[pl.ds(t, 1), :] = n_im
        return n_re, n_im

    h_re, h_im = lax.fori_loop(0, hre_s.shape[0], step, (carry_s[:, :half], carry_s[:, half:]), unroll=8)
    carry_s[:, :half] = h_re
    carry_s[:, half:] = h_im
    ht_ref[...] = carry_s[...]
    y_ref[...] = _s5_output(hre_s[...], him_s[...], u, wc_ref, d_ref, wg_ref, bg_ref)


def _s5_step_kernel(u_ref, h0_ref, lam_ref, wb_ref, wc_ref, d_ref, wg_ref, bg_ref, y_ref, ht_ref):
    half = lam_ref.shape[1]
    u = u_ref[...]
    bu_re, bu_im = _s5_input(u, wb_ref)
    a_re = lam_ref[0:1, :]
    a_im = lam_ref[1:2, :]
    h_re = h0_ref[:, :half]
    h_im = h0_ref[:, half:]
    n_re = a_re * h_re - a_im * h_im + bu_re
    n_im = a_re * h_im + a_im * h_re + bu_im
    ht_ref[:, :half] = n_re
    ht_ref[:, half:] = n_im
    y_ref[...] = _s5_output(n_re, n_im, u, wc_ref, d_ref, wg_ref, bg_ref)


def _s5_param_specs(width, nstate, imap):
    return [pl.BlockSpec((2, nstate), imap),
            pl.BlockSpec((width, 2 * nstate), imap),
            pl.BlockSpec((2 * nstate, width), imap),
            pl.BlockSpec((1, width), imap),
            pl.BlockSpec((width, width), imap),
            pl.BlockSpec((1, width), imap)]


def _s5_seq(z0, h0, params, width):
    bsz, t, _ = z0.shape
    nstate = params[0].shape[1]
    ts = min(S5_CHUNK, t)
    const = lambda b, c: (0, 0)
    return pl.pallas_call(
        _s5_seq_kernel,
        grid=(bsz, t // ts),
        in_specs=[pl.BlockSpec((None, ts, width), lambda b, c: (b, c, 0)),
                  pl.BlockSpec((None, 1, 2 * nstate), lambda b, c: (b, 0, 0))]
                 + _s5_param_specs(width, nstate, const),
        out_specs=[pl.BlockSpec((None, ts, width), lambda b, c: (b, c, 0)),
                   pl.BlockSpec((None, 1, 2 * nstate), lambda b, c: (b, 0, 0))],
        out_shape=[jax.ShapeDtypeStruct((bsz, t, width), BF16),
                   jax.ShapeDtypeStruct((bsz, 1, 2 * nstate), F32)],
        scratch_shapes=[pltpu.VMEM((ts, nstate), F32), pltpu.VMEM((ts, nstate), F32),
                        pltpu.VMEM((1, 2 * nstate), F32)],
        compiler_params=_cparams("parallel", "arbitrary"),
        name="s5_scan",
    )(z0, h0, *params)


def _s5_step(z0, h0, params, width):
    rows = z0.shape[0]
    nstate = params[0].shape[1]
    const = lambda i: (0, 0)
    return pl.pallas_call(
        _s5_step_kernel,
        grid=(1,),
        in_specs=[pl.BlockSpec((rows, width), const), pl.BlockSpec((rows, 2 * nstate), const)]
                 + _s5_param_specs(width, nstate, const),
        out_specs=[pl.BlockSpec((rows, width), const), pl.BlockSpec((rows, 2 * nstate), const)],
        out_shape=[jax.ShapeDtypeStruct((rows, width), BF16),
                   jax.ShapeDtypeStruct((rows, 2 * nstate), F32)],
        compiler_params=_cparams("arbitrary"),
        name="s5_step",
    )(z0, h0, *params)


def _mlstm_chunk_kernel(q_ref, k_ref, v_ref, o_ref, gcol_ref, grow_ref, bcol_ref, brow_ref,
                        c0_ref, n0_ref, m0_ref, y_ref, c_ref, n_ref, m_ref):
    nh = c_ref.shape[0]
    ch = q_ref.shape[0]

    @pl.when(pl.program_id(1) == 0)
    def _():
        c_ref[...] = c0_ref[...]
        n_ref[...] = n0_ref[...]
        m_ref[...] = m0_ref[...]

    gcol = gcol_ref[...] + brow_ref[...]
    grow = grow_ref[...] + bcol_ref[...]
    t_idx = lax.broadcasted_iota(I32, (ch, ch), 0)
    s_idx = lax.broadcasted_iota(I32, (ch, ch), 1)
    causal = t_idx >= s_idx
    for h in range(nh):
        sl = slice(h * HEAD_DIM, (h + 1) * HEAD_DIM)
        q = q_ref[:, sl]
        k = k_ref[:, sl] * (HEAD_DIM ** -0.5)
        v = v_ref[:, sl]
        q_bf, k_bf, v_bf = q.astype(BF16), k.astype(BF16), v.astype(BF16)
        i_col = gcol[:, h:h + 1]
        i_row = grow[h:h + 1, :]
        lf_col = jax.nn.log_sigmoid(gcol[:, nh + h:nh + h + 1])
        lf_row = jax.nn.log_sigmoid(grow[nh + h:nh + h + 1, :])
        b_col = jnp.sum(jnp.where(causal, lf_row, 0.0), axis=1, keepdims=True)
        b_row = jnp.sum(jnp.where(causal, 0.0, lf_col) , axis=0, keepdims=True)
        b_row = b_row + lf_row
        m_prev = m_ref[h][:, 0:1]
        dmat = jnp.where(causal, b_col - b_row + i_row, -jnp.inf)
        a_col = b_col + m_prev
        mj = jnp.maximum(a_col, jnp.max(dmat, axis=1, keepdims=True))
        w_intra = jnp.exp(dmat - mj)
        w_inter = jnp.exp(a_col - mj)
        s = _dot_nt(q_bf, k_bf) * w_intra
        c_prev = c_ref[h]
        n_prev = n_ref[h]
        num = _dot(s.astype(BF16), v_bf) + w_inter * _dot(q_bf, c_prev.astype(BF16))
        den = jnp.sum(s, axis=1, keepdims=True) + w_inter * jnp.sum(q * n_prev, axis=1, keepdims=True)
        hout = num / jnp.maximum(jnp.abs(den), jnp.exp(-mj))
        y_ref[:, sl] = (jax.nn.sigmoid(o_ref[:, sl]) * hout).astype(BF16)
        m_new = mj[ch - 1:ch, :]
        b_last = b_col[ch - 1:ch, :]
        w_end = jnp.exp(b_last - b_col + i_col - m_new)
        decay = jnp.exp(b_last + m_prev - m_new)
        kw = k * w_end
        c_ref[h] = decay * c_prev + _dot_tn(kw.astype(BF16), v_bf)
        n_ref[h] = decay * n_prev + jnp.sum(kw, axis=0, keepdims=True)
        m_ref[h] = jnp.broadcast_to(m_new, (1, LANES))


def _mlstm_chunked(z0, gcol, grow, bias_col, bias_row, c0, n0, m0, nh):
    bsz, t, _ = z0.shape
    width = nh * HEAD_DIM
    ch = MLSTM_CHUNK
    zspec = lambda blk: pl.BlockSpec((None, ch, width), lambda b, c: (b, c, blk))
    state = lambda shape: pl.BlockSpec((None,) + shape, lambda b, c: (b,) + (0,) * len(shape))
    return pl.pallas_call(
        _mlstm_chunk_kernel,
        grid=(bsz, t // ch),
        in_specs=[zspec(1), zspec(2), zspec(3), zspec(4),
                  pl.BlockSpec((None, ch, 2 * nh), lambda b, c: (b, c, 0)),
                  pl.BlockSpec((None, 2 * nh, ch), lambda b, c: (b, 0, c)),
                  pl.BlockSpec((2 * nh, 1), lambda b, c: (0, 0)),
                  pl.BlockSpec((1, 2 * nh), lambda b, c: (0, 0)),
                  state((nh, HEAD_DIM, HEAD_DIM)), state((nh, 1, HEAD_DIM)), state((nh, 1, LANES))],
        out_specs=[pl.BlockSpec((None, ch, width), lambda b, c: (b, c, 0)),
                   state((nh, HEAD_DIM, HEAD_DIM)), state((nh, 1, HEAD_DIM)), state((nh, 1, LANES))],
        out_shape=[jax.ShapeDtypeStruct((bsz, t, width), BF16),
                   jax.ShapeDtypeStruct((bsz, nh, HEAD_DIM, HEAD_DIM), F32),
                   jax.ShapeDtypeStruct((bsz, nh, 1, HEAD_DIM), F32),
                   jax.ShapeDtypeStruct((bsz, nh, 1, LANES), F32)],
        compiler_params=_cparams("parallel", "arbitrary"),
        name="mlstm_chunk",
    )(z0, z0, z0, z0, gcol, grow, bias_col, bias_row, c0, n0, m0)


def _to_column(row):
    n = row.shape[1]
    eye = lax.broadcasted_iota(I32, (n, n), 0) == lax.broadcasted_iota(I32, (n, n), 1)
    return jnp.sum(jnp.where(eye, row, 0.0), axis=1, keepdims=True)


def _mlstm_step_kernel(q_ref, k_ref, v_ref, o_ref, g_ref, brow_ref, c0_ref, n0_ref, m0_ref,
                       y_ref, c_ref, n_ref, m_ref):
    nh = c_ref.shape[0]
    g = g_ref[...] + brow_ref[...]
    for h in range(nh):
        sl = slice(h * HEAD_DIM, (h + 1) * HEAD_DIM)
        q = q_ref[:, sl]
        k = k_ref[:, sl] * (HEAD_DIM ** -0.5)
        v = v_ref[:, sl]
        i_pre = g[:, h:h + 1]
        lf = jax.nn.log_sigmoid(g[:, nh + h:nh + h + 1])
        m_prev = m0_ref[h][:, 0:1]
        c_prev = c0_ref[h]
        n_prev = n0_ref[h]
        a = lf + m_prev
        mj = jnp.maximum(a, i_pre)
        w_intra = jnp.exp(i_pre - mj)
        w_inter = jnp.exp(a - mj)
        s = jnp.sum(q * k, axis=1, keepdims=True) * w_intra
        q_col = _to_column(q)
        k_col = _to_column(k)
        num = s * v + w_inter * jnp.sum(q_col * c_prev, axis=0, keepdims=True)
        den = s + w_inter * jnp.sum(q * n_prev, axis=1, keepdims=True)
        hout = num / jnp.maximum(jnp.abs(den), jnp.exp(-mj))
        y_ref[:, sl] = (jax.nn.sigmoid(o_ref[:, sl]) * hout).astype(BF16)
        w_end = jnp.exp(i_pre - mj)
        decay = jnp.exp(a - mj)
        c_ref[h] = decay * c_prev + (w_end * k_col) * v
        n_ref[h] = decay * n_prev + w_end * k
        m_ref[h] = jnp.broadcast_to(mj, (1, LANES))


def _mlstm_step(z0, g, bias_row, c0, n0, m0, nh):
    bsz = z0.shape[0]
    width = nh * HEAD_DIM
    zspec = lambda blk: pl.BlockSpec((None, 1, width), lambda b: (b, 0, blk))
    state = lambda shape: pl.BlockSpec((None,) + shape, lambda b: (b,) + (0,) * len(shape))
    return pl.pallas_call(
        _mlstm_step_kernel,
        grid=(bsz,),
        in_specs=[zspec(1), zspec(2), zspec(3), zspec(4),
                  pl.BlockSpec((None, 1, 2 * nh), lambda b: (b, 0, 0)),
                  pl.BlockSpec((1, 2 * nh), lambda b: (0, 0)),
                  state((nh, HEAD_DIM, HEAD_DIM)), state((nh, 1, HEAD_DIM)), state((nh, 1, LANES))],
        out_specs=[pl.BlockSpec((None, 1, width), lambda b: (b, 0, 0)),
                   state((nh, HEAD_DIM, HEAD_DIM)), state((nh, 1, HEAD_DIM)), state((nh, 1, LANES))],
        out_shape=[jax.ShapeDtypeStruct((bsz, 1, width), BF16),
                   jax.ShapeDtypeStruct((bsz, nh, HEAD_DIM, HEAD_DIM), F32),
                   jax.ShapeDtypeStruct((bsz, nh, 1, HEAD_DIM), F32),
                   jax.ShapeDtypeStruct((bsz, nh, 1, LANES), F32)],
        compiler_params=_cparams("parallel"),
        name="mlstm_step",
    )(z0, z0, z0, z0, g, bias_row, c0, n0, m0)


def _key_to_float(key):
    bits = key ^ ((key >> 31) & jnp.int32(0x7FFFFFFF))
    return lax.bitcast_convert_type(bits, F32)


def _select_threshold(count, shape, width, j0_s):
    def bit_step(it, key):
        cand = key + (jnp.int32(1) << (31 - it))
        cand_f = _key_to_float(cand)
        cnt = count(lambda tile, col: tile >= cand_f)
        return jnp.where(cnt >= TOPK, cand, key)

    key = lax.fori_loop(0, 32, bit_step, jnp.full(shape, INT_MIN, I32))
    has_thr = key > INT_MIN
    thr = jnp.where(has_thr, _key_to_float(jnp.where(has_thr, key, 0)), -jnp.inf)
    need = TOPK - count(lambda tile, col: tile > thr)
    n_eq = count(lambda tile, col: tile == thr)
    j0_s[...] = jnp.where(has_thr, jnp.int32(width), jnp.int32(-1))
    surplus = jnp.max(jnp.where(has_thr & (n_eq > need), 1, 0))
    nbits = max(1, (width - 1).bit_length())

    @pl.when(surplus > 0)
    def _():
        def idx_step(it, j0):
            cand = j0 | (jnp.int32(1) << (nbits - 1 - it))
            cnt = count(lambda tile, col: (tile == thr) & (col < cand))
            return jnp.where(cnt < need, cand, j0)

        j0 = lax.fori_loop(0, nbits, idx_step, jnp.zeros(shape, I32))
        j0_s[...] = jnp.where(has_thr, j0, jnp.int32(-1))

    return thr, j0_s[...]


def _mask_bias(scores, cols, thr, j0):
    sel = (scores > thr) | ((scores == thr) & (cols <= j0))
    return jnp.where(sel, 0.0, NEG_BIG)


def _dsa_prompt_kernel(q_ref, qi_ref, wit_ref, ki_ref, k_ref, vt_ref, o_ref, sc_s, j0_s, m_s, acc_s):
    qb = q_ref.shape[0]
    kc_len = KEY_CHUNK
    nheads = q_ref.shape[1] // HEAD_DIM
    nkv = k_ref.shape[1] // HEAD_DIM
    rep = nheads // nkv
    i = pl.program_id(1)
    nchunks = ((i + 1) * qb + kc_len - 1) // kc_len
    qpos = i * qb + lax.broadcasted_iota(I32, (1, qb), 1)
    sub = lax.broadcasted_iota(I32, (kc_len, qb), 0)

    qi = qi_ref[...]
    wit = wit_ref[...]
    qi_all = jnp.concatenate([qi[:, h * IDX_DIM:(h + 1) * IDX_DIM] for h in range(IDX_HEADS)], axis=0)

    def score_chunk(c, _):
        off = pl.multiple_of(c * kc_len, kc_len)
        ki = ki_ref[pl.ds(off, kc_len), :]
        logits = _dot_nt(ki, qi_all)
        sc = jnp.zeros((kc_len, qb), F32)
        for h in range(IDX_HEADS):
            sc = sc + jnp.maximum(logits[:, h * qb:(h + 1) * qb], 0.0) * wit[h:h + 1, :]
        sc_s[pl.ds(off, kc_len), :] = jnp.where(sub + off <= qpos, sc, -jnp.inf)
        return 0

    lax.fori_loop(0, nchunks, score_chunk, 0)

    def count(pred):
        def body(c, acc):
            off = pl.multiple_of(c * COUNT_CHUNK, COUNT_CHUNK)
            ind = jnp.where(pred(sc_s[pl.ds(off, COUNT_CHUNK), :], sub[:COUNT_CHUNK] + off), 1, 0)
            return acc + jnp.sum(ind.reshape(COUNT_CHUNK // 8, 8, qb), axis=0)

        ncount = ((i + 1) * qb + COUNT_CHUNK - 1) // COUNT_CHUNK
        acc = lax.fori_loop(0, ncount, body, jnp.zeros((8, qb), I32))
        return jnp.sum(acc, axis=0, keepdims=True)

    thr, j0 = _select_threshold(count, (1, qb), sc_s.shape[0], j0_s)

    m_s[...] = jnp.full_like(m_s, NEG_BIG)
    acc_s[...] = jnp.zeros_like(acc_s)
    q = q_ref[...]
    q_g = [jnp.concatenate([q[:, (g * rep + r) * HEAD_DIM:(g * rep + r + 1) * HEAD_DIM]
                            for r in range(rep)], axis=0) for g in range(nkv)]
    ones_rows = jnp.ones((acc_s.shape[1] - HEAD_DIM, kc_len), BF16)

    def attend_chunk(c, _):
        off = pl.multiple_of(c * kc_len, kc_len)
        bias = _mask_bias(sc_s[pl.ds(off, kc_len), :], sub + off, thr, j0)
        bias = jnp.concatenate([bias] * rep, axis=1)
        for g in range(nkv):
            kc = k_ref[pl.ds(off, kc_len), g * HEAD_DIM:(g + 1) * HEAD_DIM]
            att = _dot_nt(kc, q_g[g]) + bias
            m_old = m_s[g]
            m_new = jnp.maximum(m_old, jnp.max(att, axis=0, keepdims=True))
            p = jnp.exp2(att - m_new).astype(BF16)
            vt = jnp.concatenate([vt_ref[g * HEAD_DIM:(g + 1) * HEAD_DIM, pl.ds(off, kc_len)], ones_rows],
                                 axis=0)
            acc_s[g] = jnp.exp2(m_old - m_new) * acc_s[g] + _dot(vt, p)
            m_s[g] = m_new
        return 0

    lax.fori_loop(0, nchunks, attend_chunk, 0)

    for g in range(nkv):
        acc = acc_s[g]
        out = acc[:HEAD_DIM, :] / acc[HEAD_DIM:HEAD_DIM + 1, :]
        for r in range(rep):
            hd = g * rep + r
            o_ref[:, hd * HEAD_DIM:(hd + 1) * HEAD_DIM] = out[:, r * qb:(r + 1) * qb].T.astype(BF16)


def _dsa_prompt(q, qi, wit, kidx, kbf, vt):
    bsz, t, width = q.shape
    kvw = kbf.shape[2]
    nkv = kvw // HEAD_DIM
    rep = width // HEAD_DIM // nkv
    qb = Q_BLOCK
    nq = t // qb
    ones_rows = 16
    return pl.pallas_call(
        _dsa_prompt_kernel,
        grid=(bsz, nq),
        in_specs=[pl.BlockSpec((None, qb, width), lambda b, i: (b, i, 0)),
                  pl.BlockSpec((None, qb, qi.shape[2]), lambda b, i: (b, i, 0)),
                  pl.BlockSpec((IDX_HEADS, qb), lambda b, i: (0, b * nq + i)),
                  pl.BlockSpec((None, t, IDX_DIM), lambda b, i: (b, 0, 0)),
                  pl.BlockSpec((None, t, kvw), lambda b, i: (b, 0, 0)),
                  pl.BlockSpec((kvw, t), lambda b, i: (0, b))],
        out_specs=pl.BlockSpec((None, qb, width), lambda b, i: (b, i, 0)),
        out_shape=jax.ShapeDtypeStruct((bsz, t, width), BF16),
        scratch_shapes=[pltpu.VMEM((t, qb), F32), pltpu.VMEM((1, qb), I32),
                        pltpu.VMEM((nkv, 1, rep * qb), F32),
                        pltpu.VMEM((nkv, HEAD_DIM + ones_rows, rep * qb), F32)],
        compiler_params=_cparams("parallel", "arbitrary"),
        name="dsa_prompt",
    )(q, qi, wit, kidx, kbf, vt)


def _dsa_score_kernel(pt_ref, qi2_ref, qi_ref, wi_ref, knew_ref, *rest):
    pages, (keys_ref, newkey_ref) = rest[:PAGES_PER_STEP], rest[PAGES_PER_STEP:]
    qi = qi_ref[...]
    wi = wi_ref[...]
    nh = qi.shape[0]
    ki = jnp.concatenate([page_ref[...].astype(BF16) for page_ref in pages], axis=0)
    logits = jnp.maximum(_dot_nt(qi2_ref[...].astype(BF16), ki), 0.0)
    half = logits.shape[1]
    keys_ref[:, :half] = jnp.sum(logits[:nh] * wi, axis=0, keepdims=True)
    keys_ref[:, half:] = jnp.sum(logits[nh:] * wi, axis=0, keepdims=True)
    logit_new = jnp.sum(qi * knew_ref[...], axis=1, keepdims=True)
    sc_new = jnp.sum(jnp.maximum(logit_new, 0.0) * wi, axis=0, keepdims=True)
    newkey_ref[...] = jnp.broadcast_to(sc_new, (1, LANES))


def _dsa_score(page_table, qi, wi, kidx_new, cache_kidx):
    bsz, npages = page_table.shape
    steps = npages // PAGES_PER_STEP
    prow = PAGE * IDX_DIM // LANES
    page_spec = lambda p: pl.BlockSpec((prow, LANES), lambda b, j, pt: (pt[b, j * PAGES_PER_STEP + p], 0))
    per_seq = lambda shape: pl.BlockSpec((None,) + shape, lambda b, j, pt: (b, 0, 0))
    zero = jnp.zeros_like(qi)
    qi2 = jnp.concatenate([jnp.concatenate([qi, zero], axis=2), jnp.concatenate([zero, qi], axis=2)], axis=1)
    scores, new = pl.pallas_call(
        _dsa_score_kernel,
        grid_spec=pltpu.PrefetchScalarGridSpec(
            num_scalar_prefetch=1,
            grid=(bsz, steps),
            in_specs=[per_seq((2 * IDX_HEADS, LANES)), per_seq((IDX_HEADS, IDX_DIM)), per_seq((IDX_HEADS, 1)),
                      per_seq((1, IDX_DIM))]
                     + [page_spec(p) for p in range(PAGES_PER_STEP)],
            out_specs=[pl.BlockSpec((None, 1, PAGES_PER_STEP * PAGE), lambda b, j, pt: (b, 0, j)),
                       per_seq((1, LANES))]),
        out_shape=[jax.ShapeDtypeStruct((bsz, 1, npages * PAGE), F32),
                   jax.ShapeDtypeStruct((bsz, 1, LANES), F32)],
        compiler_params=_cparams("parallel", "arbitrary"),
        name="dsa_decode_score",
    )(page_table, qi2, qi, wi, kidx_new, *([cache_kidx] * PAGES_PER_STEP))
    scores = scores.reshape(bsz, steps, 2, PAGES_PER_STEP, prow)
    return jnp.moveaxis(scores, 2, 4).reshape(bsz, 1, npages * PAGE), new


def _dsa_select_kernel(keys_ref, thr_ref, j0_ref, j0_s):
    rows, width = keys_ref.shape
    lane = lax.broadcasted_iota(I32, (rows, LANES), 1)

    def count(pred):
        def body(t, acc):
            off = pl.multiple_of(t * LANES, LANES)
            return acc + jnp.where(pred(keys_ref[:, pl.ds(off, LANES)], lane + off), 1, 0)

        acc = lax.fori_loop(0, width // LANES, body, jnp.zeros((rows, LANES), I32))
        return jnp.sum(acc, axis=1, keepdims=True)

    thr, j0 = _select_threshold(count, (rows, 1), width, j0_s)
    thr_ref[...] = jnp.broadcast_to(thr, thr_ref.shape)
    j0_ref[...] = jnp.broadcast_to(j0, j0_ref.shape)


def _dsa_select(keys):
    rows, width = keys.shape
    const = lambda i: (0, 0)
    return pl.pallas_call(
        _dsa_select_kernel,
        grid=(1,),
        in_specs=[pl.BlockSpec((rows, width), const)],
        out_specs=[pl.BlockSpec((rows, LANES), const), pl.BlockSpec((rows, LANES), const)],
        out_shape=[jax.ShapeDtypeStruct((rows, LANES), F32), jax.ShapeDtypeStruct((rows, LANES), I32)],
        scratch_shapes=[pltpu.VMEM((rows, 1), I32)],
        compiler_params=_cparams("arbitrary"),
        name="dsa_decode_select",
    )(keys)


def _dsa_decode_kernel(pt_ref, q_ref, keys_ref, tail_ref, thr_ref, j0_ref, knew_ref, vnew_ref, *rest,
                       n_past, nkv):
    j = pl.program_id(1)
    nheads = q_ref.shape[0]
    rep = nheads // nkv
    kpages, vpages = rest[:PAGES_PER_STEP], rest[PAGES_PER_STEP:2 * PAGES_PER_STEP]
    o_ref, m_s, l_s, acc_s = rest[2 * PAGES_PER_STEP:]

    @pl.when(j == 0)
    def _():
        m_s[...] = jnp.full_like(m_s, NEG_BIG)
        l_s[...] = jnp.zeros_like(l_s)
        acc_s[...] = jnp.zeros_like(acc_s)

    q = q_ref[...]
    thr = thr_ref[:, 0:1]
    j0 = j0_ref[:, 0:1]
    width = PAGES_PER_STEP * PAGE * nkv
    row = lax.broadcasted_iota(I32, (1, width), 1) + j * width
    head_kv = lax.broadcasted_iota(I32, (nheads, 1), 0) // rep

    def update(att, value_fn):
        m_old = m_s[...]
        m_new = jnp.maximum(m_old, jnp.max(att, axis=1, keepdims=True))
        alpha = jnp.exp2(m_old - m_new)
        p = jnp.exp2(att - m_new)
        l_s[...] = alpha * l_s[...] + jnp.sum(p, axis=1, keepdims=True)
        acc_s[...] = alpha * acc_s[...] + value_fn(p)
        m_s[...] = m_new

    kcat = jnp.concatenate([r[...].astype(BF16) for r in kpages], axis=0)
    vcat = jnp.concatenate([r[...].astype(BF16) for r in vpages], axis=0)
    bias = _mask_bias(keys_ref[...], row // nkv, thr, j0)
    bias = jnp.where(row % nkv == head_kv, bias, NEG_BIG)
    update(_dot_nt(q.astype(BF16), kcat) + bias, lambda pr: _dot(pr.astype(BF16), vcat))

    @pl.when(j == pl.num_programs(1) - 1)
    def _():
        k_new = jnp.zeros_like(q)
        v_new = jnp.zeros_like(q)
        for g in range(nkv):
            k_new = jnp.where(head_kv == g, knew_ref[g:g + 1, :], k_new)
            v_new = jnp.where(head_kv == g, vnew_ref[g:g + 1, :], v_new)
        bias_new = _mask_bias(tail_ref[:, 0:1], jnp.int32(n_past), thr, j0)
        att_new = jnp.sum(q * k_new, axis=1, keepdims=True) + bias_new
        update(att_new, lambda pr: pr * v_new)
        o_ref[...] = (acc_s[...] / l_s[...]).astype(BF16)


def _dsa_decode(page_table, q, keys, tail, thr, j0, k_new, v_new, cache_k, cache_v):
    bsz, npages = page_table.shape
    nheads = q.shape[1]
    nkv = k_new.shape[1]
    steps = npages // PAGES_PER_STEP
    page_spec = lambda p: pl.BlockSpec(
        (PAGE * nkv, HEAD_DIM), lambda b, j, pt: (pt[b, j * PAGES_PER_STEP + p], 0))
    per_seq = lambda shape: pl.BlockSpec((None,) + shape, lambda b, j, pt: (b, 0, 0))
    keys_spec = pl.BlockSpec((None, 1, PAGES_PER_STEP * PAGE * nkv), lambda b, j, pt: (b, 0, j))
    return pl.pallas_call(
        functools.partial(_dsa_decode_kernel, n_past=npages * PAGE, nkv=nkv),
        grid_spec=pltpu.PrefetchScalarGridSpec(
            num_scalar_prefetch=1,
            grid=(bsz, steps),
            in_specs=[per_seq((nheads, HEAD_DIM)), keys_spec, per_seq((1, LANES)), per_seq((1, LANES)),
                      per_seq((1, LANES)), per_seq((nkv, HEAD_DIM)), per_seq((nkv, HEAD_DIM))]
                     + [page_spec(p) for p in range(PAGES_PER_STEP)] * 2,
            out_specs=per_seq((nheads, HEAD_DIM)),
            scratch_shapes=[pltpu.VMEM((nheads, 1), F32), pltpu.VMEM((nheads, 1), F32),
                            pltpu.VMEM((nheads, HEAD_DIM), F32)]),
        out_shape=jax.ShapeDtypeStruct((bsz, nheads, HEAD_DIM), BF16),
        compiler_params=_cparams("parallel", "arbitrary"),
        name="dsa_decode_attend",
    )(page_table, q, keys, tail, thr, j0, k_new, v_new,
      *([cache_k] * PAGES_PER_STEP), *([cache_v] * PAGES_PER_STEP))


def _pad_cols(w, width):
    return jnp.pad(w, ((0, 0), (0, width - w.shape[1])))


def _block_diag(blocks):
    g, r, c = blocks.shape
    eye = jnp.eye(g, dtype=blocks.dtype)
    return (blocks[:, :, None, :] * eye[:, None, :, None]).reshape(g * r, g * c)


def _s5_params(lam_re, lam_im, log_dt, b_re, b_im, c_re, c_im, d_skip, w_glu, b_glu):
    dt = jnp.exp(log_dt)[:, None]
    mag = jnp.exp(lam_re * dt)
    bar_re = mag * jnp.cos(lam_im * dt)
    bar_im = mag * jnp.sin(lam_im * dt)
    inv = 1.0 / (lam_re * lam_re + lam_im * lam_im)
    coef_re = (((bar_re - 1.0) * lam_re + bar_im * lam_im) * inv)[..., None]
    coef_im = ((bar_im * lam_re - (bar_re - 1.0) * lam_im) * inv)[..., None]
    bb_re = coef_re * b_re - coef_im * b_im
    bb_im = coef_re * b_im + coef_im * b_re
    lam_rows = jnp.stack([bar_re.reshape(-1), bar_im.reshape(-1)])
    to_in = lambda z: _block_diag(jnp.swapaxes(z, 1, 2))
    wb = jnp.concatenate([to_in(bb_re), to_in(bb_im)], axis=1).astype(BF16)
    to_out = lambda z: _block_diag(jnp.swapaxes(z, 1, 2))
    wc = jnp.concatenate([to_out(c_re), to_out(-c_im)], axis=0).astype(BF16)
    return (lam_rows, wb, wc, d_skip[None, :], w_glu.astype(BF16), b_glu[None, :])


def kernel(x_prompt, x_sample, state_s5_re, state_s5_im, state_mlstm_c, state_mlstm_n, state_mlstm_m,
           cache_k, cache_v, cache_kidx, page_table, norm_mix, norm_mlp, norm_final, w_in0, s5_lam_re,
           s5_lam_im, s5_log_dt, s5_b_re, s5_b_im, s5_c_re, s5_c_im, s5_d, w_glu, b_glu, b_igate,
           b_fgate, w_out0, w_in1, w_out1, w_up, w_down):
    bp, tp, d = x_prompt.shape
    db, ts, _ = x_sample.shape
    assert ts == 1, "the decode path handles one new token per sequence"
    s5_groups, s5_state = s5_lam_re.shape
    s5_width = s5_groups * S5_GROUP
    nstate = s5_groups * s5_state
    nh = b_igate.shape[0]
    ml_width = nh * HEAD_DIM
    assert s5_width == ml_width == 512 and d == 1024
    n_past = page_table.shape[1] * PAGE
    kvw = cache_k.shape[2] * cache_k.shape[3]
    nheads = w_out1.shape[0] // HEAD_DIM

    gate_cols = s5_width + 4 * ml_width
    w0 = jnp.concatenate([w_in0[:, :gate_cols], _pad_cols(w_in0[:, gate_cols:], LANES)], axis=1).astype(BF16)
    s5p = _s5_params(s5_lam_re, s5_lam_im, s5_log_dt, s5_b_re, s5_b_im, s5_c_re, s5_c_im, s5_d, w_glu, b_glu)
    gate_bias = jnp.concatenate([b_igate, b_fgate])
    bias_row, bias_col = gate_bias[None, :], gate_bias[:, None]
    ki0 = 1024 + 2 * kvw + IDX_HEADS * IDX_DIM
    w1 = jnp.concatenate([w_in1[:, :ki0], _pad_cols(w_in1[:, ki0:ki0 + IDX_DIM], LANES)], axis=1).astype(BF16)
    wvt = w_in1[:, 1024 + kvw:1024 + 2 * kvw].T.astype(BF16)
    wwit = w_in1[:, ki0 + IDX_DIM:].T.astype(BF16)
    wo0, wo1 = w_out0.astype(BF16), w_out1.astype(BF16)
    wup, wdn = w_up.astype(BF16), w_down.astype(BF16)
    g_mix, g_mlp, g_fin = norm_mix[:, None, :], norm_mlp[:, None, :], norm_final[None, :]

    def trunk(x2d, bsz, t, s5_h0, c0, n0, m0, attend):
        n = bsz * t
        z0 = _proj0(x2d, g_mix[0], w0)
        gates = z0[:, gate_cols:gate_cols + 2 * nh]
        m0b = jnp.broadcast_to(m0[:, :, None, None], (bsz, nh, 1, LANES))
        n0r = n0[:, :, None, :]
        if t == 1:
            y_s5, h_t = _s5_step(z0, s5_h0, s5p, s5_width)
            y_ml, c_t, n_t, m_t = _mlstm_step(z0.reshape(bsz, 1, -1), gates.reshape(bsz, 1, 2 * nh),
                                              bias_row, c0, n0r, m0b, nh)
        else:
            y_s5, h_t = _s5_seq(z0.reshape(bsz, t, -1), s5_h0.reshape(bsz, 1, -1), s5p, s5_width)
            g3 = gates.reshape(bsz, t, 2 * nh)
            y_ml, c_t, n_t, m_t = _mlstm_chunked(z0.reshape(bsz, t, -1), g3, jnp.swapaxes(g3, 1, 2),
                                                 bias_col, bias_row, c0, n0r, m0b, nh)
        h_t = h_t.reshape(bsz, 2, s5_groups, s5_state)
        states = (h_t[:, 0], h_t[:, 1], c_t, n_t[:, :, 0, :], m_t[:, :, 0, 0])
        h1 = _post(x2d, y_s5.reshape(n, s5_width), (y_ml.reshape(n, ml_width), 0), wo0, g_mlp[0],
                   wup[0], wdn[0], g_fin, final_norm=False)
        q, k, v, kbf, qi, kidx, kidxbf, vt, wit = _proj1(h1, g_mix[1], w1, wvt, wwit)
        o = attend(q, k, v, kbf, qi, kidx, kidxbf, vt, wit)
        y = _post(h1, o, (o, 1), wo1, g_mlp[1], wup[1], wdn[1], g_fin, final_norm=True)
        rows = (k.reshape(bsz, t, -1, HEAD_DIM), v.reshape(bsz, t, -1, HEAD_DIM), kidx.reshape(bsz, t, IDX_DIM))
        return y.reshape(bsz, t, d), states, rows

    def attend_prompt(q, k, v, kbf, qi, kidx, kidxbf, vt, wit):
        r3 = lambda z: z.reshape(bp, tp, -1)
        return _dsa_prompt(r3(q), r3(qi), wit, r3(kidxbf), r3(kbf), vt).reshape(bp * tp, -1)

    def attend_decode(q, k, v, kbf, qi, kidx, kidxbf, vt, wit):
        qi3 = qi.astype(F32).reshape(db, IDX_HEADS, IDX_DIM)
        wi3 = wit.T.reshape(db, IDX_HEADS, 1)
        keys, newkey = _dsa_score(page_table, qi3, wi3, kidx.reshape(db, 1, IDX_DIM),
                                  cache_kidx.reshape(-1, LANES))
        tail = jnp.where(lax.broadcasted_iota(I32, (db, LANES), 1) == 0, newkey[:, 0, :], -jnp.inf)
        keys = jnp.concatenate([keys[:, 0, :], tail], axis=1)
        thr, j0 = _dsa_select(keys)
        nkv = kvw // HEAD_DIM
        keys_rows = jnp.repeat(keys[:, :n_past], nkv, axis=1)[:, None, :]
        o = _dsa_decode(page_table, q.astype(F32).reshape(db, nheads, HEAD_DIM), keys_rows,
                        keys[:, None, n_past:], thr[:, None, :], j0[:, None, :],
                        k.reshape(db, nkv, HEAD_DIM), v.reshape(db, nkv, HEAD_DIM),
                        cache_k.reshape(-1, HEAD_DIM), cache_v.reshape(-1, HEAD_DIM))
        return o.reshape(db, nheads * HEAD_DIM)

    zeros = lambda *shape: jnp.zeros(shape, F32)
    y_p, st_p, rows_p = trunk(x_prompt.reshape(bp * tp, d), bp, tp, zeros(bp, 2 * nstate),
                              zeros(bp, nh, HEAD_DIM, HEAD_DIM), zeros(bp, nh, HEAD_DIM), zeros(bp, nh),
                              attend_prompt)
    s5_h0 = jnp.concatenate([state_s5_re.reshape(db, nstate), state_s5_im.reshape(db, nstate)], axis=1)
    y_s, st_s, rows_s = trunk(x_sample.reshape(db, d), db, 1, s5_h0, state_mlstm_c, state_mlstm_n,
                              state_mlstm_m, attend_decode)
    return (y_p, y_s) + st_p + rows_p + st_s + rows_s
```

```python
import functools
import math

import jax
import jax.numpy as jnp
from jax import lax
from jax.experimental import pallas as pl
from jax.experimental.pallas import tpu as pltpu

F32, BF16, I32 = jnp.float32, jnp.bfloat16, jnp.int32

EPS = 1e-6
LANES = 128
PAGE = 128
S5_GROUP = 16
S5_STATE = 64
HEAD_DIM = 128
IDX_DIM = 64
IDX_HEADS = 8
TOPK = 256
Q_BLOCK = 128
KEY_CHUNK = 1024
COUNT_CHUNK = 512
SORT_TILES = 256
MLSTM_CHUNK = 128
S5_CHUNK = 256
ROW_TILE = 512
FF_CHUNK = 1024
PAGES_PER_STEP = 16
INT_MIN = -2 ** 31
NEG_BIG = -1e30
VMEM_LIMIT = 56 * 1024 * 1024


def _cparams(*sem):
    return pltpu.CompilerParams(dimension_semantics=sem, vmem_limit_bytes=VMEM_LIMIT)


def _rms(x, g):
    return x * lax.rsqrt(jnp.mean(x * x, axis=-1, keepdims=True) + EPS) * g


def _dot(a, b):
    return jnp.dot(a, b, preferred_element_type=F32)


def _dot_nt(a, b):
    return lax.dot_general(a, b, (((1,), (1,)), ((), ())), preferred_element_type=F32)


def _dot_tn(a, b):
    return lax.dot_general(a, b, (((0,), (0,)), ((), ())), preferred_element_type=F32)


def _proj0_kernel(x_ref, g_ref, w_ref, o_ref):
    xn = _rms(x_ref[...], g_ref[...]).astype(BF16)
    o_ref[...] = _dot(xn, w_ref[...])


def _proj0(x, g, w):
    n, d = x.shape
    tm = min(ROW_TILE, n)
    wtot = w.shape[1]
    return pl.pallas_call(
        _proj0_kernel,
        grid=(n // tm,),
        in_specs=[pl.BlockSpec((tm, d), lambda i: (i, 0)),
                  pl.BlockSpec((1, d), lambda i: (0, 0)),
                  pl.BlockSpec((d, wtot), lambda i: (0, 0))],
        out_specs=pl.BlockSpec((tm, wtot), lambda i: (i, 0)),
        out_shape=jax.ShapeDtypeStruct((n, wtot), F32),
        compiler_params=_cparams("parallel"),
        name="proj0",
    )(x, g, w)


_Q1, _K1, _V1, _QI1, _KI1, _END1 = 0, 1024, 1280, 1536, 2048, 2176
LOG2E = 1.4426950408889634


def _proj1_kernel(x_ref, g_ref, w_ref, wvt_ref, wwit_ref, q_ref, k_ref, v_ref, kbf_ref, qi_ref,
                  kidx_ref, kidxbf_ref, vt_ref, wit_ref):
    xn = _rms(x_ref[...], g_ref[...]).astype(BF16)
    z = _dot(xn, w_ref[...])
    q_ref[...] = (z[:, _Q1:_K1] * (HEAD_DIM ** -0.5 * LOG2E)).astype(BF16)
    k = z[:, _K1:_V1]
    k_ref[...] = k
    v_ref[...] = z[:, _V1:_QI1]
    kbf_ref[...] = k.astype(BF16)
    qi_ref[...] = z[:, _QI1:_KI1].astype(BF16)
    kidx = z[:, _KI1:_KI1 + IDX_DIM]
    kidx_ref[...] = kidx
    kidxbf_ref[...] = kidx.astype(BF16)
    vt_ref[...] = _dot_nt(wvt_ref[...], xn).astype(BF16)
    wit_ref[...] = _dot_nt(wwit_ref[...], xn) * ((IDX_DIM ** -0.5) * (IDX_HEADS ** -0.5))


def _proj1(x, g, w, wvt, wwit):
    n, d = x.shape
    tm = min(ROW_TILE, n)
    kvw = wvt.shape[0]
    row = lambda width: pl.BlockSpec((tm, width), lambda i: (i, 0))
    col = lambda height: pl.BlockSpec((height, tm), lambda i: (0, i))
    full = lambda a: pl.BlockSpec(a.shape, lambda i: (0, 0))
    shp = lambda width, dt: jax.ShapeDtypeStruct((n, width), dt)
    return pl.pallas_call(
        _proj1_kernel,
        grid=(n // tm,),
        in_specs=[row(d), pl.BlockSpec((1, d), lambda i: (0, 0)), full(w), full(wvt), full(wwit)],
        out_specs=[row(1024), row(kvw), row(kvw), row(kvw), row(512), row(IDX_DIM), row(IDX_DIM),
                   col(kvw), col(IDX_HEADS)],
        out_shape=[shp(1024, BF16), shp(kvw, F32), shp(kvw, F32), shp(kvw, BF16), shp(512, BF16),
                   shp(IDX_DIM, F32), shp(IDX_DIM, BF16),
                   jax.ShapeDtypeStruct((kvw, n), BF16), jax.ShapeDtypeStruct((IDX_HEADS, n), F32)],
        compiler_params=_cparams("parallel"),
        name="proj1",
    )(x, g, w, wvt, wwit)


def _post_kernel(h_ref, ya_ref, yb_ref, wo_ref, g_ref, wup_ref, wdn_ref, gf_ref, out_ref,
                 h1_s, xn_s, acc_s, *, final_norm):
    j = pl.program_id(1)
    half = ya_ref.shape[1]

    @pl.when(j == 0)
    def _():
        h1 = h_ref[...] + _dot(ya_ref[...], wo_ref[:half, :]) + _dot(yb_ref[...], wo_ref[half:, :])
        h1_s[...] = h1
        xn_s[...] = _rms(h1, g_ref[...]).astype(BF16)
        acc_s[...] = jnp.zeros_like(acc_s)

    r = jnp.maximum(_dot(xn_s[...], wup_ref[...]), 0.0)
    acc_s[...] += _dot((r * r).astype(BF16), wdn_ref[...])

    @pl.when(j == pl.num_programs(1) - 1)
    def _():
        o = h1_s[...] + acc_s[...]
        if final_norm:
            o = _rms(o, gf_ref[...])
        out_ref[...] = o


def _post(h, ya, yb_spec_arg, wo, g, wup, wdn, gf, *, final_norm):
    n, d = h.shape
    tm = min(ROW_TILE, n)
    yb, yb_col = yb_spec_arg
    half = d // 2
    dff = wup.shape[1]
    return pl.pallas_call(
        functools.partial(_post_kernel, final_norm=final_norm),
        grid=(n // tm, dff // FF_CHUNK),
        in_specs=[pl.BlockSpec((tm, d), lambda i, j: (i, 0)),
                  pl.BlockSpec((tm, half), lambda i, j: (i, 0)),
                  pl.BlockSpec((tm, half), lambda i, j: (i, yb_col)),
                  pl.BlockSpec((d, d), lambda i, j: (0, 0)),
                  pl.BlockSpec((1, d), lambda i, j: (0, 0)),
                  pl.BlockSpec((d, FF_CHUNK), lambda i, j: (0, j)),
                  pl.BlockSpec((FF_CHUNK, d), lambda i, j: (j, 0)),
                  pl.BlockSpec((1, d), lambda i, j: (0, 0))],
        out_specs=pl.BlockSpec((tm, d), lambda i, j: (i, 0)),
        out_shape=jax.ShapeDtypeStruct((n, d), F32),
        scratch_shapes=[pltpu.VMEM((tm, d), F32), pltpu.VMEM((tm, d), BF16), pltpu.VMEM((tm, d), F32)],
        compiler_params=_cparams("parallel", "arbitrary"),
        name="post_final" if final_norm else "post",
    )(h, ya, yb, wo, g, wup, wdn, gf)


S5_SPLIT = 2


def _s5_input(u, wb_ref):
    width, nstate = wb_ref.shape[0], wb_ref.shape[1] // 2
    wi, ns = width // S5_SPLIT, nstate // S5_SPLIT
    u_bf = u.astype(BF16)
    part = lambda base: jnp.concatenate(
        [_dot(u_bf[:, r * wi:(r + 1) * wi], wb_ref[r * wi:(r + 1) * wi, base + r * ns:base + (r + 1) * ns])
         for r in range(S5_SPLIT)], axis=1)
    return part(0), part(nstate)


def _s5_output(hre, him, u, wc_ref, d_ref, wg_ref, bg_ref):
    nstate, width = hre.shape[1], wc_ref.shape[1]
    wi, ns = width // S5_SPLIT, nstate // S5_SPLIT
    hre_bf, him_bf = hre.astype(BF16), him.astype(BF16)
    y = jnp.concatenate(
        [_dot(hre_bf[:, r * ns:(r + 1) * ns], wc_ref[r * ns:(r + 1) * ns, r * wi:(r + 1) * wi])
         + _dot(him_bf[:, r * ns:(r + 1) * ns], wc_ref[nstate + r * ns:nstate + (r + 1) * ns, r * wi:(r + 1) * wi])
         for r in range(S5_SPLIT)], axis=1)
    y = jax.nn.gelu(y + d_ref[...] * u)
    gate = jax.nn.sigmoid(_dot(y.astype(BF16), wg_ref[...]) + bg_ref[...])
    return (y * gate).astype(BF16)


def _s5_seq_kernel(u_ref, h0_ref, lam_ref, wb_ref, wc_ref, d_ref, wg_ref, bg_ref, y_ref, ht_ref,
                   hre_s, him_s, carry_s):
    half = hre_s.shape[1]

    @pl.when(pl.program_id(1) == 0)
    def _():
        carry_s[...] = h0_ref[...]

    u = u_ref[...]
    hre_s[...], him_s[...] = _s5_input(u, wb_ref)
    a_re = lam_ref[0:1, :]
    a_im = lam_ref[1:2, :]

    def step(t, carry):
        h_re, h_im = carry
        n_re = a_re * h_re - a_im * h_im + hre_s[pl.ds(t, 1), :]
        n_im = a_re * h_im + a_im * h_re + him_s[pl.ds(t, 1), :]
        hre_s[pl.ds(t, 1), :] = n_re
        him_s[pl.ds(t, 1), :] = n_im
        return n_re, n_im

    h_re, h_im = lax.fori_loop(0, hre_s.shape[0], step, (carry_s[:, :half], carry_s[:, half:]), unroll=8)
    carry_s[:, :half] = h_re
    carry_s[:, half:] = h_im
    ht_ref[...] = carry_s[...]
    y_ref[...] = _s5_output(hre_s[...], him_s[...], u, wc_ref, d_ref, wg_ref, bg_ref)


def _s5_step_kernel(u_ref, h0_ref, lam_ref, wb_ref, wc_ref, d_ref, wg_ref, bg_ref, y_ref, ht_ref):
    half = lam_ref.shape[1]
    u = u_ref[...]
    bu_re, bu_im = _s5_input(u, wb_ref)
    a_re = lam_ref[0:1, :]
    a_im = lam_ref[1:2, :]
    h_re = h0_ref[:, :half]
    h_im = h0_ref[:, half:]
    n_re = a_re * h_re - a_im * h_im + bu_re
    n_im = a_re * h_im + a_im * h_re + bu_im
    ht_ref[:, :half] = n_re
    ht_ref[:, half:] = n_im
    y_ref[...] = _s5_output(n_re, n_im, u, wc_ref, d_ref, wg_ref, bg_ref)


def _s5_param_specs(width, nstate, imap):
    return [pl.BlockSpec((2, nstate), imap),
            pl.BlockSpec((width, 2 * nstate), imap),
            pl.BlockSpec((2 * nstate, width), imap),
            pl.BlockSpec((1, width), imap),
            pl.BlockSpec((width, width), imap),
            pl.BlockSpec((1, width), imap)]


def _s5_seq(z0, h0, params, width):
    bsz, t, _ = z0.shape
    nstate = params[0].shape[1]
    ts = min(S5_CHUNK, t)
    const = lambda b, c: (0, 0)
    return pl.pallas_call(
        _s5_seq_kernel,
        grid=(bsz, t // ts),
        in_specs=[pl.BlockSpec((None, ts, width), lambda b, c: (b, c, 0)),
                  pl.BlockSpec((None, 1, 2 * nstate), lambda b, c: (b, 0, 0))]
                 + _s5_param_specs(width, nstate, const),
        out_specs=[pl.BlockSpec((None, ts, width), lambda b, c: (b, c, 0)),
                   pl.BlockSpec((None, 1, 2 * nstate), lambda b, c: (b, 0, 0))],
        out_shape=[jax.ShapeDtypeStruct((bsz, t, width), BF16),
                   jax.ShapeDtypeStruct((bsz, 1, 2 * nstate), F32)],
        scratch_shapes=[pltpu.VMEM((ts, nstate), F32), pltpu.VMEM((ts, nstate), F32),
                        pltpu.VMEM((1, 2 * nstate), F32)],
        compiler_params=_cparams("parallel", "arbitrary"),
        name="s5_scan",
    )(z0, h0, *params)


def _s5_step(z0, h0, params, width):
    rows = z0.shape[0]
    nstate = params[0].shape[1]
    const = lambda i: (0, 0)
    return pl.pallas_call(
        _s5_step_kernel,
        grid=(1,),
        in_specs=[pl.BlockSpec((rows, width), const), pl.BlockSpec((rows, 2 * nstate), const)]
                 + _s5_param_specs(width, nstate, const),
        out_specs=[pl.BlockSpec((rows, width), const), pl.BlockSpec((rows, 2 * nstate), const)],
        out_shape=[jax.ShapeDtypeStruct((rows, width), BF16),
                   jax.ShapeDtypeStruct((rows, 2 * nstate), F32)],
        compiler_params=_cparams("arbitrary"),
        name="s5_step",
    )(z0, h0, *params)


def _mlstm_chunk_kernel(q_ref, k_ref, v_ref, o_ref, gcol_ref, grow_ref, bcol_ref, brow_ref,
                        c0_ref, n0_ref, m0_ref, y_ref, c_ref, n_ref, m_ref):
    nh = c_ref.shape[0]
    ch = q_ref.shape[0]

    @pl.when(pl.program_id(1) == 0)
    def _():
        c_ref[...] = c0_ref[...]
        n_ref[...] = n0_ref[...]
        m_ref[...] = m0_ref[...]

    gcol = gcol_ref[...] + brow_ref[...]
    grow = grow_ref[...] + bcol_ref[...]
    t_idx = lax.broadcasted_iota(I32, (ch, ch), 0)
    s_idx = lax.broadcasted_iota(I32, (ch, ch), 1)
    causal = t_idx >= s_idx
    for h in range(nh):
        sl = slice(h * HEAD_DIM, (h + 1) * HEAD_DIM)
        q = q_ref[:, sl]
        k = k_ref[:, sl] * (HEAD_DIM ** -0.5)
        v = v_ref[:, sl]
        q_bf, k_bf, v_bf = q.astype(BF16), k.astype(BF16), v.astype(BF16)
        i_col = gcol[:, h:h + 1]
        i_row = grow[h:h + 1, :]
        lf_col = jax.nn.log_sigmoid(gcol[:, nh + h:nh + h + 1])
        lf_row = jax.nn.log_sigmoid(grow[nh + h:nh + h + 1, :])
        b_col = jnp.sum(jnp.where(causal, lf_row, 0.0), axis=1, keepdims=True)
        b_row = jnp.sum(jnp.where(causal, 0.0, lf_col) , axis=0, keepdims=True)
        b_row = b_row + lf_row
        m_prev = m_ref[h][:, 0:1]
        dmat = jnp.where(causal, b_col - b_row + i_row, -jnp.inf)
        a_col = b_col + m_prev
        mj = jnp.maximum(a_col, jnp.max(dmat, axis=1, keepdims=True))
        w_intra = jnp.exp(dmat - mj)
        w_inter = jnp.exp(a_col - mj)
        s = _dot_nt(q_bf, k_bf) * w_intra
        c_prev = c_ref[h]
        n_prev = n_ref[h]
        num = _dot(s.astype(BF16), v_bf) + w_inter * _dot(q_bf, c_prev.astype(BF16))
        den = jnp.sum(s, axis=1, keepdims=True) + w_inter * jnp.sum(q * n_prev, axis=1, keepdims=True)
        hout = num / jnp.maximum(jnp.abs(den), jnp.exp(-mj))
        y_ref[:, sl] = (jax.nn.sigmoid(o_ref[:, sl]) * hout).astype(BF16)
        m_new = mj[ch - 1:ch, :]
        b_last = b_col[ch - 1:ch, :]
        w_end = jnp.exp(b_last - b_col + i_col - m_new)
        decay = jnp.exp(b_last + m_prev - m_new)
        kw = k * w_end
        c_ref[h] = decay * c_prev + _dot_tn(kw.astype(BF16), v_bf)
        n_ref[h] = decay * n_prev + jnp.sum(kw, axis=0, keepdims=True)
        m_ref[h] = jnp.broadcast_to(m_new, (1, LANES))


def _mlstm_chunked(z0, gcol, grow, bias_col, bias_row, c0, n0, m0, nh):
    bsz, t, _ = z0.shape
    width = nh * HEAD_DIM
    ch = MLSTM_CHUNK
    zspec = lambda blk: pl.BlockSpec((None, ch, width), lambda b, c: (b, c, blk))
    state = lambda shape: pl.BlockSpec((None,) + shape, lambda b, c: (b,) + (0,) * len(shape))
    return pl.pallas_call(
        _mlstm_chunk_kernel,
        grid=(bsz, t // ch),
        in_specs=[zspec(1), zspec(2), zspec(3), zspec(4),
                  pl.BlockSpec((None, ch, 2 * nh), lambda b, c: (b, c, 0)),
                  pl.BlockSpec((None, 2 * nh, ch), lambda b, c: (b, 0, c)),
                  pl.BlockSpec((2 * nh, 1), lambda b, c: (0, 0)),
                  pl.BlockSpec((1, 2 * nh), lambda b, c: (0, 0)),
                  state((nh, HEAD_DIM, HEAD_DIM)), state((nh, 1, HEAD_DIM)), state((nh, 1, LANES))],
        out_specs=[pl.BlockSpec((None, ch, width), lambda b, c: (b, c, 0)),
                   state((nh, HEAD_DIM, HEAD_DIM)), state((nh, 1, HEAD_DIM)), state((nh, 1, LANES))],
        out_shape=[jax.ShapeDtypeStruct((bsz, t, width), BF16),
                   jax.ShapeDtypeStruct((bsz, nh, HEAD_DIM, HEAD_DIM), F32),
                   jax.ShapeDtypeStruct((bsz, nh, 1, HEAD_DIM), F32),
                   jax.ShapeDtypeStruct((bsz, nh, 1, LANES), F32)],
        compiler_params=_cparams("parallel", "arbitrary"),
        name="mlstm_chunk",
    )(z0, z0, z0, z0, gcol, grow, bias_col, bias_row, c0, n0, m0)


def _to_column(row):
    n = row.shape[1]
    eye = lax.broadcasted_iota(I32, (n, n), 0) == lax.broadcasted_iota(I32, (n, n), 1)
    return jnp.sum(jnp.where(eye, row, 0.0), axis=1, keepdims=True)


def _mlstm_step_kernel(q_ref, k_ref, v_ref, o_ref, g_ref, brow_ref, c0_ref, n0_ref, m0_ref,
                       y_ref, c_ref, n_ref, m_ref):
    nh = c_ref.shape[0]
    g = g_ref[...] + brow_ref[...]
    for h in range(nh):
        sl = slice(h * HEAD_DIM, (h + 1) * HEAD_DIM)
        q = q_ref[:, sl]
        k = k_ref[:, sl] * (HEAD_DIM ** -0.5)
        v = v_ref[:, sl]
        i_pre = g[:, h:h + 1]
        lf = jax.nn.log_sigmoid(g[:, nh + h:nh + h + 1])
        m_prev = m0_ref[h][:, 0:1]
        c_prev = c0_ref[h]
        n_prev = n0_ref[h]
        a = lf + m_prev
        mj = jnp.maximum(a, i_pre)
        w_intra = jnp.exp(i_pre - mj)
        w_inter = jnp.exp(a - mj)
        s = jnp.sum(q * k, axis=1, keepdims=True) * w_intra
        q_col = _to_column(q)
        k_col = _to_column(k)
        num = s * v + w_inter * jnp.sum(q_col * c_prev, axis=0, keepdims=True)
        den = s + w_inter * jnp.sum(q * n_prev, axis=1, keepdims=True)
        hout = num / jnp.maximum(jnp.abs(den), jnp.exp(-mj))
        y_ref[:, sl] = (jax.nn.sigmoid(o_ref[:, sl]) * hout).astype(BF16)
        w_end = jnp.exp(i_pre - mj)
        decay = jnp.exp(a - mj)
        c_ref[h] = decay * c_prev + (w_end * k_col) * v
        n_ref[h] = decay * n_prev + w_end * k
        m_ref[h] = jnp.broadcast_to(mj, (1, LANES))


def _mlstm_step(z0, g, bias_row, c0, n0, m0, nh):
    bsz = z0.shape[0]
    width = nh * HEAD_DIM
    zspec = lambda blk: pl.BlockSpec((None, 1, width), lambda b: (b, 0, blk))
    state = lambda shape: pl.BlockSpec((None,) + shape, lambda b: (b,) + (0,) * len(shape))
    return pl.pallas_call(
        _mlstm_step_kernel,
        grid=(bsz,),
        in_specs=[zspec(1), zspec(2), zspec(3), zspec(4),
                  pl.BlockSpec((None, 1, 2 * nh), lambda b: (b, 0, 0)),
                  pl.BlockSpec((1, 2 * nh), lambda b: (0, 0)),
                  state((nh, HEAD_DIM, HEAD_DIM)), state((nh, 1, HEAD_DIM)), state((nh, 1, LANES))],
        out_specs=[pl.BlockSpec((None, 1, width), lambda b: (b, 0, 0)),
                   state((nh, HEAD_DIM, HEAD_DIM)), state((nh, 1, HEAD_DIM)), state((nh, 1, LANES))],
        out_shape=[jax.ShapeDtypeStruct((bsz, 1, width), BF16),
                   jax.ShapeDtypeStruct((bsz, nh, HEAD_DIM, HEAD_DIM), F32),
                   jax.ShapeDtypeStruct((bsz, nh, 1, HEAD_DIM), F32),
                   jax.ShapeDtypeStruct((bsz, nh, 1, LANES), F32)],
        compiler_params=_cparams("parallel"),
        name="mlstm_step",
    )(z0, z0, z0, z0, g, bias_row, c0, n0, m0)


def _key_to_float(key):
    bits = key ^ ((key >> 31) & jnp.int32(0x7FFFFFFF))
    return lax.bitcast_convert_type(bits, F32)


def _kth_by_bit_search(count, shape):
    def bit_step(it, key):
        cand = key + (jnp.int32(1) << (31 - it))
        cand_f = _key_to_float(cand)
        cnt = count(lambda tile, col: tile >= cand_f)
        return jnp.where(cnt >= TOPK, cand, key)

    key = lax.fori_loop(0, 32, bit_step, jnp.full(shape, INT_MIN, I32))
    has_thr = key > INT_MIN
    return jnp.where(has_thr, _key_to_float(jnp.where(has_thr, key, 0)), -jnp.inf)


def _select_threshold(count, shape, width, j0_s, thr):
    has_thr = thr > -jnp.inf
    need = TOPK - count(lambda tile, col: tile > thr)
    n_eq = count(lambda tile, col: tile == thr)
    j0_s[...] = jnp.where(has_thr, jnp.int32(width), jnp.int32(-1))
    surplus = jnp.max(jnp.where(has_thr & (n_eq > need), 1, 0))
    nbits = max(1, (width - 1).bit_length())

    @pl.when(surplus > 0)
    def _():
        def idx_step(it, j0):
            cand = j0 | (jnp.int32(1) << (nbits - 1 - it))
            cnt = count(lambda tile, col: (tile == thr) & (col < cand))
            return jnp.where(cnt < need, cand, j0)

        j0 = lax.fori_loop(0, nbits, idx_step, jnp.zeros(shape, I32))
        j0_s[...] = jnp.where(has_thr, j0, jnp.int32(-1))

    return thr, j0_s[...]


def _network_pass(load, store, stages, groups):
    for grp in groups:
        vals = [load(i) for i in grp]
        pos = {gi: n for n, gi in enumerate(grp)}
        for size, dist in stages:
            for gi in grp:
                gl = gi ^ dist
                if gl > gi:
                    a, b = pos[gi], pos[gl]
                    hi, lo = jnp.maximum(vals[a], vals[b]), jnp.minimum(vals[a], vals[b])
                    vals[a], vals[b] = (hi, lo) if (gi & size) == 0 else (lo, hi)
        for n, gi in enumerate(grp):
            store(gi, vals[n])


_NET_GROUP = 16
_LOW_GROUPS = [[m * _NET_GROUP + t for t in range(_NET_GROUP)] for m in range(SORT_TILES // _NET_GROUP)]
_HIGH_GROUPS = [[m + (SORT_TILES // _NET_GROUP) * t for t in range(_NET_GROUP)]
                for m in range(SORT_TILES // _NET_GROUP)]


def _tile(ref, base, i):
    return ref.at[pl.ds(base + 8 * i, 8), :]


def _sort_block_desc(src, src_base, dst):
    assert SORT_TILES == 256 and _NET_GROUP == 16
    low = lambda size: [(size, d) for d in (8, 4, 2, 1) if d < size]
    first = [st for size in (2, 4, 8, 16) for st in low(size)]
    _network_pass(lambda i: _tile(src, src_base, i)[...],
                  lambda i, v: _tile(dst, 0, i).__setitem__(Ellipsis, v), first, _LOW_GROUPS)
    ld = lambda i: _tile(dst, 0, i)[...]
    st = lambda i, v: _tile(dst, 0, i).__setitem__(Ellipsis, v)
    for size in (32, 64, 128, 256):
        _network_pass(ld, st, [(size, d) for d in (128, 64, 32, 16) if d < size], _HIGH_GROUPS)
        _network_pass(ld, st, low(size), _LOW_GROUPS)


def _merge_top(run, other, tmp, shift=None):
    def ld(i):
        o = _tile(other, 0, SORT_TILES - 1 - i)[...]
        if shift is not None:
            o = pltpu.roll(o, shift, axis=0)
        return jnp.maximum(_tile(run, 0, i)[...], o)

    _network_pass(ld, lambda i, v: _tile(tmp, 0, i).__setitem__(Ellipsis, v),
                  [(SORT_TILES, d) for d in (128, 64, 32, 16)], _HIGH_GROUPS)
    _network_pass(lambda i: _tile(tmp, 0, i)[...], lambda i, v: _tile(run, 0, i).__setitem__(Ellipsis, v),
                  [(SORT_TILES, d) for d in (8, 4, 2, 1)], _LOW_GROUPS)


def _mask_bias(scores, cols, thr, j0):
    sel = (scores > thr) | ((scores == thr) & (cols <= j0))
    return jnp.where(sel, 0.0, NEG_BIG)


def _dsa_prompt_kernel(q_ref, qi_ref, wit_ref, ki_ref, k_ref, vt_ref, o_ref, sc_s, j0_s, m_s, acc_s,
                       run_s, blk_s, tmp_s):
    qb = q_ref.shape[0]
    kc_len = KEY_CHUNK
    nheads = q_ref.shape[1] // HEAD_DIM
    nkv = k_ref.shape[1] // HEAD_DIM
    rep = nheads // nkv
    i = pl.program_id(1)
    nchunks = ((i + 1) * qb + kc_len - 1) // kc_len
    qpos = i * qb + lax.broadcasted_iota(I32, (1, qb), 1)
    sub = lax.broadcasted_iota(I32, (kc_len, qb), 0)

    qi = qi_ref[...]
    wit = wit_ref[...]
    qi_all = jnp.concatenate([qi[:, h * IDX_DIM:(h + 1) * IDX_DIM] for h in range(IDX_HEADS)], axis=0)

    def score_chunk(c, _):
        off = pl.multiple_of(c * kc_len, kc_len)
        ki = ki_ref[pl.ds(off, kc_len), :]
        logits = _dot_nt(ki, qi_all)
        sc = jnp.zeros((kc_len, qb), F32)
        for h in range(IDX_HEADS):
            sc = sc + jnp.maximum(logits[:, h * qb:(h + 1) * qb], 0.0) * wit[h:h + 1, :]
        sc_s[pl.ds(off, kc_len), :] = jnp.where(sub + off <= qpos, sc, -jnp.inf)
        return 0

    lax.fori_loop(0, nchunks, score_chunk, 0)

    sort_rows = 8 * SORT_TILES
    nblocks = ((i + 1) * qb + sort_rows - 1) // sort_rows

    @pl.when(nchunks * kc_len < nblocks * sort_rows)
    def _():
        sc_s[pl.ds(pl.multiple_of(nchunks * kc_len, kc_len), kc_len), :] = jnp.full((kc_len, qb), -jnp.inf, F32)

    run_s[...] = jnp.full_like(run_s, -jnp.inf)

    def sort_block(b, _):
        _sort_block_desc(sc_s, pl.multiple_of(b * sort_rows, sort_rows), blk_s)
        _merge_top(run_s, blk_s, tmp_s)
        return 0

    lax.fori_loop(0, nblocks, sort_block, 0)
    for shift in (4, 2, 1):
        _merge_top(run_s, run_s, tmp_s, shift)
    kth = run_s[8 * (SORT_TILES - 1):8 * (SORT_TILES - 1) + 1, :]

    def count(pred):
        def body(c, acc):
            off = pl.multiple_of(c * COUNT_CHUNK, COUNT_CHUNK)
            ind = jnp.where(pred(sc_s[pl.ds(off, COUNT_CHUNK), :], sub[:COUNT_CHUNK] + off), 1, 0)
            return acc + jnp.sum(ind.reshape(COUNT_CHUNK // 8, 8, qb), axis=0)

        ncount = ((i + 1) * qb + COUNT_CHUNK - 1) // COUNT_CHUNK
        acc = lax.fori_loop(0, ncount, body, jnp.zeros((8, qb), I32))
        return jnp.sum(acc, axis=0, keepdims=True)

    thr, j0 = _select_threshold(count, (1, qb), sc_s.shape[0], j0_s, kth)

    m_s[...] = jnp.full_like(m_s, NEG_BIG)
    acc_s[...] = jnp.zeros_like(acc_s)
    q = q_ref[...]
    q_g = [jnp.concatenate([q[:, (g * rep + r) * HEAD_DIM:(g * rep + r + 1) * HEAD_DIM]
                            for r in range(rep)], axis=0) for g in range(nkv)]
    ones_rows = jnp.ones((acc_s.shape[1] - HEAD_DIM, kc_len), BF16)

    def attend_chunk(c, _):
        off = pl.multiple_of(c * kc_len, kc_len)
        bias = _mask_bias(sc_s[pl.ds(off, kc_len), :], sub + off, thr, j0)
        bias = jnp.concatenate([bias] * rep, axis=1)
        for g in range(nkv):
            kc = k_ref[pl.ds(off, kc_len), g * HEAD_DIM:(g + 1) * HEAD_DIM]
            att = _dot_nt(kc, q_g[g]) + bias
            m_old = m_s[g]
            m_new = jnp.maximum(m_old, jnp.max(att, axis=0, keepdims=True))
            p = jnp.exp2(att - m_new).astype(BF16)
            vt = jnp.concatenate([vt_ref[g * HEAD_DIM:(g + 1) * HEAD_DIM, pl.ds(off, kc_len)], ones_rows],
                                 axis=0)
            acc_s[g] = jnp.exp2(m_old - m_new) * acc_s[g] + _dot(vt, p)
            m_s[g] = m_new
        return 0

    lax.fori_loop(0, nchunks, attend_chunk, 0)

    for g in range(nkv):
        acc = acc_s[g]
        out = acc[:HEAD_DIM, :] / acc[HEAD_DIM:HEAD_DIM + 1, :]
        for r in range(rep):
            hd = g * rep + r
            o_ref[:, hd * HEAD_DIM:(hd + 1) * HEAD_DIM] = out[:, r * qb:(r + 1) * qb].T.astype(BF16)


def _dsa_prompt(q, qi, wit, kidx, kbf, vt):
    bsz, t, width = q.shape
    kvw = kbf.shape[2]
    nkv = kvw // HEAD_DIM
    rep = width // HEAD_DIM // nkv
    qb = Q_BLOCK
    nq = t // qb
    assert t % (8 * SORT_TILES) == 0 and 8 * SORT_TILES == 2 * KEY_CHUNK and TOPK == SORT_TILES
    ones_rows = 16
    return pl.pallas_call(
        _dsa_prompt_kernel,
        grid=(bsz, nq),
        in_specs=[pl.BlockSpec((None, qb, width), lambda b, i: (b, i, 0)),
                  pl.BlockSpec((None, qb, qi.shape[2]), lambda b, i: (b, i, 0)),
                  pl.BlockSpec((IDX_HEADS, qb), lambda b, i: (0, b * nq + i)),
                  pl.BlockSpec((None, t, IDX_DIM), lambda b, i: (b, 0, 0)),
                  pl.BlockSpec((None, t, kvw), lambda b, i: (b, 0, 0)),
                  pl.BlockSpec((kvw, t), lambda b, i: (0, b))],
        out_specs=pl.BlockSpec((None, qb, width), lambda b, i: (b, i, 0)),
        out_shape=jax.ShapeDtypeStruct((bsz, t, width), BF16),
        scratch_shapes=[pltpu.VMEM((t, qb), F32), pltpu.VMEM((1, qb), I32),
                        pltpu.VMEM((nkv, 1, rep * qb), F32),
                        pltpu.VMEM((nkv, HEAD_DIM + ones_rows, rep * qb), F32),
                        pltpu.VMEM((8 * SORT_TILES, qb), F32), pltpu.VMEM((8 * SORT_TILES, qb), F32),
                        pltpu.VMEM((8 * SORT_TILES, qb), F32)],
        compiler_params=_cparams("parallel", "arbitrary"),
        name="dsa_prompt",
    )(q, qi, wit, kidx, kbf, vt)


def _dsa_score_kernel(pt_ref, qi2_ref, qi_ref, wi_ref, knew_ref, *rest):
    pages, (keys_ref, newkey_ref) = rest[:PAGES_PER_STEP], rest[PAGES_PER_STEP:]
    qi = qi_ref[...]
    wi = wi_ref[...]
    nh = qi.shape[0]
    ki = jnp.concatenate([page_ref[...].astype(BF16) for page_ref in pages], axis=0)
    logits = jnp.maximum(_dot_nt(qi2_ref[...].astype(BF16), ki), 0.0)
    half = logits.shape[1]
    keys_ref[:, :half] = jnp.sum(logits[:nh] * wi, axis=0, keepdims=True)
    keys_ref[:, half:] = jnp.sum(logits[nh:] * wi, axis=0, keepdims=True)
    logit_new = jnp.sum(qi * knew_ref[...], axis=1, keepdims=True)
    sc_new = jnp.sum(jnp.maximum(logit_new, 0.0) * wi, axis=0, keepdims=True)
    newkey_ref[...] = jnp.broadcast_to(sc_new, (1, LANES))


def _dsa_score(page_table, qi, wi, kidx_new, cache_kidx):
    bsz, npages = page_table.shape
    steps = npages // PAGES_PER_STEP
    prow = PAGE * IDX_DIM // LANES
    page_spec = lambda p: pl.BlockSpec((prow, LANES), lambda b, j, pt: (pt[b, j * PAGES_PER_STEP + p], 0))
    per_seq = lambda shape: pl.BlockSpec((None,) + shape, lambda b, j, pt: (b, 0, 0))
    zero = jnp.zeros_like(qi)
    qi2 = jnp.concatenate([jnp.concatenate([qi, zero], axis=2), jnp.concatenate([zero, qi], axis=2)], axis=1)
    scores, new = pl.pallas_call(
        _dsa_score_kernel,
        grid_spec=pltpu.PrefetchScalarGridSpec(
            num_scalar_prefetch=1,
            grid=(bsz, steps),
            in_specs=[per_seq((2 * IDX_HEADS, LANES)), per_seq((IDX_HEADS, IDX_DIM)), per_seq((IDX_HEADS, 1)),
                      per_seq((1, IDX_DIM))]
                     + [page_spec(p) for p in range(PAGES_PER_STEP)],
            out_specs=[pl.BlockSpec((None, 1, PAGES_PER_STEP * PAGE), lambda b, j, pt: (b, 0, j)),
                       per_seq((1, LANES))]),
        out_shape=[jax.ShapeDtypeStruct((bsz, 1, npages * PAGE), F32),
                   jax.ShapeDtypeStruct((bsz, 1, LANES), F32)],
        compiler_params=_cparams("parallel", "arbitrary"),
        name="dsa_decode_score",
    )(page_table, qi2, qi, wi, kidx_new, *([cache_kidx] * PAGES_PER_STEP))
    scores = scores.reshape(bsz, steps, 2, PAGES_PER_STEP, prow)
    return jnp.moveaxis(scores, 2, 4).reshape(bsz, 1, npages * PAGE), new


def _dsa_select_kernel(keys_ref, thr_ref, j0_ref, j0_s):
    rows, width = keys_ref.shape
    lane = lax.broadcasted_iota(I32, (rows, LANES), 1)

    def count(pred):
        def body(t, acc):
            off = pl.multiple_of(t * LANES, LANES)
            return acc + jnp.where(pred(keys_ref[:, pl.ds(off, LANES)], lane + off), 1, 0)

        acc = lax.fori_loop(0, width // LANES, body, jnp.zeros((rows, LANES), I32))
        return jnp.sum(acc, axis=1, keepdims=True)

    thr, j0 = _select_threshold(count, (rows, 1), width, j0_s, _kth_by_bit_search(count, (rows, 1)))
    thr_ref[...] = jnp.broadcast_to(thr, thr_ref.shape)
    j0_ref[...] = jnp.broadcast_to(j0, j0_ref.shape)


def _dsa_select(keys):
    rows, width = keys.shape
    const = lambda i: (0, 0)
    return pl.pallas_call(
        _dsa_select_kernel,
        grid=(1,),
        in_specs=[pl.BlockSpec((rows, width), const)],
        out_specs=[pl.BlockSpec((rows, LANES), const), pl.BlockSpec((rows, LANES), const)],
        out_shape=[jax.ShapeDtypeStruct((rows, LANES), F32), jax.ShapeDtypeStruct((rows, LANES), I32)],
        scratch_shapes=[pltpu.VMEM((rows, 1), I32)],
        compiler_params=_cparams("arbitrary"),
        name="dsa_decode_select",
    )(keys)


def _dsa_decode_kernel(pt_ref, q_ref, keys_ref, tail_ref, thr_ref, j0_ref, knew_ref, vnew_ref, *rest,
                       n_past, nkv):
    j = pl.program_id(1)
    nheads = q_ref.shape[0]
    rep = nheads // nkv
    kpages, vpages = rest[:PAGES_PER_STEP], rest[PAGES_PER_STEP:2 * PAGES_PER_STEP]
    o_ref, m_s, l_s, acc_s = rest[2 * PAGES_PER_STEP:]

    @pl.when(j == 0)
    def _():
        m_s[...] = jnp.full_like(m_s, NEG_BIG)
        l_s[...] = jnp.zeros_like(l_s)
        acc_s[...] = jnp.zeros_like(acc_s)

    q = q_ref[...]
    thr = thr_ref[:, 0:1]
    j0 = j0_ref[:, 0:1]
    width = PAGES_PER_STEP * PAGE * nkv
    row = lax.broadcasted_iota(I32, (1, width), 1) + j * width
    head_kv = lax.broadcasted_iota(I32, (nheads, 1), 0) // rep

    def update(att, value_fn):
        m_old = m_s[...]
        m_new = jnp.maximum(m_old, jnp.max(att, axis=1, keepdims=True))
        alpha = jnp.exp2(m_old - m_new)
        p = jnp.exp2(att - m_new)
        l_s[...] = alpha * l_s[...] + jnp.sum(p, axis=1, keepdims=True)
        acc_s[...] = alpha * acc_s[...] + value_fn(p)
        m_s[...] = m_new

    kcat = jnp.concatenate([r[...].astype(BF16) for r in kpages], axis=0)
    vcat = jnp.concatenate([r[...].astype(BF16) for r in vpages], axis=0)
    bias = _mask_bias(keys_ref[...], row // nkv, thr, j0)
    bias = jnp.where(row % nkv == head_kv, bias, NEG_BIG)
    update(_dot_nt(q.astype(BF16), kcat) + bias, lambda pr: _dot(pr.astype(BF16), vcat))

    @pl.when(j == pl.num_programs(1) - 1)
    def _():
        k_new = jnp.zeros_like(q)
        v_new = jnp.zeros_like(q)
        for g in range(nkv):
            k_new = jnp.where(head_kv == g, knew_ref[g:g + 1, :], k_new)
            v_new = jnp.where(head_kv == g, vnew_ref[g:g + 1, :], v_new)
        bias_new = _mask_bias(tail_ref[:, 0:1], jnp.int32(n_past), thr, j0)
        att_new = jnp.sum(q * k_new, axis=1, keepdims=True) + bias_new
        update(att_new, lambda pr: pr * v_new)
        o_ref[...] = (acc_s[...] / l_s[...]).astype(BF16)


def _dsa_decode(page_table, q, keys, tail, thr, j0, k_new, v_new, cache_k, cache_v):
    bsz, npages = page_table.shape
    nheads = q.shape[1]
    nkv = k_new.shape[1]
    steps = npages // PAGES_PER_STEP
    page_spec = lambda p: pl.BlockSpec(
        (PAGE * nkv, HEAD_DIM), lambda b, j, pt: (pt[b, j * PAGES_PER_STEP + p], 0))
    per_seq = lambda shape: pl.BlockSpec((None,) + shape, lambda b, j, pt: (b, 0, 0))
    keys_spec = pl.BlockSpec((None, 1, PAGES_PER_STEP * PAGE * nkv), lambda b, j, pt: (b, 0, j))
    return pl.pallas_call(
        functools.partial(_dsa_decode_kernel, n_past=npages * PAGE, nkv=nkv),
        grid_spec=pltpu.PrefetchScalarGridSpec(
            num_scalar_prefetch=1,
            grid=(bsz, steps),
            in_specs=[per_seq((nheads, HEAD_DIM)), keys_spec, per_seq((1, LANES)), per_seq((1, LANES)),
                      per_seq((1, LANES)), per_seq((nkv, HEAD_DIM)), per_seq((nkv, HEAD_DIM))]
                     + [page_spec(p) for p in range(PAGES_PER_STEP)] * 2,
            out_specs=per_seq((nheads, HEAD_DIM)),
            scratch_shapes=[pltpu.VMEM((nheads, 1), F32), pltpu.VMEM((nheads, 1), F32),
                            pltpu.VMEM((nheads, HEAD_DIM), F32)]),
        out_shape=jax.ShapeDtypeStruct((bsz, nheads, HEAD_DIM), BF16),
        compiler_params=_cparams("parallel", "arbitrary"),
        name="dsa_decode_attend",
    )(page_table, q, keys, tail, thr, j0, k_new, v_new,
      *([cache_k] * PAGES_PER_STEP), *([cache_v] * PAGES_PER_STEP))


def _pad_cols(w, width):
    return jnp.pad(w, ((0, 0), (0, width - w.shape[1])))


def _block_diag(blocks):
    g, r, c = blocks.shape
    eye = jnp.eye(g, dtype=blocks.dtype)
    return (blocks[:, :, None, :] * eye[:, None, :, None]).reshape(g * r, g * c)


def _s5_params(lam_re, lam_im, log_dt, b_re, b_im, c_re, c_im, d_skip, w_glu, b_glu):
    dt = jnp.exp(log_dt)[:, None]
    mag = jnp.exp(lam_re * dt)
    bar_re = mag * jnp.cos(lam_im * dt)
    bar_im = mag * jnp.sin(lam_im * dt)
    inv = 1.0 / (lam_re * lam_re + lam_im * lam_im)
    coef_re = (((bar_re - 1.0) * lam_re + bar_im * lam_im) * inv)[..., None]
    coef_im = ((bar_im * lam_re - (bar_re - 1.0) * lam_im) * inv)[..., None]
    bb_re = coef_re * b_re - coef_im * b_im
    bb_im = coef_re * b_im + coef_im * b_re
    lam_rows = jnp.stack([bar_re.reshape(-1), bar_im.reshape(-1)])
    to_in = lambda z: _block_diag(jnp.swapaxes(z, 1, 2))
    wb = jnp.concatenate([to_in(bb_re), to_in(bb_im)], axis=1).astype(BF16)
    to_out = lambda z: _block_diag(jnp.swapaxes(z, 1, 2))
    wc = jnp.concatenate([to_out(c_re), to_out(-c_im)], axis=0).astype(BF16)
    return (lam_rows, wb, wc, d_skip[None, :], w_glu.astype(BF16), b_glu[None, :])


def kernel(x_prompt, x_sample, state_s5_re, state_s5_im, state_mlstm_c, state_mlstm_n, state_mlstm_m,
           cache_k, cache_v, cache_kidx, page_table, norm_mix, norm_mlp, norm_final, w_in0, s5_lam_re,
           s5_lam_im, s5_log_dt, s5_b_re, s5_b_im, s5_c_re, s5_c_im, s5_d, w_glu, b_glu, b_igate,
           b_fgate, w_out0, w_in1, w_out1, w_up, w_down):
    bp, tp, d = x_prompt.shape
    db, ts, _ = x_sample.shape
    assert ts == 1, "the decode path handles one new token per sequence"
    s5_groups, s5_state = s5_lam_re.shape
    s5_width = s5_groups * S5_GROUP
    nstate = s5_groups * s5_state
    nh = b_igate.shape[0]
    ml_width = nh * HEAD_DIM
    assert s5_width == ml_width == 512 and d == 1024
    n_past = page_table.shape[1] * PAGE
    kvw = cache_k.shape[2] * cache_k.shape[3]
    nheads = w_out1.shape[0] // HEAD_DIM

    gate_cols = s5_width + 4 * ml_width
    w0 = jnp.concatenate([w_in0[:, :gate_cols], _pad_cols(w_in0[:, gate_cols:], LANES)], axis=1).astype(BF16)
    s5p = _s5_params(s5_lam_re, s5_lam_im, s5_log_dt, s5_b_re, s5_b_im, s5_c_re, s5_c_im, s5_d, w_glu, b_glu)
    gate_bias = jnp.concatenate([b_igate, b_fgate])
    bias_row, bias_col = gate_bias[None, :], gate_bias[:, None]
    ki0 = 1024 + 2 * kvw + IDX_HEADS * IDX_DIM
    w1 = jnp.concatenate([w_in1[:, :ki0], _pad_cols(w_in1[:, ki0:ki0 + IDX_DIM], LANES)], axis=1).astype(BF16)
    wvt = w_in1[:, 1024 + kvw:1024 + 2 * kvw].T.astype(BF16)
    wwit = w_in1[:, ki0 + IDX_DIM:].T.astype(BF16)
    wo0, wo1 = w_out0.astype(BF16), w_out1.astype(BF16)
    wup, wdn = w_up.astype(BF16), w_down.astype(BF16)
    g_mix, g_mlp, g_fin = norm_mix[:, None, :], norm_mlp[:, None, :], norm_final[None, :]

    def trunk(x2d, bsz, t, s5_h0, c0, n0, m0, attend):
        n = bsz * t
        z0 = _proj0(x2d, g_mix[0], w0)
        gates = z0[:, gate_cols:gate_cols + 2 * nh]
        m0b = jnp.broadcast_to(m0[:, :, None, None], (bsz, nh, 1, LANES))
        n0r = n0[:, :, None, :]
        if t == 1:
            y_s5, h_t = _s5_step(z0, s5_h0, s5p, s5_width)
            y_ml, c_t, n_t, m_t = _mlstm_step(z0.reshape(bsz, 1, -1), gates.reshape(bsz, 1, 2 * nh),
                                              bias_row, c0, n0r, m0b, nh)
        else:
            y_s5, h_t = _s5_seq(z0.reshape(bsz, t, -1), s5_h0.reshape(bsz, 1, -1), s5p, s5_width)
            g3 = gates.reshape(bsz, t, 2 * nh)
            y_ml, c_t, n_t, m_t = _mlstm_chunked(z0.reshape(bsz, t, -1), g3, jnp.swapaxes(g3, 1, 2),
                                                 bias_col, bias_row, c0, n0r, m0b, nh)
        h_t = h_t.reshape(bsz, 2, s5_groups, s5_state)
        states = (h_t[:, 0], h_t[:, 1], c_t, n_t[:, :, 0, :], m_t[:, :, 0, 0])
        h1 = _post(x2d, y_s5.reshape(n, s5_width), (y_ml.reshape(n, ml_width), 0), wo0, g_mlp[0],
                   wup[0], wdn[0], g_fin, final_norm=False)
        q, k, v, kbf, qi, kidx, kidxbf, vt, wit = _proj1(h1, g_mix[1], w1, wvt, wwit)
        o = attend(q, k, v, kbf, qi, kidx, kidxbf, vt, wit)
        y = _post(h1, o, (o, 1), wo1, g_mlp[1], wup[1], wdn[1], g_fin, final_norm=True)
        rows = (k.reshape(bsz, t, -1, HEAD_DIM), v.reshape(bsz, t, -1, HEAD_DIM), kidx.reshape(bsz, t, IDX_DIM))
        return y.reshape(bsz, t, d), states, rows

    def attend_prompt(q, k, v, kbf, qi, kidx, kidxbf, vt, wit):
        r3 = lambda z: z.reshape(bp, tp, -1)
        return _dsa_prompt(r3(q), r3(qi), wit, r3(kidxbf), r3(kbf), vt).reshape(bp * tp, -1)

    def attend_decode(q, k, v, kbf, qi, kidx, kidxbf, vt, wit):
        qi3 = qi.astype(F32).reshape(db, IDX_HEADS, IDX_DIM)
        wi3 = wit.T.reshape(db, IDX_HEADS, 1)
        keys, newkey = _dsa_score(page_table, qi3, wi3, kidx.reshape(db, 1, IDX_DIM),
                                  cache_kidx.reshape(-1, LANES))
        tail = jnp.where(lax.broadcasted_iota(I32, (db, LANES), 1) == 0, newkey[:, 0, :], -jnp.inf)
        keys = jnp.concatenate([keys[:, 0, :], tail], axis=1)
        thr, j0 = _dsa_select(keys)
        nkv = kvw // HEAD_DIM
        keys_rows = jnp.repeat(keys[:, :n_past], nkv, axis=1)[:, None, :]
        o = _dsa_decode(page_table, q.astype(F32).reshape(db, nheads, HEAD_DIM), keys_rows,
                        keys[:, None, n_past:], thr[:, None, :], j0[:, None, :],
                        k.reshape(db, nkv, HEAD_DIM), v.reshape(db, nkv, HEAD_DIM),
                        cache_k.reshape(-1, HEAD_DIM), cache_v.reshape(-1, HEAD_DIM))
        return o.reshape(db, nheads * HEAD_DIM)

    zeros = lambda *shape: jnp.zeros(shape, F32)
    y_p, st_p, rows_p = trunk(x_prompt.reshape(bp * tp, d), bp, tp, zeros(bp, 2 * nstate),
                              zeros(bp, nh, HEAD_DIM, HEAD_DIM), zeros(bp, nh, HEAD_DIM), zeros(bp, nh),
                              attend_prompt)
    s5_h0 = jnp.concatenate([state_s5_re.reshape(db, nstate), state_s5_im.reshape(db, nstate)], axis=1)
    y_s, st_s, rows_s = trunk(x_sample.reshape(db, d), db, 1, s5_h0, state_mlstm_c, state_mlstm_n,
                              state_mlstm_m, attend_decode)
    return (y_p, y_s) + st_p + rows_p + st_s + rows_s
```

```python
import functools
import math

import jax
import jax.numpy as jnp
from jax import lax
from jax.experimental import pallas as pl
from jax.experimental.pallas import tpu as pltpu

F32, BF16, I32 = jnp.float32, jnp.bfloat16, jnp.int32

EPS = 1e-6
LANES = 128
PAGE = 128
S5_GROUP = 16
S5_STATE = 64
HEAD_DIM = 128
IDX_DIM = 64
IDX_HEADS = 8
TOPK = 256
Q_BLOCK = 128
KEY_CHUNK = 1024
COUNT_CHUNK = 512
SORT_TILES = 256
MLSTM_CHUNK = 128
S5_CHUNK = 256
ROW_TILE = 512
FF_CHUNK = 2048
PAGES_PER_STEP = 16
INT_MIN = -2 ** 31
NEG_BIG = -1e30
VMEM_LIMIT = 56 * 1024 * 1024


def _cparams(*sem):
    return pltpu.CompilerParams(dimension_semantics=sem, vmem_limit_bytes=VMEM_LIMIT)


def _rms(x, g):
    return x * lax.rsqrt(jnp.mean(x * x, axis=-1, keepdims=True) + EPS) * g


def _dot(a, b):
    return jnp.dot(a, b, preferred_element_type=F32)


def _dot_nt(a, b):
    return lax.dot_general(a, b, (((1,), (1,)), ((), ())), preferred_element_type=F32)


def _dot_tn(a, b):
    return lax.dot_general(a, b, (((0,), (0,)), ((), ())), preferred_element_type=F32)


def _proj0_kernel(x_ref, g_ref, w_ref, o_ref):
    xn = _rms(x_ref[...], g_ref[...]).astype(BF16)
    o_ref[...] = _dot(xn, w_ref[...])


def _proj0(x, g, w):
    n, d = x.shape
    tm = min(ROW_TILE, n)
    wtot = w.shape[1]
    return pl.pallas_call(
        _proj0_kernel,
        grid=(n // tm,),
        in_specs=[pl.BlockSpec((tm, d), lambda i: (i, 0)),
                  pl.BlockSpec((1, d), lambda i: (0, 0)),
                  pl.BlockSpec((d, wtot), lambda i: (0, 0))],
        out_specs=pl.BlockSpec((tm, wtot), lambda i: (i, 0)),
        out_shape=jax.ShapeDtypeStruct((n, wtot), F32),
        compiler_params=_cparams("parallel"),
        name="proj0",
    )(x, g, w)


_Q1, _K1, _V1, _QI1, _KI1, _END1 = 0, 1024, 1280, 1536, 2048, 2176
LOG2E = 1.4426950408889634


def _proj1_kernel(x_ref, g_ref, w_ref, wvt_ref, wwit_ref, q_ref, k_ref, v_ref, kbf_ref, qi_ref,
                  kidx_ref, kidxbf_ref, vt_ref, wit_ref):
    xn = _rms(x_ref[...], g_ref[...]).astype(BF16)
    z = _dot(xn, w_ref[...])
    q_ref[...] = (z[:, _Q1:_K1] * (HEAD_DIM ** -0.5 * LOG2E)).astype(BF16)
    k = z[:, _K1:_V1]
    k_ref[...] = k
    v_ref[...] = z[:, _V1:_QI1]
    kbf_ref[...] = k.astype(BF16)
    qi_ref[...] = z[:, _QI1:_KI1].astype(BF16)
    kidx = z[:, _KI1:_KI1 + IDX_DIM]
    kidx_ref[...] = kidx
    kidxbf_ref[...] = kidx.astype(BF16)
    vt_ref[...] = _dot_nt(wvt_ref[...], xn).astype(BF16)
    wit_ref[...] = _dot_nt(wwit_ref[...], xn) * ((IDX_DIM ** -0.5) * (IDX_HEADS ** -0.5))


def _proj1(x, g, w, wvt, wwit):
    n, d = x.shape
    tm = min(ROW_TILE, n)
    kvw = wvt.shape[0]
    row = lambda width: pl.BlockSpec((tm, width), lambda i: (i, 0))
    col = lambda height: pl.BlockSpec((height, tm), lambda i: (0, i))
    full = lambda a: pl.BlockSpec(a.shape, lambda i: (0, 0))
    shp = lambda width, dt: jax.ShapeDtypeStruct((n, width), dt)
    return pl.pallas_call(
        _proj1_kernel,
        grid=(n // tm,),
        in_specs=[row(d), pl.BlockSpec((1, d), lambda i: (0, 0)), full(w), full(wvt), full(wwit)],
        out_specs=[row(1024), row(kvw), row(kvw), row(kvw), row(512), row(IDX_DIM), row(IDX_DIM),
                   col(kvw), col(IDX_HEADS)],
        out_shape=[shp(1024, BF16), shp(kvw, F32), shp(kvw, F32), shp(kvw, BF16), shp(512, BF16),
                   shp(IDX_DIM, F32), shp(IDX_DIM, BF16),
                   jax.ShapeDtypeStruct((kvw, n), BF16), jax.ShapeDtypeStruct((IDX_HEADS, n), F32)],
        compiler_params=_cparams("parallel"),
        name="proj1",
    )(x, g, w, wvt, wwit)


def _post_kernel(h_ref, ya_ref, yb_ref, wo_ref, g_ref, wup_ref, wdn_ref, gf_ref, out_ref,
                 h1_s, xn_s, acc_s, *, final_norm):
    j = pl.program_id(1)
    half = ya_ref.shape[1]

    @pl.when(j == 0)
    def _():
        h1 = h_ref[...] + _dot(ya_ref[...], wo_ref[:half, :]) + _dot(yb_ref[...], wo_ref[half:, :])
        h1_s[...] = h1
        xn_s[...] = _rms(h1, g_ref[...]).astype(BF16)
        acc_s[...] = jnp.zeros_like(acc_s)

    r = jnp.maximum(_dot(xn_s[...], wup_ref[...]), 0.0)
    acc_s[...] += _dot((r * r).astype(BF16), wdn_ref[...])

    @pl.when(j == pl.num_programs(1) - 1)
    def _():
        o = h1_s[...] + acc_s[...]
        if final_norm:
            o = _rms(o, gf_ref[...])
        out_ref[...] = o


def _post(h, ya, yb_spec_arg, wo, g, wup, wdn, gf, *, final_norm):
    n, d = h.shape
    tm = min(ROW_TILE, n)
    yb, yb_col = yb_spec_arg
    half = d // 2
    dff = wup.shape[1]
    return pl.pallas_call(
        functools.partial(_post_kernel, final_norm=final_norm),
        grid=(n // tm, dff // FF_CHUNK),
        in_specs=[pl.BlockSpec((tm, d), lambda i, j: (i, 0)),
                  pl.BlockSpec((tm, half), lambda i, j: (i, 0)),
                  pl.BlockSpec((tm, half), lambda i, j: (i, yb_col)),
                  pl.BlockSpec((d, d), lambda i, j: (0, 0)),
                  pl.BlockSpec((1, d), lambda i, j: (0, 0)),
                  pl.BlockSpec((d, FF_CHUNK), lambda i, j: (0, j)),
                  pl.BlockSpec((FF_CHUNK, d), lambda i, j: (j, 0)),
                  pl.BlockSpec((1, d), lambda i, j: (0, 0))],
        out_specs=pl.BlockSpec((tm, d), lambda i, j: (i, 0)),
        out_shape=jax.ShapeDtypeStruct((n, d), F32),
        scratch_shapes=[pltpu.VMEM((tm, d), F32), pltpu.VMEM((tm, d), BF16), pltpu.VMEM((tm, d), F32)],
        compiler_params=_cparams("parallel", "arbitrary"),
        name="post_final" if final_norm else "post",
    )(h, ya, yb, wo, g, wup, wdn, gf)


S5_SPLIT = 2


def _s5_input(u, wb_ref):
    width, nstate = wb_ref.shape[0], wb_ref.shape[1] // 2
    wi, ns = width // S5_SPLIT, nstate // S5_SPLIT
    u_bf = u.astype(BF16)
    part = lambda base: jnp.concatenate(
        [_dot(u_bf[:, r * wi:(r + 1) * wi], wb_ref[r * wi:(r + 1) * wi, base + r * ns:base + (r + 1) * ns])
         for r in range(S5_SPLIT)], axis=1)
    return part(0), part(nstate)


def _s5_output(hre, him, u, wc_ref, d_ref, wg_ref, bg_ref):
    nstate, width = hre.shape[1], wc_ref.shape[1]
    wi, ns = width // S5_SPLIT, nstate // S5_SPLIT
    hre_bf, him_bf = hre.astype(BF16), him.astype(BF16)
    y = jnp.concatenate(
        [_dot(hre_bf[:, r * ns:(r + 1) * ns], wc_ref[r * ns:(r + 1) * ns, r * wi:(r + 1) * wi])
         + _dot(him_bf[:, r * ns:(r + 1) * ns], wc_ref[nstate + r * ns:nstate + (r + 1) * ns, r * wi:(r + 1) * wi])
         for r in range(S5_SPLIT)], axis=1)
    y = jax.nn.gelu(y + d_ref[...] * u)
    gate = jax.nn.sigmoid(_dot(y.astype(BF16), wg_ref[...]) + bg_ref[...])
    return (y * gate).astype(BF16)


def _s5_seq_kernel(u_ref, h0_ref, lam_ref, wb_ref, wc_ref, d_ref, wg_ref, bg_ref, y_ref, ht_ref,
                   hre_s, him_s, carry_s):
    half = hre_s.shape[1]

    @pl.when(pl.program_id(1) == 0)
    def _():
        carry_s[...] = h0_ref[...]

    u = u_ref[...]
    hre_s[...], him_s[...] = _s5_input(u, wb_ref)
    a_re = lam_ref[0:1, :]
    a_im = lam_ref[1:2, :]

    def step(t, carry):
        h_re, h_im = carry
        n_re = a_re * h_re - a_im * h_im + hre_s[pl.ds(t, 1), :]
        n_im = a_re * h_im + a_im * h_re + him_s[pl.ds(t, 1), :]
        hre_s[pl.ds(t, 1), :] = n_re
        him_s[pl.ds(t, 1), :] = n_im
        return n_re, n_im

    h_re, h_im = lax.fori_loop(0, hre_s.shape[0], step, (carry_s[:, :half], carry_s[:, half:]), unroll=8)
    carry_s[:, :half] = h_re
    carry_s[:, half:] = h_im
    ht_ref[...] = carry_s[...]
    y_ref[...] = _s5_output(hre_s[...], him_s[...], u, wc_ref, d_ref, wg_ref, bg_ref)


def _s5_step_kernel(u_ref, h0_ref, lam_ref, wb_ref, wc_ref, d_ref, wg_ref, bg_ref, y_ref, ht_ref):
    half = lam_ref.shape[1]
    u = u_ref[...]
    bu_re, bu_im = _s5_input(u, wb_ref)
    a_re = lam_ref[0:1, :]
    a_im = lam_ref[1:2, :]
    h_re = h0_ref[:, :half]
    h_im = h0_ref[:, half:]
    n_re = a_re * h_re - a_im * h_im + bu_re
    n_im = a_re * h_im + a_im * h_re + bu_im
    ht_ref[:, :half] = n_re
    ht_ref[:, half:] = n_im
    y_ref[...] = _s5_output(n_re, n_im, u, wc_ref, d_ref, wg_ref, bg_ref)


def _s5_param_specs(width, nstate, imap):
    return [pl.BlockSpec((2, nstate), imap),
            pl.BlockSpec((width, 2 * nstate), imap),
            pl.BlockSpec((2 * nstate, width), imap),
            pl.BlockSpec((1, width), imap),
            pl.BlockSpec((width, width), imap),
            pl.BlockSpec((1, width), imap)]


def _s5_seq(z0, h0, params, width):
    bsz, t, _ = z0.shape
    nstate = params[0].shape[1]
    ts = min(S5_CHUNK, t)
    const = lambda b, c: (0, 0)
    return pl.pallas_call(
        _s5_seq_kernel,
        grid=(bsz, t // ts),
        in_specs=[pl.BlockSpec((None, ts, width), lambda b, c: (b, c, 0)),
                  pl.BlockSpec((None, 1, 2 * nstate), lambda b, c: (b, 0, 0))]
                 + _s5_param_specs(width, nstate, const),
        out_specs=[pl.BlockSpec((None, ts, width), lambda b, c: (b, c, 0)),
                   pl.BlockSpec((None, 1, 2 * nstate), lambda b, c: (b, 0, 0))],
        out_shape=[jax.ShapeDtypeStruct((bsz, t, width), BF16),
                   jax.ShapeDtypeStruct((bsz, 1, 2 * nstate), F32)],
        scratch_shapes=[pltpu.VMEM((ts, nstate), F32), pltpu.VMEM((ts, nstate), F32),
                        pltpu.VMEM((1, 2 * nstate), F32)],
        compiler_params=_cparams("parallel", "arbitrary"),
        name="s5_scan",
    )(z0, h0, *params)


def _s5_step(z0, h0, params, width):
    rows = z0.shape[0]
    nstate = params[0].shape[1]
    const = lambda i: (0, 0)
    return pl.pallas_call(
        _s5_step_kernel,
        grid=(1,),
        in_specs=[pl.BlockSpec((rows, width), const), pl.BlockSpec((rows, 2 * nstate), const)]
                 + _s5_param_specs(width, nstate, const),
        out_specs=[pl.BlockSpec((rows, width), const), pl.BlockSpec((rows, 2 * nstate), const)],
        out_shape=[jax.ShapeDtypeStruct((rows, width), BF16),
                   jax.ShapeDtypeStruct((rows, 2 * nstate), F32)],
        compiler_params=_cparams("arbitrary"),
        name="s5_step",
    )(z0, h0, *params)


def _mlstm_chunk_kernel(q_ref, k_ref, v_ref, o_ref, gcol_ref, grow_ref, bcol_ref, brow_ref,
                        c0_ref, n0_ref, m0_ref, y_ref, c_ref, n_ref, m_ref):
    nh = c_ref.shape[0]
    ch = q_ref.shape[0]

    @pl.when(pl.program_id(1) == 0)
    def _():
        c_ref[...] = c0_ref[...]
        n_ref[...] = n0_ref[...]
        m_ref[...] = m0_ref[...]

    gcol = gcol_ref[...] + brow_ref[...]
    grow = grow_ref[...] + bcol_ref[...]
    t_idx = lax.broadcasted_iota(I32, (ch, ch), 0)
    s_idx = lax.broadcasted_iota(I32, (ch, ch), 1)
    causal = t_idx >= s_idx
    for h in range(nh):
        sl = slice(h * HEAD_DIM, (h + 1) * HEAD_DIM)
        q = q_ref[:, sl]
        k = k_ref[:, sl] * (HEAD_DIM ** -0.5)
        v = v_ref[:, sl]
        q_bf, k_bf, v_bf = q.astype(BF16), k.astype(BF16), v.astype(BF16)
        i_col = gcol[:, h:h + 1]
        i_row = grow[h:h + 1, :]
        lf_col = jax.nn.log_sigmoid(gcol[:, nh + h:nh + h + 1])
        lf_row = jax.nn.log_sigmoid(grow[nh + h:nh + h + 1, :])
        b_col = jnp.sum(jnp.where(causal, lf_row, 0.0), axis=1, keepdims=True)
        b_row = jnp.sum(jnp.where(causal, 0.0, lf_col) , axis=0, keepdims=True)
        b_row = b_row + lf_row
        m_prev = m_ref[h][:, 0:1]
        dmat = jnp.where(causal, b_col - b_row + i_row, -jnp.inf)
        a_col = b_col + m_prev
        mj = jnp.maximum(a_col, jnp.max(dmat, axis=1, keepdims=True))
        w_intra = jnp.exp(dmat - mj)
        w_inter = jnp.exp(a_col - mj)
        s = _dot_nt(q_bf, k_bf) * w_intra
        c_prev = c_ref[h]
        n_prev = n_ref[h]
        num = _dot(s.astype(BF16), v_bf) + w_inter * _dot(q_bf, c_prev.astype(BF16))
        den = jnp.sum(s, axis=1, keepdims=True) + w_inter * jnp.sum(q * n_prev, axis=1, keepdims=True)
        hout = num / jnp.maximum(jnp.abs(den), jnp.exp(-mj))
        y_ref[:, sl] = (jax.nn.sigmoid(o_ref[:, sl]) * hout).astype(BF16)
        m_new = mj[ch - 1:ch, :]
        b_last = b_col[ch - 1:ch, :]
        w_end = jnp.exp(b_last - b_col + i_col - m_new)
        decay = jnp.exp(b_last + m_prev - m_new)
        kw = k * w_end
        c_ref[h] = decay * c_prev + _dot_tn(kw.astype(BF16), v_bf)
        n_ref[h] = decay * n_prev + jnp.sum(kw, axis=0, keepdims=True)
        m_ref[h] = jnp.broadcast_to(m_new, (1, LANES))


def _mlstm_chunked(z0, gcol, grow, bias_col, bias_row, c0, n0, m0, nh):
    bsz, t, _ = z0.shape
    width = nh * HEAD_DIM
    ch = MLSTM_CHUNK
    zspec = lambda blk: pl.BlockSpec((None, ch, width), lambda b, c: (b, c, blk))
    state = lambda shape: pl.BlockSpec((None,) + shape, lambda b, c: (b,) + (0,) * len(shape))
    return pl.pallas_call(
        _mlstm_chunk_kernel,
        grid=(bsz, t // ch),
        in_specs=[zspec(1), zspec(2), zspec(3), zspec(4),
                  pl.BlockSpec((None, ch, 2 * nh), lambda b, c: (b, c, 0)),
                  pl.BlockSpec((None, 2 * nh, ch), lambda b, c: (b, 0, c)),
                  pl.BlockSpec((2 * nh, 1), lambda b, c: (0, 0)),
                  pl.BlockSpec((1, 2 * nh), lambda b, c: (0, 0)),
                  state((nh, HEAD_DIM, HEAD_DIM)), state((nh, 1, HEAD_DIM)), state((nh, 1, LANES))],
        out_specs=[pl.BlockSpec((None, ch, width), lambda b, c: (b, c, 0)),
                   state((nh, HEAD_DIM, HEAD_DIM)), state((nh, 1, HEAD_DIM)), state((nh, 1, LANES))],
        out_shape=[jax.ShapeDtypeStruct((bsz, t, width), BF16),
                   jax.ShapeDtypeStruct((bsz, nh, HEAD_DIM, HEAD_DIM), F32),
                   jax.ShapeDtypeStruct((bsz, nh, 1, HEAD_DIM), F32),
                   jax.ShapeDtypeStruct((bsz, nh, 1, LANES), F32)],
        compiler_params=_cparams("parallel", "arbitrary"),
        name="mlstm_chunk",
    )(z0, z0, z0, z0, gcol, grow, bias_col, bias_row, c0, n0, m0)


def _to_column(row):
    n = row.shape[1]
    eye = lax.broadcasted_iota(I32, (n, n), 0) == lax.broadcasted_iota(I32, (n, n), 1)
    return jnp.sum(jnp.where(eye, row, 0.0), axis=1, keepdims=True)


def _mlstm_step_kernel(q_ref, k_ref, v_ref, o_ref, g_ref, brow_ref, c0_ref, n0_ref, m0_ref,
                       y_ref, c_ref, n_ref, m_ref):
    nh = c_ref.shape[0]
    g = g_ref[...] + brow_ref[...]
    for h in range(nh):
        sl = slice(h * HEAD_DIM, (h + 1) * HEAD_DIM)
        q = q_ref[:, sl]
        k = k_ref[:, sl] * (HEAD_DIM ** -0.5)
        v = v_ref[:, sl]
        i_pre = g[:, h:h + 1]
        lf = jax.nn.log_sigmoid(g[:, nh + h:nh + h + 1])
        m_prev = m0_ref[h][:, 0:1]
        c_prev = c0_ref[h]
        n_prev = n0_ref[h]
        a = lf + m_prev
        mj = jnp.maximum(a, i_pre)
        w_intra = jnp.exp(i_pre - mj)
        w_inter = jnp.exp(a - mj)
        s = jnp.sum(q * k, axis=1, keepdims=True) * w_intra
        q_col = _to_column(q)
        k_col = _to_column(k)
        num = s * v + w_inter * jnp.sum(q_col * c_prev, axis=0, keepdims=True)
        den = s + w_inter * jnp.sum(q * n_prev, axis=1, keepdims=True)
        hout = num / jnp.maximum(jnp.abs(den), jnp.exp(-mj))
        y_ref[:, sl] = (jax.nn.sigmoid(o_ref[:, sl]) * hout).astype(BF16)
        w_end = jnp.exp(i_pre - mj)
        decay = jnp.exp(a - mj)
        c_ref[h] = decay * c_prev + (w_end * k_col) * v
        n_ref[h] = decay * n_prev + w_end * k
        m_ref[h] = jnp.broadcast_to(mj, (1, LANES))


def _mlstm_step(z0, g, bias_row, c0, n0, m0, nh):
    bsz = z0.shape[0]
    width = nh * HEAD_DIM
    zspec = lambda blk: pl.BlockSpec((None, 1, width), lambda b: (b, 0, blk))
    state = lambda shape: pl.BlockSpec((None,) + shape, lambda b: (b,) + (0,) * len(shape))
    return pl.pallas_call(
        _mlstm_step_kernel,
        grid=(bsz,),
        in_specs=[zspec(1), zspec(2), zspec(3), zspec(4),
                  pl.BlockSpec((None, 1, 2 * nh), lambda b: (b, 0, 0)),
                  pl.BlockSpec((1, 2 * nh), lambda b: (0, 0)),
                  state((nh, HEAD_DIM, HEAD_DIM)), state((nh, 1, HEAD_DIM)), state((nh, 1, LANES))],
        out_specs=[pl.BlockSpec((None, 1, width), lambda b: (b, 0, 0)),
                   state((nh, HEAD_DIM, HEAD_DIM)), state((nh, 1, HEAD_DIM)), state((nh, 1, LANES))],
        out_shape=[jax.ShapeDtypeStruct((bsz, 1, width), BF16),
                   jax.ShapeDtypeStruct((bsz, nh, HEAD_DIM, HEAD_DIM), F32),
                   jax.ShapeDtypeStruct((bsz, nh, 1, HEAD_DIM), F32),
                   jax.ShapeDtypeStruct((bsz, nh, 1, LANES), F32)],
        compiler_params=_cparams("parallel"),
        name="mlstm_step",
    )(z0, z0, z0, z0, g, bias_row, c0, n0, m0)


def _key_to_float(key):
    bits = key ^ ((key >> 31) & jnp.int32(0x7FFFFFFF))
    return lax.bitcast_convert_type(bits, F32)


def _kth_by_bit_search(count, shape):
    def bit_step(it, key):
        cand = key + (jnp.int32(1) << (31 - it))
        cand_f = _key_to_float(cand)
        cnt = count(lambda tile, col: tile >= cand_f)
        return jnp.where(cnt >= TOPK, cand, key)

    key = lax.fori_loop(0, 32, bit_step, jnp.full(shape, INT_MIN, I32))
    has_thr = key > INT_MIN
    return jnp.where(has_thr, _key_to_float(jnp.where(has_thr, key, 0)), -jnp.inf)


def _select_threshold(count, shape, width, j0_s, thr):
    has_thr = thr > -jnp.inf
    need = TOPK - count(lambda tile, col: tile > thr)
    n_eq = count(lambda tile, col: tile == thr)
    j0_s[...] = jnp.where(has_thr, jnp.int32(width), jnp.int32(-1))
    surplus = jnp.max(jnp.where(has_thr & (n_eq > need), 1, 0))
    nbits = max(1, (width - 1).bit_length())

    @pl.when(surplus > 0)
    def _():
        def idx_step(it, j0):
            cand = j0 | (jnp.int32(1) << (nbits - 1 - it))
            cnt = count(lambda tile, col: (tile == thr) & (col < cand))
            return jnp.where(cnt < need, cand, j0)

        j0 = lax.fori_loop(0, nbits, idx_step, jnp.zeros(shape, I32))
        j0_s[...] = jnp.where(has_thr, j0, jnp.int32(-1))

    return thr, j0_s[...]


def _network_pass(load, store, stages, groups):
    for grp in groups:
        vals = [load(i) for i in grp]
        pos = {gi: n for n, gi in enumerate(grp)}
        for size, dist in stages:
            for gi in grp:
                gl = gi ^ dist
                if gl > gi:
                    a, b = pos[gi], pos[gl]
                    hi, lo = jnp.maximum(vals[a], vals[b]), jnp.minimum(vals[a], vals[b])
                    vals[a], vals[b] = (hi, lo) if (gi & size) == 0 else (lo, hi)
        for n, gi in enumerate(grp):
            store(gi, vals[n])


_NET_GROUP = 16
_LOW_GROUPS = [[m * _NET_GROUP + t for t in range(_NET_GROUP)] for m in range(SORT_TILES // _NET_GROUP)]
_HIGH_GROUPS = [[m + (SORT_TILES // _NET_GROUP) * t for t in range(_NET_GROUP)]
                for m in range(SORT_TILES // _NET_GROUP)]


def _tile(ref, base, i):
    return ref.at[pl.ds(base + 8 * i, 8), :]


def _sort_block_desc(src, src_base, dst):
    assert SORT_TILES == 256 and _NET_GROUP == 16
    low = lambda size: [(size, d) for d in (8, 4, 2, 1) if d < size]
    first = [st for size in (2, 4, 8, 16) for st in low(size)]
    _network_pass(lambda i: _tile(src, src_base, i)[...],
                  lambda i, v: _tile(dst, 0, i).__setitem__(Ellipsis, v), first, _LOW_GROUPS)
    ld = lambda i: _tile(dst, 0, i)[...]
    st = lambda i, v: _tile(dst, 0, i).__setitem__(Ellipsis, v)
    for size in (32, 64, 128, 256):
        _network_pass(ld, st, [(size, d) for d in (128, 64, 32, 16) if d < size], _HIGH_GROUPS)
        _network_pass(ld, st, low(size), _LOW_GROUPS)


def _merge_top(run, other, tmp, shift=None):
    def ld(i):
        o = _tile(other, 0, SORT_TILES - 1 - i)[...]
        if shift is not None:
            o = pltpu.roll(o, shift, axis=0)
        return jnp.maximum(_tile(run, 0, i)[...], o)

    _network_pass(ld, lambda i, v: _tile(tmp, 0, i).__setitem__(Ellipsis, v),
                  [(SORT_TILES, d) for d in (128, 64, 32, 16)], _HIGH_GROUPS)
    _network_pass(lambda i: _tile(tmp, 0, i)[...], lambda i, v: _tile(run, 0, i).__setitem__(Ellipsis, v),
                  [(SORT_TILES, d) for d in (8, 4, 2, 1)], _LOW_GROUPS)


def _mask_bias(scores, cols, thr, j0):
    sel = (scores > thr) | ((scores == thr) & (cols <= j0))
    return jnp.where(sel, 0.0, NEG_BIG)


def _dsa_prompt_kernel(q_ref, qi_ref, wit_ref, ki_ref, k_ref, vt_ref, o_ref, sc_s, j0_s, m_s, acc_s,
                       run_s, blk_s, tmp_s):
    qb = q_ref.shape[0]
    kc_len = KEY_CHUNK
    nheads = q_ref.shape[1] // HEAD_DIM
    nkv = k_ref.shape[1] // HEAD_DIM
    rep = nheads // nkv
    i = pl.program_id(1)
    nchunks = ((i + 1) * qb + kc_len - 1) // kc_len
    qpos = i * qb + lax.broadcasted_iota(I32, (1, qb), 1)
    sub = lax.broadcasted_iota(I32, (kc_len, qb), 0)

    qi = qi_ref[...]
    wit = wit_ref[...]
    qi_all = jnp.concatenate([qi[:, h * IDX_DIM:(h + 1) * IDX_DIM] for h in range(IDX_HEADS)], axis=0)

    def score_chunk(c, _):
        off = pl.multiple_of(c * kc_len, kc_len)
        ki = ki_ref[pl.ds(off, kc_len), :]
        logits = _dot_nt(ki, qi_all)
        sc = jnp.zeros((kc_len, qb), F32)
        for h in range(IDX_HEADS):
            sc = sc + jnp.maximum(logits[:, h * qb:(h + 1) * qb], 0.0) * wit[h:h + 1, :]
        sc_s[pl.ds(off, kc_len), :] = jnp.where(sub + off <= qpos, sc, -jnp.inf)
        return 0

    lax.fori_loop(0, nchunks, score_chunk, 0)

    sort_rows = 8 * SORT_TILES
    nblocks = ((i + 1) * qb + sort_rows - 1) // sort_rows

    @pl.when(nchunks * kc_len < nblocks * sort_rows)
    def _():
        sc_s[pl.ds(pl.multiple_of(nchunks * kc_len, kc_len), kc_len), :] = jnp.full((kc_len, qb), -jnp.inf, F32)

    run_s[...] = jnp.full_like(run_s, -jnp.inf)

    def sort_block(b, _):
        _sort_block_desc(sc_s, pl.multiple_of(b * sort_rows, sort_rows), blk_s)
        _merge_top(run_s, blk_s, tmp_s)
        return 0

    lax.fori_loop(0, nblocks, sort_block, 0)
    for shift in (4, 2, 1):
        _merge_top(run_s, run_s, tmp_s, shift)
    kth = run_s[8 * (SORT_TILES - 1):8 * (SORT_TILES - 1) + 1, :]

    def count(pred):
        def body(c, acc):
            off = pl.multiple_of(c * COUNT_CHUNK, COUNT_CHUNK)
            ind = jnp.where(pred(sc_s[pl.ds(off, COUNT_CHUNK), :], sub[:COUNT_CHUNK] + off), 1, 0)
            return acc + jnp.sum(ind.reshape(COUNT_CHUNK // 8, 8, qb), axis=0)

        ncount = ((i + 1) * qb + COUNT_CHUNK - 1) // COUNT_CHUNK
        acc = lax.fori_loop(0, ncount, body, jnp.zeros((8, qb), I32))
        return jnp.sum(acc, axis=0, keepdims=True)

    thr, j0 = _select_threshold(count, (1, qb), sc_s.shape[0], j0_s, kth)

    m_s[...] = jnp.full_like(m_s, NEG_BIG)
    acc_s[...] = jnp.zeros_like(acc_s)
    q = q_ref[...]
    q_g = [jnp.concatenate([q[:, (g * rep + r) * HEAD_DIM:(g * rep + r + 1) * HEAD_DIM]
                            for r in range(rep)], axis=0) for g in range(nkv)]
    ones_rows = jnp.ones((acc_s.shape[1] - HEAD_DIM, kc_len), BF16)

    def attend_chunk(c, _):
        off = pl.multiple_of(c * kc_len, kc_len)
        bias = _mask_bias(sc_s[pl.ds(off, kc_len), :], sub + off, thr, j0)
        bias = jnp.concatenate([bias] * rep, axis=1)
        for g in range(nkv):
            kc = k_ref[pl.ds(off, kc_len), g * HEAD_DIM:(g + 1) * HEAD_DIM]
            att = _dot_nt(kc, q_g[g]) + bias
            m_old = m_s[g]
            m_new = jnp.maximum(m_old, jnp.max(att, axis=0, keepdims=True))
            p = jnp.exp2(att - m_new).astype(BF16)
            vt = jnp.concatenate([vt_ref[g * HEAD_DIM:(g + 1) * HEAD_DIM, pl.ds(off, kc_len)], ones_rows],
                                 axis=0)
            acc_s[g] = jnp.exp2(m_old - m_new) * acc_s[g] + _dot(vt, p)
            m_s[g] = m_new
        return 0

    lax.fori_loop(0, nchunks, attend_chunk, 0)

    for g in range(nkv):
        acc = acc_s[g]
        out = acc[:HEAD_DIM, :] / acc[HEAD_DIM:HEAD_DIM + 1, :]
        for r in range(rep):
            hd = g * rep + r
            o_ref[:, hd * HEAD_DIM:(hd + 1) * HEAD_DIM] = out[:, r * qb:(r + 1) * qb].T.astype(BF16)


def _dsa_prompt(q, qi, wit, kidx, kbf, vt):
    bsz, t, width = q.shape
    kvw = kbf.shape[2]
    nkv = kvw // HEAD_DIM
    rep = width // HEAD_DIM // nkv
    qb = Q_BLOCK
    nq = t // qb
    assert t % (8 * SORT_TILES) == 0 and 8 * SORT_TILES == 2 * KEY_CHUNK and TOPK == SORT_TILES
    ones_rows = 16
    return pl.pallas_call(
        _dsa_prompt_kernel,
        grid=(bsz, nq),
        in_specs=[pl.BlockSpec((None, qb, width), lambda b, i: (b, i, 0)),
                  pl.BlockSpec((None, qb, qi.shape[2]), lambda b, i: (b, i, 0)),
                  pl.BlockSpec((IDX_HEADS, qb), lambda b, i: (0, b * nq + i)),
                  pl.BlockSpec((None, t, IDX_DIM), lambda b, i: (b, 0, 0)),
                  pl.BlockSpec((None, t, kvw), lambda b, i: (b, 0, 0)),
                  pl.BlockSpec((kvw, t), lambda b, i: (0, b))],
        out_specs=pl.BlockSpec((None, qb, width), lambda b, i: (b, i, 0)),
        out_shape=jax.ShapeDtypeStruct((bsz, t, width), BF16),
        scratch_shapes=[pltpu.VMEM((t, qb), F32), pltpu.VMEM((1, qb), I32),
                        pltpu.VMEM((nkv, 1, rep * qb), F32),
                        pltpu.VMEM((nkv, HEAD_DIM + ones_rows, rep * qb), F32),
                        pltpu.VMEM((8 * SORT_TILES, qb), F32), pltpu.VMEM((8 * SORT_TILES, qb), F32),
                        pltpu.VMEM((8 * SORT_TILES, qb), F32)],
        compiler_params=_cparams("parallel", "arbitrary"),
        name="dsa_prompt",
    )(q, qi, wit, kidx, kbf, vt)


def _dsa_score_kernel(pt_ref, qi_ref, wi_ref, knew_ref, *rest):
    pages, (keys_ref, newkey_ref) = rest[:PAGES_PER_STEP], rest[PAGES_PER_STEP:]
    qi = qi_ref[...]
    wi = wi_ref[...]
    ki = jnp.concatenate([page_ref[...].astype(BF16) for page_ref in pages], axis=0)
    logits = _dot_nt(qi.astype(BF16), ki)
    keys_ref[...] = jnp.sum(jnp.maximum(logits, 0.0) * wi, axis=0, keepdims=True)
    logit_new = jnp.sum(qi * knew_ref[...], axis=1, keepdims=True)
    sc_new = jnp.sum(jnp.maximum(logit_new, 0.0) * wi, axis=0, keepdims=True)
    newkey_ref[...] = jnp.broadcast_to(sc_new, (1, LANES))


def _dsa_score(page_table, qi, wi, kidx_new, cache_kidx):
    bsz, npages = page_table.shape
    steps = npages // PAGES_PER_STEP
    page_spec = lambda p: pl.BlockSpec(
        (None, PAGE, IDX_DIM), lambda b, j, pt: (pt[b, j * PAGES_PER_STEP + p], 0, 0))
    per_seq = lambda shape: pl.BlockSpec((None,) + shape, lambda b, j, pt: (b, 0, 0))
    return pl.pallas_call(
        _dsa_score_kernel,
        grid_spec=pltpu.PrefetchScalarGridSpec(
            num_scalar_prefetch=1,
            grid=(bsz, steps),
            in_specs=[per_seq((IDX_HEADS, IDX_DIM)), per_seq((IDX_HEADS, 1)), per_seq((1, IDX_DIM))]
                     + [page_spec(p) for p in range(PAGES_PER_STEP)],
            out_specs=[pl.BlockSpec((None, 1, PAGES_PER_STEP * PAGE), lambda b, j, pt: (b, 0, j)),
                       per_seq((1, LANES))]),
        out_shape=[jax.ShapeDtypeStruct((bsz, 1, npages * PAGE), F32),
                   jax.ShapeDtypeStruct((bsz, 1, LANES), F32)],
        compiler_params=_cparams("parallel", "arbitrary"),
        name="dsa_decode_score",
    )(page_table, qi, wi, kidx_new, *([cache_kidx] * PAGES_PER_STEP))


def _dsa_select_kernel(keys_ref, thr_ref, j0_ref, j0_s):
    rows, width = keys_ref.shape
    lane = lax.broadcasted_iota(I32, (rows, LANES), 1)

    def count(pred):
        def body(t, acc):
            off = pl.multiple_of(t * LANES, LANES)
            return acc + jnp.where(pred(keys_ref[:, pl.ds(off, LANES)], lane + off), 1, 0)

        acc = lax.fori_loop(0, width // LANES, body, jnp.zeros((rows, LANES), I32))
        return jnp.sum(acc, axis=1, keepdims=True)

    thr, j0 = _select_threshold(count, (rows, 1), width, j0_s, _kth_by_bit_search(count, (rows, 1)))
    thr_ref[...] = jnp.broadcast_to(thr, thr_ref.shape)
    j0_ref[...] = jnp.broadcast_to(j0, j0_ref.shape)


def _dsa_select(keys):
    rows, width = keys.shape
    const = lambda i: (0, 0)
    return pl.pallas_call(
        _dsa_select_kernel,
        grid=(1,),
        in_specs=[pl.BlockSpec((rows, width), const)],
        out_specs=[pl.BlockSpec((rows, LANES), const), pl.BlockSpec((rows, LANES), const)],
        out_shape=[jax.ShapeDtypeStruct((rows, LANES), F32), jax.ShapeDtypeStruct((rows, LANES), I32)],
        scratch_shapes=[pltpu.VMEM((rows, 1), I32)],
        compiler_params=_cparams("arbitrary"),
        name="dsa_decode_select",
    )(keys)


def _dsa_decode_kernel(pt_ref, q_ref, keys_ref, tail_ref, thr_ref, j0_ref, knew_ref, vnew_ref, *rest,
                       n_past, nkv):
    j = pl.program_id(1)
    nheads = q_ref.shape[0]
    rep = nheads // nkv
    kpages, vpages = rest[:PAGES_PER_STEP], rest[PAGES_PER_STEP:2 * PAGES_PER_STEP]
    o_ref, m_s, l_s, acc_s = rest[2 * PAGES_PER_STEP:]

    @pl.when(j == 0)
    def _():
        m_s[...] = jnp.full_like(m_s, NEG_BIG)
        l_s[...] = jnp.zeros_like(l_s)
        acc_s[...] = jnp.zeros_like(acc_s)

    q = q_ref[...]
    thr = thr_ref[:, 0:1]
    j0 = j0_ref[:, 0:1]
    width = PAGES_PER_STEP * PAGE * nkv
    row = lax.broadcasted_iota(I32, (1, width), 1) + j * width
    head_kv = lax.broadcasted_iota(I32, (nheads, 1), 0) // rep

    def update(att, value_fn):
        m_old = m_s[...]
        m_new = jnp.maximum(m_old, jnp.max(att, axis=1, keepdims=True))
        alpha = jnp.exp2(m_old - m_new)
        p = jnp.exp2(att - m_new)
        l_s[...] = alpha * l_s[...] + jnp.sum(p, axis=1, keepdims=True)
        acc_s[...] = alpha * acc_s[...] + value_fn(p)
        m_s[...] = m_new

    kcat = jnp.concatenate([r[...].astype(BF16) for r in kpages], axis=0)
    vcat = jnp.concatenate([r[...].astype(BF16) for r in vpages], axis=0)
    bias = _mask_bias(keys_ref[...], row // nkv, thr, j0)
    bias = jnp.where(row % nkv == head_kv, bias, NEG_BIG)
    update(_dot_nt(q.astype(BF16), kcat) + bias, lambda pr: _dot(pr.astype(BF16), vcat))

    @pl.when(j == pl.num_programs(1) - 1)
    def _():
        k_new = jnp.zeros_like(q)
        v_new = jnp.zeros_like(q)
        for g in range(nkv):
            k_new = jnp.where(head_kv == g, knew_ref[g:g + 1, :], k_new)
            v_new = jnp.where(head_kv == g, vnew_ref[g:g + 1, :], v_new)
        bias_new = _mask_bias(tail_ref[:, 0:1], jnp.int32(n_past), thr, j0)
        att_new = jnp.sum(q * k_new, axis=1, keepdims=True) + bias_new
        update(att_new, lambda pr: pr * v_new)
        o_ref[...] = (acc_s[...] / l_s[...]).astype(BF16)


def _dsa_decode(page_table, q, keys, tail, thr, j0, k_new, v_new, cache_k, cache_v):
    bsz, npages = page_table.shape
    nheads = q.shape[1]
    nkv = k_new.shape[1]
    steps = npages // PAGES_PER_STEP
    page_spec = lambda p: pl.BlockSpec(
        (PAGE * nkv, HEAD_DIM), lambda b, j, pt: (pt[b, j * PAGES_PER_STEP + p], 0))
    per_seq = lambda shape: pl.BlockSpec((None,) + shape, lambda b, j, pt: (b, 0, 0))
    keys_spec = pl.BlockSpec((None, 1, PAGES_PER_STEP * PAGE * nkv), lambda b, j, pt: (b, 0, j))
    return pl.pallas_call(
        functools.partial(_dsa_decode_kernel, n_past=npages * PAGE, nkv=nkv),
        grid_spec=pltpu.PrefetchScalarGridSpec(
            num_scalar_prefetch=1,
            grid=(bsz, steps),
            in_specs=[per_seq((nheads, HEAD_DIM)), keys_spec, per_seq((1, LANES)), per_seq((1, LANES)),
                      per_seq((1, LANES)), per_seq((nkv, HEAD_DIM)), per_seq((nkv, HEAD_DIM))]
                     + [page_spec(p) for p in range(PAGES_PER_STEP)] * 2,
            out_specs=per_seq((nheads, HEAD_DIM)),
            scratch_shapes=[pltpu.VMEM((nheads, 1), F32), pltpu.VMEM((nheads, 1), F32),
                            pltpu.VMEM((nheads, HEAD_DIM), F32)]),
        out_shape=jax.ShapeDtypeStruct((bsz, nheads, HEAD_DIM), BF16),
        compiler_params=_cparams("parallel", "arbitrary"),
        name="dsa_decode_attend",
    )(page_table, q, keys, tail, thr, j0, k_new, v_new,
      *([cache_k] * PAGES_PER_STEP), *([cache_v] * PAGES_PER_STEP))


def _pad_cols(w, width):
    return jnp.pad(w, ((0, 0), (0, width - w.shape[1])))


def _block_diag(blocks):
    g, r, c = blocks.shape
    eye = jnp.eye(g, dtype=blocks.dtype)
    return (blocks[:, :, None, :] * eye[:, None, :, None]).reshape(g * r, g * c)


def _s5_params(lam_re, lam_im, log_dt, b_re, b_im, c_re, c_im, d_skip, w_glu, b_glu):
    dt = jnp.exp(log_dt)[:, None]
    mag = jnp.exp(lam_re * dt)
    bar_re = mag * jnp.cos(lam_im * dt)
    bar_im = mag * jnp.sin(lam_im * dt)
    inv = 1.0 / (lam_re * lam_re + lam_im * lam_im)
    coef_re = (((bar_re - 1.0) * lam_re + bar_im * lam_im) * inv)[..., None]
    coef_im = ((bar_im * lam_re - (bar_re - 1.0) * lam_im) * inv)[..., None]
    bb_re = coef_re * b_re - coef_im * b_im
    bb_im = coef_re * b_im + coef_im * b_re
    lam_rows = jnp.stack([bar_re.reshape(-1), bar_im.reshape(-1)])
    to_in = lambda z: _block_diag(jnp.swapaxes(z, 1, 2))
    wb = jnp.concatenate([to_in(bb_re), to_in(bb_im)], axis=1).astype(BF16)
    to_out = lambda z: _block_diag(jnp.swapaxes(z, 1, 2))
    wc = jnp.concatenate([to_out(c_re), to_out(-c_im)], axis=0).astype(BF16)
    return (lam_rows, wb, wc, d_skip[None, :], w_glu.astype(BF16), b_glu[None, :])


def kernel(x_prompt, x_sample, state_s5_re, state_s5_im, state_mlstm_c, state_mlstm_n, state_mlstm_m,
           cache_k, cache_v, cache_kidx, page_table, norm_mix, norm_mlp, norm_final, w_in0, s5_lam_re,
           s5_lam_im, s5_log_dt, s5_b_re, s5_b_im, s5_c_re, s5_c_im, s5_d, w_glu, b_glu, b_igate,
           b_fgate, w_out0, w_in1, w_out1, w_up, w_down):
    bp, tp, d = x_prompt.shape
    db, ts, _ = x_sample.shape
    assert ts == 1, "the decode path handles one new token per sequence"
    s5_groups, s5_state = s5_lam_re.shape
    s5_width = s5_groups * S5_GROUP
    nstate = s5_groups * s5_state
    nh = b_igate.shape[0]
    ml_width = nh * HEAD_DIM
    assert s5_width == ml_width == 512 and d == 1024
    n_past = page_table.shape[1] * PAGE
    kvw = cache_k.shape[2] * cache_k.shape[3]
    nheads = w_out1.shape[0] // HEAD_DIM

    gate_cols = s5_width + 4 * ml_width
    w0 = jnp.concatenate([w_in0[:, :gate_cols], _pad_cols(w_in0[:, gate_cols:], LANES)], axis=1).astype(BF16)
    s5p = _s5_params(s5_lam_re, s5_lam_im, s5_log_dt, s5_b_re, s5_b_im, s5_c_re, s5_c_im, s5_d, w_glu, b_glu)
    gate_bias = jnp.concatenate([b_igate, b_fgate])
    bias_row, bias_col = gate_bias[None, :], gate_bias[:, None]
    ki0 = 1024 + 2 * kvw + IDX_HEADS * IDX_DIM
    w1 = jnp.concatenate([w_in1[:, :ki0], _pad_cols(w_in1[:, ki0:ki0 + IDX_DIM], LANES)], axis=1).astype(BF16)
    wvt = w_in1[:, 1024 + kvw:1024 + 2 * kvw].T.astype(BF16)
    wwit = w_in1[:, ki0 + IDX_DIM:].T.astype(BF16)
    wo0, wo1 = w_out0.astype(BF16), w_out1.astype(BF16)
    wup, wdn = w_up.astype(BF16), w_down.astype(BF16)
    g_mix, g_mlp, g_fin = norm_mix[:, None, :], norm_mlp[:, None, :], norm_final[None, :]

    def trunk(x2d, bsz, t, s5_h0, c0, n0, m0, attend):
        n = bsz * t
        z0 = _proj0(x2d, g_mix[0], w0)
        gates = z0[:, gate_cols:gate_cols + 2 * nh]
        m0b = jnp.broadcast_to(m0[:, :, None, None], (bsz, nh, 1, LANES))
        n0r = n0[:, :, None, :]
        if t == 1:
            y_s5, h_t = _s5_step(z0, s5_h0, s5p, s5_width)
            y_ml, c_t, n_t, m_t = _mlstm_step(z0.reshape(bsz, 1, -1), gates.reshape(bsz, 1, 2 * nh),
                                              bias_row, c0, n0r, m0b, nh)
        else:
            y_s5, h_t = _s5_seq(z0.reshape(bsz, t, -1), s5_h0.reshape(bsz, 1, -1), s5p, s5_width)
            g3 = gates.reshape(bsz, t, 2 * nh)
            y_ml, c_t, n_t, m_t = _mlstm_chunked(z0.reshape(bsz, t, -1), g3, jnp.swapaxes(g3, 1, 2),
                                                 bias_col, bias_row, c0, n0r, m0b, nh)
        h_t = h_t.reshape(bsz, 2, s5_groups, s5_state)
        states = (h_t[:, 0], h_t[:, 1], c_t, n_t[:, :, 0, :], m_t[:, :, 0, 0])
        h1 = _post(x2d, y_s5.reshape(n, s5_width), (y_ml.reshape(n, ml_width), 0), wo0, g_mlp[0],
                   wup[0], wdn[0], g_fin, final_norm=False)
        q, k, v, kbf, qi, kidx, kidxbf, vt, wit = _proj1(h1, g_mix[1], w1, wvt, wwit)
        o = attend(q, k, v, kbf, qi, kidx, kidxbf, vt, wit)
        y = _post(h1, o, (o, 1), wo1, g_mlp[1], wup[1], wdn[1], g_fin, final_norm=True)
        rows = (k.reshape(bsz, t, -1, HEAD_DIM), v.reshape(bsz, t, -1, HEAD_DIM), kidx.reshape(bsz, t, IDX_DIM))
        return y.reshape(bsz, t, d), states, rows

    def attend_prompt(q, k, v, kbf, qi, kidx, kidxbf, vt, wit):
        r3 = lambda z: z.reshape(bp, tp, -1)
        return _dsa_prompt(r3(q), r3(qi), wit, r3(kidxbf), r3(kbf), vt).reshape(bp * tp, -1)

    def attend_decode(q, k, v, kbf, qi, kidx, kidxbf, vt, wit):
        qi3 = qi.astype(F32).reshape(db, IDX_HEADS, IDX_DIM)
        wi3 = wit.T.reshape(db, IDX_HEADS, 1)
        keys, newkey = _dsa_score(page_table, qi3, wi3, kidx.reshape(db, 1, IDX_DIM), cache_kidx)
        tail = jnp.where(lax.broadcasted_iota(I32, (db, LANES), 1) == 0, newkey[:, 0, :], -jnp.inf)
        keys = jnp.concatenate([keys[:, 0, :], tail], axis=1)
        thr, j0 = _dsa_select(keys)
        nkv = kvw // HEAD_DIM
        keys_rows = jnp.repeat(keys[:, :n_past], nkv, axis=1)[:, None, :]
        o = _dsa_decode(page_table, q.astype(F32).reshape(db, nheads, HEAD_DIM), keys_rows,
                        keys[:, None, n_past:], thr[:, None, :], j0[:, None, :],
                        k.reshape(db, nkv, HEAD_DIM), v.reshape(db, nkv, HEAD_DIM),
                        cache_k.reshape(-1, HEAD_DIM), cache_v.reshape(-1, HEAD_DIM))
        return o.reshape(db, nheads * HEAD_DIM)

    zeros = lambda *shape: jnp.zeros(shape, F32)
    y_p, st_p, rows_p = trunk(x_prompt.reshape(bp * tp, d), bp, tp, zeros(bp, 2 * nstate),
                              zeros(bp, nh, HEAD_DIM, HEAD_DIM), zeros(bp, nh, HEAD_DIM), zeros(bp, nh),
                              attend_prompt)
    s5_h0 = jnp.concatenate([state_s5_re.reshape(db, nstate), state_s5_im.reshape(db, nstate)], axis=1)
    y_s, st_s, rows_s = trunk(x_sample.reshape(db, d), db, 1, s5_h0, state_mlstm_c, state_mlstm_n,
                              state_mlstm_m, attend_decode)
    return (y_p, y_s) + st_p + rows_p + st_s + rows_s
```

```python
import functools
import math

import jax
import jax.numpy as jnp
from jax import lax
from jax.experimental import pallas as pl
from jax.experimental.pallas import tpu as pltpu

F32, BF16, I32 = jnp.float32, jnp.bfloat16, jnp.int32

EPS = 1e-6
LANES = 128
PAGE = 128
S5_GROUP = 16
S5_STATE = 64
HEAD_DIM = 128
IDX_DIM = 64
IDX_HEADS = 8
TOPK = 256
Q_BLOCK = 128
KEY_CHUNK = 1024
COUNT_CHUNK = 512
SORT_TILES = 256
MLSTM_CHUNK = 128
S5_CHUNK = 256
ROW_TILE = 512
FF_CHUNK = 4096
PAGES_PER_STEP = 16
INT_MIN = -2 ** 31
NEG_BIG = -1e30
VMEM_LIMIT = 56 * 1024 * 1024


def _cparams(*sem):
    return pltpu.CompilerParams(dimension_semantics=sem, vmem_limit_bytes=VMEM_LIMIT)


def _rms(x, g):
    return x * lax.rsqrt(jnp.mean(x * x, axis=-1, keepdims=True) + EPS) * g


def _dot(a, b):
    return jnp.dot(a, b, preferred_element_type=F32)


def _dot_nt(a, b):
    return lax.dot_general(a, b, (((1,), (1,)), ((), ())), preferred_element_type=F32)


def _dot_tn(a, b):
    return lax.dot_general(a, b, (((0,), (0,)), ((), ())), preferred_element_type=F32)


def _proj0_kernel(x_ref, g_ref, w_ref, o_ref):
    xn = _rms(x_ref[...], g_ref[...]).astype(BF16)
    o_ref[...] = _dot(xn, w_ref[...])


def _proj0(x, g, w):
    n, d = x.shape
    tm = min(ROW_TILE, n)
    wtot = w.shape[1]
    return pl.pallas_call(
        _proj0_kernel,
        grid=(n // tm,),
        in_specs=[pl.BlockSpec((tm, d), lambda i: (i, 0)),
                  pl.BlockSpec((1, d), lambda i: (0, 0)),
                  pl.BlockSpec((d, wtot), lambda i: (0, 0))],
        out_specs=pl.BlockSpec((tm, wtot), lambda i: (i, 0)),
        out_shape=jax.ShapeDtypeStruct((n, wtot), F32),
        compiler_params=_cparams("parallel"),
        name="proj0",
    )(x, g, w)


_Q1, _K1, _V1, _QI1, _KI1, _END1 = 0, 1024, 1280, 1536, 2048, 2176
LOG2E = 1.4426950408889634


def _proj1_kernel(x_ref, g_ref, w_ref, wvt_ref, wwit_ref, q_ref, k_ref, v_ref, kbf_ref, qi_ref,
                  kidx_ref, kidxbf_ref, vt_ref, wit_ref):
    xn = _rms(x_ref[...], g_ref[...]).astype(BF16)
    z = _dot(xn, w_ref[...])
    q_ref[...] = (z[:, _Q1:_K1] * (HEAD_DIM ** -0.5 * LOG2E)).astype(BF16)
    k = z[:, _K1:_V1]
    k_ref[...] = k
    v_ref[...] = z[:, _V1:_QI1]
    kbf_ref[...] = k.astype(BF16)
    qi_ref[...] = z[:, _QI1:_KI1].astype(BF16)
    kidx = z[:, _KI1:_KI1 + IDX_DIM]
    kidx_ref[...] = kidx
    kidxbf_ref[...] = kidx.astype(BF16)
    vt_ref[...] = _dot_nt(wvt_ref[...], xn).astype(BF16)
    wit_ref[...] = _dot_nt(wwit_ref[...], xn) * ((IDX_DIM ** -0.5) * (IDX_HEADS ** -0.5))


def _proj1(x, g, w, wvt, wwit):
    n, d = x.shape
    tm = min(ROW_TILE, n)
    kvw = wvt.shape[0]
    row = lambda width: pl.BlockSpec((tm, width), lambda i: (i, 0))
    col = lambda height: pl.BlockSpec((height, tm), lambda i: (0, i))
    full = lambda a: pl.BlockSpec(a.shape, lambda i: (0, 0))
    shp = lambda width, dt: jax.ShapeDtypeStruct((n, width), dt)
    return pl.pallas_call(
        _proj1_kernel,
        grid=(n // tm,),
        in_specs=[row(d), pl.BlockSpec((1, d), lambda i: (0, 0)), full(w), full(wvt), full(wwit)],
        out_specs=[row(1024), row(kvw), row(kvw), row(kvw), row(512), row(IDX_DIM), row(IDX_DIM),
                   col(kvw), col(IDX_HEADS)],
        out_shape=[shp(1024, BF16), shp(kvw, F32), shp(kvw, F32), shp(kvw, BF16), shp(512, BF16),
                   shp(IDX_DIM, F32), shp(IDX_DIM, BF16),
                   jax.ShapeDtypeStruct((kvw, n), BF16), jax.ShapeDtypeStruct((IDX_HEADS, n), F32)],
        compiler_params=_cparams("parallel"),
        name="proj1",
    )(x, g, w, wvt, wwit)


def _post_kernel(h_ref, ya_ref, yb_ref, wo_ref, g_ref, wup_ref, wdn_ref, gf_ref, out_ref,
                 h1_s, xn_s, acc_s, *, final_norm):
    j = pl.program_id(1)
    half = ya_ref.shape[1]

    @pl.when(j == 0)
    def _():
        h1 = h_ref[...] + _dot(ya_ref[...], wo_ref[:half, :]) + _dot(yb_ref[...], wo_ref[half:, :])
        h1_s[...] = h1
        xn_s[...] = _rms(h1, g_ref[...]).astype(BF16)
        acc_s[...] = jnp.zeros_like(acc_s)

    r = jnp.maximum(_dot(xn_s[...], wup_ref[...]), 0.0)
    acc_s[...] += _dot((r * r).astype(BF16), wdn_ref[...])

    @pl.when(j == pl.num_programs(1) - 1)
    def _():
        o = h1_s[...] + acc_s[...]
        if final_norm:
            o = _rms(o, gf_ref[...])
        out_ref[...] = o


def _post(h, ya, yb_spec_arg, wo, g, wup, wdn, gf, *, final_norm):
    n, d = h.shape
    tm = min(ROW_TILE, n)
    yb, yb_col = yb_spec_arg
    half = d // 2
    dff = wup.shape[1]
    once = pl.Buffered(1) if dff == FF_CHUNK else None
    return pl.pallas_call(
        functools.partial(_post_kernel, final_norm=final_norm),
        grid=(n // tm, dff // FF_CHUNK),
        in_specs=[pl.BlockSpec((tm, d), lambda i, j: (i, 0)),
                  pl.BlockSpec((tm, half), lambda i, j: (i, 0)),
                  pl.BlockSpec((tm, half), lambda i, j: (i, yb_col)),
                  pl.BlockSpec((d, d), lambda i, j: (0, 0), pipeline_mode=once),
                  pl.BlockSpec((1, d), lambda i, j: (0, 0)),
                  pl.BlockSpec((d, FF_CHUNK), lambda i, j: (0, j), pipeline_mode=once),
                  pl.BlockSpec((FF_CHUNK, d), lambda i, j: (j, 0), pipeline_mode=once),
                  pl.BlockSpec((1, d), lambda i, j: (0, 0))],
        out_specs=pl.BlockSpec((tm, d), lambda i, j: (i, 0)),
        out_shape=jax.ShapeDtypeStruct((n, d), F32),
        scratch_shapes=[pltpu.VMEM((tm, d), F32), pltpu.VMEM((tm, d), BF16), pltpu.VMEM((tm, d), F32)],
        compiler_params=_cparams("parallel", "arbitrary"),
        name="post_final" if final_norm else "post",
    )(h, ya, yb, wo, g, wup, wdn, gf)


S5_SPLIT = 2


def _s5_input(u, wb_ref):
    width, nstate = wb_ref.shape[0], wb_ref.shape[1] // 2
    wi, ns = width // S5_SPLIT, nstate // S5_SPLIT
    u_bf = u.astype(BF16)
    part = lambda base: jnp.concatenate(
        [_dot(u_bf[:, r * wi:(r + 1) * wi], wb_ref[r * wi:(r + 1) * wi, base + r * ns:base + (r + 1) * ns])
         for r in range(S5_SPLIT)], axis=1)
    return part(0), part(nstate)


def _s5_output(hre, him, u, wc_ref, d_ref, wg_ref, bg_ref):
    nstate, width = hre.shape[1], wc_ref.shape[1]
    wi, ns = width // S5_SPLIT, nstate // S5_SPLIT
    hre_bf, him_bf = hre.astype(BF16), him.astype(BF16)
    y = jnp.concatenate(
        [_dot(hre_bf[:, r * ns:(r + 1) * ns], wc_ref[r * ns:(r + 1) * ns, r * wi:(r + 1) * wi])
         + _dot(him_bf[:, r * ns:(r + 1) * ns], wc_ref[nstate + r * ns:nstate + (r + 1) * ns, r * wi:(r + 1) * wi])
         for r in range(S5_SPLIT)], axis=1)
    y = jax.nn.gelu(y + d_ref[...] * u)
    gate = jax.nn.sigmoid(_dot(y.astype(BF16), wg_ref[...]) + bg_ref[...])
    return (y * gate).astype(BF16)


def _s5_seq_kernel(u_ref, h0_ref, lam_ref, wb_ref, wc_ref, d_ref, wg_ref, bg_ref, y_ref, ht_ref,
                   hre_s, him_s, carry_s):
    bsz, _, half = hre_s.shape

    @pl.when(pl.program_id(0) == 0)
    def _():
        carry_s[...] = h0_ref[...]

    for b in range(bsz):
        hre_s[b], him_s[b] = _s5_input(u_ref[b], wb_ref)
    a_re = lam_ref[0:1, :]
    a_im = lam_ref[1:2, :]

    def step(t, carry):
        out = []
        for b in range(bsz):
            h_re, h_im = carry[2 * b], carry[2 * b + 1]
            n_re = a_re * h_re - a_im * h_im + hre_s[b, pl.ds(t, 1), :]
            n_im = a_re * h_im + a_im * h_re + him_s[b, pl.ds(t, 1), :]
            hre_s[b, pl.ds(t, 1), :] = n_re
            him_s[b, pl.ds(t, 1), :] = n_im
            out += [n_re, n_im]
        return tuple(out)

    init = tuple(carry_s[b][:, lo:lo + half] for b in range(bsz) for lo in (0, half))
    final = lax.fori_loop(0, hre_s.shape[1], step, init, unroll=8)
    for b in range(bsz):
        carry_s[b, :, :half] = final[2 * b]
        carry_s[b, :, half:] = final[2 * b + 1]
        y_ref[b] = _s5_output(hre_s[b], him_s[b], u_ref[b], wc_ref, d_ref, wg_ref, bg_ref)
    ht_ref[...] = carry_s[...]


def _s5_step_kernel(u_ref, h0_ref, lam_ref, wb_ref, wc_ref, d_ref, wg_ref, bg_ref, y_ref, ht_ref):
    half = lam_ref.shape[1]
    u = u_ref[...]
    bu_re, bu_im = _s5_input(u, wb_ref)
    a_re = lam_ref[0:1, :]
    a_im = lam_ref[1:2, :]
    h_re = h0_ref[:, :half]
    h_im = h0_ref[:, half:]
    n_re = a_re * h_re - a_im * h_im + bu_re
    n_im = a_re * h_im + a_im * h_re + bu_im
    ht_ref[:, :half] = n_re
    ht_ref[:, half:] = n_im
    y_ref[...] = _s5_output(n_re, n_im, u, wc_ref, d_ref, wg_ref, bg_ref)


def _s5_param_specs(width, nstate, imap):
    return [pl.BlockSpec((2, nstate), imap),
            pl.BlockSpec((width, 2 * nstate), imap),
            pl.BlockSpec((2 * nstate, width), imap),
            pl.BlockSpec((1, width), imap),
            pl.BlockSpec((width, width), imap),
            pl.BlockSpec((1, width), imap)]


def _s5_seq(z0, h0, params, width):
    bsz, t, _ = z0.shape
    nstate = params[0].shape[1]
    ts = min(S5_CHUNK, t)
    const = lambda c: (0, 0)
    return pl.pallas_call(
        _s5_seq_kernel,
        grid=(t // ts,),
        in_specs=[pl.BlockSpec((bsz, ts, width), lambda c: (0, c, 0)),
                  pl.BlockSpec((bsz, 1, 2 * nstate), lambda c: (0, 0, 0))]
                 + _s5_param_specs(width, nstate, const),
        out_specs=[pl.BlockSpec((bsz, ts, width), lambda c: (0, c, 0)),
                   pl.BlockSpec((bsz, 1, 2 * nstate), lambda c: (0, 0, 0))],
        out_shape=[jax.ShapeDtypeStruct((bsz, t, width), BF16),
                   jax.ShapeDtypeStruct((bsz, 1, 2 * nstate), F32)],
        scratch_shapes=[pltpu.VMEM((bsz, ts, nstate), F32), pltpu.VMEM((bsz, ts, nstate), F32),
                        pltpu.VMEM((bsz, 1, 2 * nstate), F32)],
        compiler_params=_cparams("arbitrary"),
        name="s5_scan",
    )(z0, h0, *params)


def _s5_step(z0, h0, params, width):
    rows = z0.shape[0]
    nstate = params[0].shape[1]
    const = lambda i: (0, 0)
    return pl.pallas_call(
        _s5_step_kernel,
        grid=(1,),
        in_specs=[pl.BlockSpec((rows, width), const), pl.BlockSpec((rows, 2 * nstate), const)]
                 + _s5_param_specs(width, nstate, const),
        out_specs=[pl.BlockSpec((rows, width), const), pl.BlockSpec((rows, 2 * nstate), const)],
        out_shape=[jax.ShapeDtypeStruct((rows, width), BF16),
                   jax.ShapeDtypeStruct((rows, 2 * nstate), F32)],
        compiler_params=_cparams("arbitrary"),
        name="s5_step",
    )(z0, h0, *params)


def _mlstm_chunk_kernel(q_ref, k_ref, v_ref, o_ref, gcol_ref, grow_ref, bcol_ref, brow_ref,
                        c0_ref, n0_ref, m0_ref, y_ref, c_ref, n_ref, m_ref):
    nh = c_ref.shape[0]
    ch = q_ref.shape[0]

    @pl.when(pl.program_id(1) == 0)
    def _():
        c_ref[...] = c0_ref[...]
        n_ref[...] = n0_ref[...]
        m_ref[...] = m0_ref[...]

    gcol = gcol_ref[...] + brow_ref[...]
    grow = grow_ref[...] + bcol_ref[...]
    t_idx = lax.broadcasted_iota(I32, (ch, ch), 0)
    s_idx = lax.broadcasted_iota(I32, (ch, ch), 1)
    causal = t_idx >= s_idx
    for h in range(nh):
        sl = slice(h * HEAD_DIM, (h + 1) * HEAD_DIM)
        q = q_ref[:, sl]
        k = k_ref[:, sl] * (HEAD_DIM ** -0.5)
        v = v_ref[:, sl]
        q_bf, k_bf, v_bf = q.astype(BF16), k.astype(BF16), v.astype(BF16)
        i_col = gcol[:, h:h + 1]
        i_row = grow[h:h + 1, :]
        lf_col = jax.nn.log_sigmoid(gcol[:, nh + h:nh + h + 1])
        lf_row = jax.nn.log_sigmoid(grow[nh + h:nh + h + 1, :])
        b_col = jnp.sum(jnp.where(causal, lf_row, 0.0), axis=1, keepdims=True)
        b_row = jnp.sum(jnp.where(causal, 0.0, lf_col) , axis=0, keepdims=True)
        b_row = b_row + lf_row
        m_prev = m_ref[h][:, 0:1]
        dmat = jnp.where(causal, b_col - b_row + i_row, -jnp.inf)
        a_col = b_col + m_prev
        mj = jnp.maximum(a_col, jnp.max(dmat, axis=1, keepdims=True))
        w_intra = jnp.exp(dmat - mj)
        w_inter = jnp.exp(a_col - mj)
        s = _dot_nt(q_bf, k_bf) * w_intra
        c_prev = c_ref[h]
        n_prev = n_ref[h]
        num = _dot(s.astype(BF16), v_bf) + w_inter * _dot(q_bf, c_prev.astype(BF16))
        den = jnp.sum(s, axis=1, keepdims=True) + w_inter * jnp.sum(q * n_prev, axis=1, keepdims=True)
        hout = num / jnp.maximum(jnp.abs(den), jnp.exp(-mj))
        y_ref[:, sl] = (jax.nn.sigmoid(o_ref[:, sl]) * hout).astype(BF16)
        m_new = mj[ch - 1:ch, :]
        b_last = b_col[ch - 1:ch, :]
        w_end = jnp.exp(b_last - b_col + i_col - m_new)
        decay = jnp.exp(b_last + m_prev - m_new)
        kw = k * w_end
        c_ref[h] = decay * c_prev + _dot_tn(kw.astype(BF16), v_bf)
        n_ref[h] = decay * n_prev + jnp.sum(kw, axis=0, keepdims=True)
        m_ref[h] = jnp.broadcast_to(m_new, (1, LANES))


def _mlstm_chunked(z0, gcol, grow, bias_col, bias_row, c0, n0, m0, nh):
    bsz, t, _ = z0.shape
    width = nh * HEAD_DIM
    ch = MLSTM_CHUNK
    zspec = lambda blk: pl.BlockSpec((None, ch, width), lambda b, c: (b, c, blk))
    state = lambda shape: pl.BlockSpec((None,) + shape, lambda b, c: (b,) + (0,) * len(shape))
    return pl.pallas_call(
        _mlstm_chunk_kernel,
        grid=(bsz, t // ch),
        in_specs=[zspec(1), zspec(2), zspec(3), zspec(4),
                  pl.BlockSpec((None, ch, 2 * nh), lambda b, c: (b, c, 0)),
                  pl.BlockSpec((None, 2 * nh, ch), lambda b, c: (b, 0, c)),
                  pl.BlockSpec((2 * nh, 1), lambda b, c: (0, 0)),
                  pl.BlockSpec((1, 2 * nh), lambda b, c: (0, 0)),
                  state((nh, HEAD_DIM, HEAD_DIM)), state((nh, 1, HEAD_DIM)), state((nh, 1, LANES))],
        out_specs=[pl.BlockSpec((None, ch, width), lambda b, c: (b, c, 0)),
                   state((nh, HEAD_DIM, HEAD_DIM)), state((nh, 1, HEAD_DIM)), state((nh, 1, LANES))],
        out_shape=[jax.ShapeDtypeStruct((bsz, t, width), BF16),
                   jax.ShapeDtypeStruct((bsz, nh, HEAD_DIM, HEAD_DIM), F32),
                   jax.ShapeDtypeStruct((bsz, nh, 1, HEAD_DIM), F32),
                   jax.ShapeDtypeStruct((bsz, nh, 1, LANES), F32)],
        compiler_params=_cparams("parallel", "arbitrary"),
        name="mlstm_chunk",
    )(z0, z0, z0, z0, gcol, grow, bias_col, bias_row, c0, n0, m0)


def _to_column(row):
    n = row.shape[1]
    eye = lax.broadcasted_iota(I32, (n, n), 0) == lax.broadcasted_iota(I32, (n, n), 1)
    return jnp.sum(jnp.where(eye, row, 0.0), axis=1, keepdims=True)


def _mlstm_step_kernel(q_ref, k_ref, v_ref, o_ref, g_ref, brow_ref, c0_ref, n0_ref, m0_ref,
                       y_ref, c_ref, n_ref, m_ref):
    nh = c_ref.shape[0]
    g = g_ref[...] + brow_ref[...]
    for h in range(nh):
        sl = slice(h * HEAD_DIM, (h + 1) * HEAD_DIM)
        q = q_ref[:, sl]
        k = k_ref[:, sl] * (HEAD_DIM ** -0.5)
        v = v_ref[:, sl]
        i_pre = g[:, h:h + 1]
        lf = jax.nn.log_sigmoid(g[:, nh + h:nh + h + 1])
        m_prev = m0_ref[h][:, 0:1]
        c_prev = c0_ref[h]
        n_prev = n0_ref[h]
        a = lf + m_prev
        mj = jnp.maximum(a, i_pre)
        w_intra = jnp.exp(i_pre - mj)
        w_inter = jnp.exp(a - mj)
        s = jnp.sum(q * k, axis=1, keepdims=True) * w_intra
        q_col = _to_column(q)
        k_col = _to_column(k)
        num = s * v + w_inter * jnp.sum(q_col * c_prev, axis=0, keepdims=True)
        den = s + w_inter * jnp.sum(q * n_prev, axis=1, keepdims=True)
        hout = num / jnp.maximum(jnp.abs(den), jnp.exp(-mj))
        y_ref[:, sl] = (jax.nn.sigmoid(o_ref[:, sl]) * hout).astype(BF16)
        w_end = jnp.exp(i_pre - mj)
        decay = jnp.exp(a - mj)
        c_ref[h] = decay * c_prev + (w_end * k_col) * v
        n_ref[h] = decay * n_prev + w_end * k
        m_ref[h] = jnp.broadcast_to(mj, (1, LANES))


def _mlstm_step(z0, g, bias_row, c0, n0, m0, nh):
    bsz = z0.shape[0]
    width = nh * HEAD_DIM
    zspec = lambda blk: pl.BlockSpec((None, 1, width), lambda b: (b, 0, blk))
    state = lambda shape: pl.BlockSpec((None,) + shape, lambda b: (b,) + (0,) * len(shape))
    return pl.pallas_call(
        _mlstm_step_kernel,
        grid=(bsz,),
        in_specs=[zspec(1), zspec(2), zspec(3), zspec(4),
                  pl.BlockSpec((None, 1, 2 * nh), lambda b: (b, 0, 0)),
                  pl.BlockSpec((1, 2 * nh), lambda b: (0, 0)),
                  state((nh, HEAD_DIM, HEAD_DIM)), state((nh, 1, HEAD_DIM)), state((nh, 1, LANES))],
        out_specs=[pl.BlockSpec((None, 1, width), lambda b: (b, 0, 0)),
                   state((nh, HEAD_DIM, HEAD_DIM)), state((nh, 1, HEAD_DIM)), state((nh, 1, LANES))],
        out_shape=[jax.ShapeDtypeStruct((bsz, 1, width), BF16),
                   jax.ShapeDtypeStruct((bsz, nh, HEAD_DIM, HEAD_DIM), F32),
                   jax.ShapeDtypeStruct((bsz, nh, 1, HEAD_DIM), F32),
                   jax.ShapeDtypeStruct((bsz, nh, 1, LANES), F32)],
        compiler_params=_cparams("parallel"),
        name="mlstm_step",
    )(z0, z0, z0, z0, g, bias_row, c0, n0, m0)


def _key_to_float(key):
    bits = key ^ ((key >> 31) & jnp.int32(0x7FFFFFFF))
    return lax.bitcast_convert_type(bits, F32)


def _kth_by_bit_search(count, shape):
    def bit_step(it, key):
        cand = key + (jnp.int32(1) << (31 - it))
        cand_f = _key_to_float(cand)
        cnt = count(lambda tile, col: tile >= cand_f)
        return jnp.where(cnt >= TOPK, cand, key)

    key = lax.fori_loop(0, 32, bit_step, jnp.full(shape, INT_MIN, I32))
    has_thr = key > INT_MIN
    return jnp.where(has_thr, _key_to_float(jnp.where(has_thr, key, 0)), -jnp.inf)


def _select_threshold(count, shape, width, j0_s, thr):
    has_thr = thr > -jnp.inf
    need = TOPK - count(lambda tile, col: tile > thr)
    n_eq = count(lambda tile, col: tile == thr)
    j0_s[...] = jnp.where(has_thr, jnp.int32(width), jnp.int32(-1))
    surplus = jnp.max(jnp.where(has_thr & (n_eq > need), 1, 0))
    nbits = max(1, (width - 1).bit_length())

    @pl.when(surplus > 0)
    def _():
        def idx_step(it, j0):
            cand = j0 | (jnp.int32(1) << (nbits - 1 - it))
            cnt = count(lambda tile, col: (tile == thr) & (col < cand))
            return jnp.where(cnt < need, cand, j0)

        j0 = lax.fori_loop(0, nbits, idx_step, jnp.zeros(shape, I32))
        j0_s[...] = jnp.where(has_thr, j0, jnp.int32(-1))

    return thr, j0_s[...]


def _network_pass(load, store, stages, groups):
    for grp in groups:
        vals = [load(i) for i in grp]
        pos = {gi: n for n, gi in enumerate(grp)}
        for size, dist in stages:
            for gi in grp:
                gl = gi ^ dist
                if gl > gi:
                    a, b = pos[gi], pos[gl]
                    hi, lo = jnp.maximum(vals[a], vals[b]), jnp.minimum(vals[a], vals[b])
                    vals[a], vals[b] = (hi, lo) if (gi & size) == 0 else (lo, hi)
        for n, gi in enumerate(grp):
            store(gi, vals[n])


_NET_GROUP = 16
_LOW_GROUPS = [[m * _NET_GROUP + t for t in range(_NET_GROUP)] for m in range(SORT_TILES // _NET_GROUP)]
_HIGH_GROUPS = [[m + (SORT_TILES // _NET_GROUP) * t for t in range(_NET_GROUP)]
                for m in range(SORT_TILES // _NET_GROUP)]


def _tile(ref, base, i):
    return ref.at[pl.ds(base + 8 * i, 8), :]


def _sort_block_desc(src, src_base, dst):
    assert SORT_TILES == 256 and _NET_GROUP == 16
    low = lambda size: [(size, d) for d in (8, 4, 2, 1) if d < size]
    first = [st for size in (2, 4, 8, 16) for st in low(size)]
    _network_pass(lambda i: _tile(src, src_base, i)[...],
                  lambda i, v: _tile(dst, 0, i).__setitem__(Ellipsis, v), first, _LOW_GROUPS)
    ld = lambda i: _tile(dst, 0, i)[...]
    st = lambda i, v: _tile(dst, 0, i).__setitem__(Ellipsis, v)
    for size in (32, 64, 128, 256):
        _network_pass(ld, st, [(size, d) for d in (128, 64, 32, 16) if d < size], _HIGH_GROUPS)
        _network_pass(ld, st, low(size), _LOW_GROUPS)


def _merge_top(run, other, tmp, shift=None):
    def ld(i):
        o = _tile(other, 0, SORT_TILES - 1 - i)[...]
        if shift is not None:
            o = pltpu.roll(o, shift, axis=0)
        return jnp.maximum(_tile(run, 0, i)[...], o)

    _network_pass(ld, lambda i, v: _tile(tmp, 0, i).__setitem__(Ellipsis, v),
                  [(SORT_TILES, d) for d in (128, 64, 32, 16)], _HIGH_GROUPS)
    _network_pass(lambda i: _tile(tmp, 0, i)[...], lambda i, v: _tile(run, 0, i).__setitem__(Ellipsis, v),
                  [(SORT_TILES, d) for d in (8, 4, 2, 1)], _LOW_GROUPS)


def _mask_bias(scores, cols, thr, j0):
    sel = (scores > thr) | ((scores == thr) & (cols <= j0))
    return jnp.where(sel, 0.0, NEG_BIG)


def _dsa_prompt_kernel(q_ref, qi_ref, wit_ref, ki_ref, k_ref, vt_ref, o_ref, sc_s, j0_s, m_s, acc_s,
                       run_s, blk_s, tmp_s):
    qb = q_ref.shape[0]
    kc_len = KEY_CHUNK
    nheads = q_ref.shape[1] // HEAD_DIM
    nkv = k_ref.shape[1] // HEAD_DIM
    rep = nheads // nkv
    i = pl.program_id(1)
    nchunks = ((i + 1) * qb + kc_len - 1) // kc_len
    qpos = i * qb + lax.broadcasted_iota(I32, (1, qb), 1)
    sub = lax.broadcasted_iota(I32, (kc_len, qb), 0)

    qi = qi_ref[...]
    wit = wit_ref[...]
    qi_all = jnp.concatenate([qi[:, h * IDX_DIM:(h + 1) * IDX_DIM] for h in range(IDX_HEADS)], axis=0)

    def score_chunk(c, _):
        off = pl.multiple_of(c * kc_len, kc_len)
        ki = ki_ref[pl.ds(off, kc_len), :]
        logits = _dot_nt(ki, qi_all)
        sc = jnp.zeros((kc_len, qb), F32)
        for h in range(IDX_HEADS):
            sc = sc + jnp.maximum(logits[:, h * qb:(h + 1) * qb], 0.0) * wit[h:h + 1, :]
        sc_s[pl.ds(off, kc_len), :] = jnp.where(sub + off <= qpos, sc, -jnp.inf)
        return 0

    lax.fori_loop(0, nchunks, score_chunk, 0)

    sort_rows = 8 * SORT_TILES
    nblocks = ((i + 1) * qb + sort_rows - 1) // sort_rows

    @pl.when(nchunks * kc_len < nblocks * sort_rows)
    def _():
        sc_s[pl.ds(pl.multiple_of(nchunks * kc_len, kc_len), kc_len), :] = jnp.full((kc_len, qb), -jnp.inf, F32)

    run_s[...] = jnp.full_like(run_s, -jnp.inf)

    def sort_block(b, _):
        _sort_block_desc(sc_s, pl.multiple_of(b * sort_rows, sort_rows), blk_s)
        _merge_top(run_s, blk_s, tmp_s)
        return 0

    lax.fori_loop(0, nblocks, sort_block, 0)
    for shift in (4, 2):
        _merge_top(run_s, run_s, tmp_s, shift)
    top = [jnp.maximum(_tile(run_s, 0, t)[...], pltpu.roll(_tile(run_s, 0, SORT_TILES - 1 - t)[...], 1, axis=0))
           for t in range(SORT_TILES)]
    while len(top) > 1:
        top = [jnp.minimum(a, b) for a, b in zip(top[0::2], top[1::2])]
    kth = top[0][0:1, :]

    def count(pred):
        def body(c, acc):
            off = pl.multiple_of(c * COUNT_CHUNK, COUNT_CHUNK)
            ind = jnp.where(pred(sc_s[pl.ds(off, COUNT_CHUNK), :], sub[:COUNT_CHUNK] + off), 1, 0)
            return acc + jnp.sum(ind.reshape(COUNT_CHUNK // 8, 8, qb), axis=0)

        ncount = ((i + 1) * qb + COUNT_CHUNK - 1) // COUNT_CHUNK
        acc = lax.fori_loop(0, ncount, body, jnp.zeros((8, qb), I32))
        return jnp.sum(acc, axis=0, keepdims=True)

    thr, j0 = _select_threshold(count, (1, qb), sc_s.shape[0], j0_s, kth)

    m_s[...] = jnp.full_like(m_s, NEG_BIG)
    acc_s[...] = jnp.zeros_like(acc_s)
    q = q_ref[...]
    q_g = [jnp.concatenate([q[:, (g * rep + r) * HEAD_DIM:(g * rep + r + 1) * HEAD_DIM]
                            for r in range(rep)], axis=0) for g in range(nkv)]
    ones_rows = jnp.ones((acc_s.shape[1] - HEAD_DIM, kc_len), BF16)

    def attend_chunk(c, _):
        off = pl.multiple_of(c * kc_len, kc_len)
        bias = _mask_bias(sc_s[pl.ds(off, kc_len), :], sub + off, thr, j0)
        bias = jnp.concatenate([bias] * rep, axis=1)
        for g in range(nkv):
            kc = k_ref[pl.ds(off, kc_len), g * HEAD_DIM:(g + 1) * HEAD_DIM]
            att = _dot_nt(kc, q_g[g]) + bias
            m_old = m_s[g]
            m_new = jnp.maximum(m_old, jnp.max(att, axis=0, keepdims=True))
            p = jnp.exp2(att - m_new).astype(BF16)
            vt = jnp.concatenate([vt_ref[g * HEAD_DIM:(g + 1) * HEAD_DIM, pl.ds(off, kc_len)], ones_rows],
                                 axis=0)
            acc_s[g] = jnp.exp2(m_old - m_new) * acc_s[g] + _dot(vt, p)
            m_s[g] = m_new
        return 0

    lax.fori_loop(0, nchunks, attend_chunk, 0)

    for g in range(nkv):
        acc = acc_s[g]
        out = acc[:HEAD_DIM, :] / acc[HEAD_DIM:HEAD_DIM + 1, :]
        for r in range(rep):
            hd = g * rep + r
            o_ref[:, hd * HEAD_DIM:(hd + 1) * HEAD_DIM] = out[:, r * qb:(r + 1) * qb].T.astype(BF16)


def _dsa_prompt(q, qi, wit, kidx, kbf, vt):
    bsz, t, width = q.shape
    kvw = kbf.shape[2]
    nkv = kvw // HEAD_DIM
    rep = width // HEAD_DIM // nkv
    qb = Q_BLOCK
    nq = t // qb
    assert t % (8 * SORT_TILES) == 0 and 8 * SORT_TILES == 2 * KEY_CHUNK and TOPK == SORT_TILES
    ones_rows = 16
    return pl.pallas_call(
        _dsa_prompt_kernel,
        grid=(bsz, nq),
        in_specs=[pl.BlockSpec((None, qb, width), lambda b, i: (b, i, 0)),
                  pl.BlockSpec((None, qb, qi.shape[2]), lambda b, i: (b, i, 0)),
                  pl.BlockSpec((IDX_HEADS, qb), lambda b, i: (0, b * nq + i)),
                  pl.BlockSpec((None, t, IDX_DIM), lambda b, i: (b, 0, 0)),
                  pl.BlockSpec((None, t, kvw), lambda b, i: (b, 0, 0)),
                  pl.BlockSpec((kvw, t), lambda b, i: (0, b))],
        out_specs=pl.BlockSpec((None, qb, width), lambda b, i: (b, i, 0)),
        out_shape=jax.ShapeDtypeStruct((bsz, t, width), BF16),
        scratch_shapes=[pltpu.VMEM((t, qb), F32), pltpu.VMEM((1, qb), I32),
                        pltpu.VMEM((nkv, 1, rep * qb), F32),
                        pltpu.VMEM((nkv, HEAD_DIM + ones_rows, rep * qb), F32),
                        pltpu.VMEM((8 * SORT_TILES, qb), F32), pltpu.VMEM((8 * SORT_TILES, qb), F32),
                        pltpu.VMEM((8 * SORT_TILES, qb), F32)],
        compiler_params=_cparams("parallel", "arbitrary"),
        name="dsa_prompt",
    )(q, qi, wit, kidx, kbf, vt)


def _dsa_score_kernel(pt_ref, qi_ref, wi_ref, knew_ref, *rest):
    pages, (keys_ref, newkey_ref) = rest[:PAGES_PER_STEP], rest[PAGES_PER_STEP:]
    qi = qi_ref[...]
    wi = wi_ref[...]
    ki = jnp.concatenate([page_ref[...].astype(BF16) for page_ref in pages], axis=0)
    logits = _dot_nt(qi.astype(BF16), ki)
    keys_ref[...] = jnp.sum(jnp.maximum(logits, 0.0) * wi, axis=0, keepdims=True)
    logit_new = jnp.sum(qi * knew_ref[...], axis=1, keepdims=True)
    sc_new = jnp.sum(jnp.maximum(logit_new, 0.0) * wi, axis=0, keepdims=True)
    newkey_ref[...] = jnp.broadcast_to(sc_new, (1, LANES))


def _dsa_score(page_table, qi, wi, kidx_new, cache_kidx):
    bsz, npages = page_table.shape
    steps = npages // PAGES_PER_STEP
    page_spec = lambda p: pl.BlockSpec(
        (None, PAGE, IDX_DIM), lambda b, j, pt: (pt[b, j * PAGES_PER_STEP + p], 0, 0))
    per_seq = lambda shape: pl.BlockSpec((None,) + shape, lambda b, j, pt: (b, 0, 0))
    return pl.pallas_call(
        _dsa_score_kernel,
        grid_spec=pltpu.PrefetchScalarGridSpec(
            num_scalar_prefetch=1,
            grid=(bsz, steps),
            in_specs=[per_seq((IDX_HEADS, IDX_DIM)), per_seq((IDX_HEADS, 1)), per_seq((1, IDX_DIM))]
                     + [page_spec(p) for p in range(PAGES_PER_STEP)],
            out_specs=[pl.BlockSpec((None, 1, PAGES_PER_STEP * PAGE), lambda b, j, pt: (b, 0, j)),
                       per_seq((1, LANES))]),
        out_shape=[jax.ShapeDtypeStruct((bsz, 1, npages * PAGE), F32),
                   jax.ShapeDtypeStruct((bsz, 1, LANES), F32)],
        compiler_params=_cparams("parallel", "arbitrary"),
        name="dsa_decode_score",
    )(page_table, qi, wi, kidx_new, *([cache_kidx] * PAGES_PER_STEP))


def _dsa_select_kernel(keys_ref, thr_ref, j0_ref, j0_s):
    rows, width = keys_ref.shape
    lane = lax.broadcasted_iota(I32, (rows, LANES), 1)

    def count(pred):
        def body(t, acc):
            off = pl.multiple_of(t * LANES, LANES)
            return acc + jnp.where(pred(keys_ref[:, pl.ds(off, LANES)], lane + off), 1, 0)

        acc = lax.fori_loop(0, width // LANES, body, jnp.zeros((rows, LANES), I32))
        return jnp.sum(acc, axis=1, keepdims=True)

    thr, j0 = _select_threshold(count, (rows, 1), width, j0_s, _kth_by_bit_search(count, (rows, 1)))
    thr_ref[...] = jnp.broadcast_to(thr, thr_ref.shape)
    j0_ref[...] = jnp.broadcast_to(j0, j0_ref.shape)


def _dsa_select(keys):
    rows, width = keys.shape
    const = lambda i: (0, 0)
    return pl.pallas_call(
        _dsa_select_kernel,
        grid=(1,),
        in_specs=[pl.BlockSpec((rows, width), const)],
        out_specs=[pl.BlockSpec((rows, LANES), const), pl.BlockSpec((rows, LANES), const)],
        out_shape=[jax.ShapeDtypeStruct((rows, LANES), F32), jax.ShapeDtypeStruct((rows, LANES), I32)],
        scratch_shapes=[pltpu.VMEM((rows, 1), I32)],
        compiler_params=_cparams("arbitrary"),
        name="dsa_decode_select",
    )(keys)


def _dsa_decode_kernel(pt_ref, q_ref, keys_ref, tail_ref, thr_ref, j0_ref, knew_ref, vnew_ref, *rest,
                       n_past, nkv):
    j = pl.program_id(1)
    nheads = q_ref.shape[0]
    rep = nheads // nkv
    kpages, vpages = rest[:PAGES_PER_STEP], rest[PAGES_PER_STEP:2 * PAGES_PER_STEP]
    o_ref, m_s, l_s, acc_s = rest[2 * PAGES_PER_STEP:]

    @pl.when(j == 0)
    def _():
        m_s[...] = jnp.full_like(m_s, NEG_BIG)
        l_s[...] = jnp.zeros_like(l_s)
        acc_s[...] = jnp.zeros_like(acc_s)

    q = q_ref[...]
    thr = thr_ref[:, 0:1]
    j0 = j0_ref[:, 0:1]
    width = PAGES_PER_STEP * PAGE * nkv
    row = lax.broadcasted_iota(I32, (1, width), 1) + j * width
    head_kv = lax.broadcasted_iota(I32, (nheads, 1), 0) // rep

    def update(att, value_fn):
        m_old = m_s[...]
        m_new = jnp.maximum(m_old, jnp.max(att, axis=1, keepdims=True))
        alpha = jnp.exp2(m_old - m_new)
        p = jnp.exp2(att - m_new)
        l_s[...] = alpha * l_s[...] + jnp.sum(p, axis=1, keepdims=True)
        acc_s[...] = alpha * acc_s[...] + value_fn(p)
        m_s[...] = m_new

    kcat = jnp.concatenate([r[...].astype(BF16) for r in kpages], axis=0)
    vcat = jnp.concatenate([r[...].astype(BF16) for r in vpages], axis=0)
    bias = _mask_bias(keys_ref[...], row // nkv, thr, j0)
    bias = jnp.where(row % nkv == head_kv, bias, NEG_BIG)
    update(_dot_nt(q.astype(BF16), kcat) + bias, lambda pr: _dot(pr.astype(BF16), vcat))

    @pl.when(j == pl.num_programs(1) - 1)
    def _():
        k_new = jnp.zeros_like(q)
        v_new = jnp.zeros_like(q)
        for g in range(nkv):
            k_new = jnp.where(head_kv == g, knew_ref[g:g + 1, :], k_new)
            v_new = jnp.where(head_kv == g, vnew_ref[g:g + 1, :], v_new)
        bias_new = _mask_bias(tail_ref[:, 0:1], jnp.int32(n_past), thr, j0)
        att_new = jnp.sum(q * k_new, axis=1, keepdims=True) + bias_new
        update(att_new, lambda pr: pr * v_new)
        o_ref[...] = (acc_s[...] / l_s[...]).astype(BF16)


def _dsa_decode(page_table, q, keys, tail, thr, j0, k_new, v_new, cache_k, cache_v):
    bsz, npages = page_table.shape
    nheads = q.shape[1]
    nkv = k_new.shape[1]
    steps = npages // PAGES_PER_STEP
    page_spec = lambda p: pl.BlockSpec(
        (PAGE * nkv, HEAD_DIM), lambda b, j, pt: (pt[b, j * PAGES_PER_STEP + p], 0))
    per_seq = lambda shape: pl.BlockSpec((None,) + shape, lambda b, j, pt: (b, 0, 0))
    keys_spec = pl.BlockSpec((None, 1, PAGES_PER_STEP * PAGE * nkv), lambda b, j, pt: (b, 0, j))
    return pl.pallas_call(
        functools.partial(_dsa_decode_kernel, n_past=npages * PAGE, nkv=nkv),
        grid_spec=pltpu.PrefetchScalarGridSpec(
            num_scalar_prefetch=1,
            grid=(bsz, steps),
            in_specs=[per_seq((nheads, HEAD_DIM)), keys_spec, per_seq((1, LANES)), per_seq((1, LANES)),
                      per_seq((1, LANES)), per_seq((nkv, HEAD_DIM)), per_seq((nkv, HEAD_DIM))]
                     + [page_spec(p) for p in range(PAGES_PER_STEP)] * 2,
            out_specs=per_seq((nheads, HEAD_DIM)),
            scratch_shapes=[pltpu.VMEM((nheads, 1), F32), pltpu.VMEM((nheads, 1), F32),
                            pltpu.VMEM((nheads, HEAD_DIM), F32)]),
        out_shape=jax.ShapeDtypeStruct((bsz, nheads, HEAD_DIM), BF16),
        compiler_params=_cparams("parallel", "arbitrary"),
        name="dsa_decode_attend",
    )(page_table, q, keys, tail, thr, j0, k_new, v_new,
      *([cache_k] * PAGES_PER_STEP), *([cache_v] * PAGES_PER_STEP))


def _pad_cols(w, width):
    return jnp.pad(w, ((0, 0), (0, width - w.shape[1])))


def _block_diag(blocks):
    g, r, c = blocks.shape
    eye = jnp.eye(g, dtype=blocks.dtype)
    return (blocks[:, :, None, :] * eye[:, None, :, None]).reshape(g * r, g * c)


def _s5_params(lam_re, lam_im, log_dt, b_re, b_im, c_re, c_im, d_skip, w_glu, b_glu):
    dt = jnp.exp(log_dt)[:, None]
    mag = jnp.exp(lam_re * dt)
    bar_re = mag * jnp.cos(lam_im * dt)
    bar_im = mag * jnp.sin(lam_im * dt)
    inv = 1.0 / (lam_re * lam_re + lam_im * lam_im)
    coef_re = (((bar_re - 1.0) * lam_re + bar_im * lam_im) * inv)[..., None]
    coef_im = ((bar_im * lam_re - (bar_re - 1.0) * lam_im) * inv)[..., None]
    bb_re = coef_re * b_re - coef_im * b_im
    bb_im = coef_re * b_im + coef_im * b_re
    lam_rows = jnp.stack([bar_re.reshape(-1), bar_im.reshape(-1)])
    to_in = lambda z: _block_diag(jnp.swapaxes(z, 1, 2))
    wb = jnp.concatenate([to_in(bb_re), to_in(bb_im)], axis=1).astype(BF16)
    to_out = lambda z: _block_diag(jnp.swapaxes(z, 1, 2))
    wc = jnp.concatenate([to_out(c_re), to_out(-c_im)], axis=0).astype(BF16)
    return (lam_rows, wb, wc, d_skip[None, :], w_glu.astype(BF16), b_glu[None, :])


def kernel(x_prompt, x_sample, state_s5_re, state_s5_im, state_mlstm_c, state_mlstm_n, state_mlstm_m,
           cache_k, cache_v, cache_kidx, page_table, norm_mix, norm_mlp, norm_final, w_in0, s5_lam_re,
           s5_lam_im, s5_log_dt, s5_b_re, s5_b_im, s5_c_re, s5_c_im, s5_d, w_glu, b_glu, b_igate,
           b_fgate, w_out0, w_in1, w_out1, w_up, w_down):
    bp, tp, d = x_prompt.shape
    db, ts, _ = x_sample.shape
    assert ts == 1, "the decode path handles one new token per sequence"
    s5_groups, s5_state = s5_lam_re.shape
    s5_width = s5_groups * S5_GROUP
    nstate = s5_groups * s5_state
    nh = b_igate.shape[0]
    ml_width = nh * HEAD_DIM
    assert s5_width == ml_width == 512 and d == 1024
    n_past = page_table.shape[1] * PAGE
    kvw = cache_k.shape[2] * cache_k.shape[3]
    nheads = w_out1.shape[0] // HEAD_DIM

    gate_cols = s5_width + 4 * ml_width
    w0 = jnp.concatenate([w_in0[:, :gate_cols], _pad_cols(w_in0[:, gate_cols:], LANES)], axis=1).astype(BF16)
    s5p = _s5_params(s5_lam_re, s5_lam_im, s5_log_dt, s5_b_re, s5_b_im, s5_c_re, s5_c_im, s5_d, w_glu, b_glu)
    gate_bias = jnp.concatenate([b_igate, b_fgate])
    bias_row, bias_col = gate_bias[None, :], gate_bias[:, None]
    ki0 = 1024 + 2 * kvw + IDX_HEADS * IDX_DIM
    w1 = jnp.concatenate([w_in1[:, :ki0], _pad_cols(w_in1[:, ki0:ki0 + IDX_DIM], LANES)], axis=1).astype(BF16)
    wvt = w_in1[:, 1024 + kvw:1024 + 2 * kvw].T.astype(BF16)
    wwit = w_in1[:, ki0 + IDX_DIM:].T.astype(BF16)
    wo0, wo1 = w_out0.astype(BF16), w_out1.astype(BF16)
    wup, wdn = w_up.astype(BF16), w_down.astype(BF16)
    g_mix, g_mlp, g_fin = norm_mix[:, None, :], norm_mlp[:, None, :], norm_final[None, :]

    def trunk(x2d, bsz, t, s5_h0, c0, n0, m0, attend):
        n = bsz * t
        z0 = _proj0(x2d, g_mix[0], w0)
        gates = z0[:, gate_cols:gate_cols + 2 * nh]
        m0b = jnp.broadcast_to(m0[:, :, None, None], (bsz, nh, 1, LANES))
        n0r = n0[:, :, None, :]
        if t == 1:
            y_s5, h_t = _s5_step(z0, s5_h0, s5p, s5_width)
            y_ml, c_t, n_t, m_t = _mlstm_step(z0.reshape(bsz, 1, -1), gates.reshape(bsz, 1, 2 * nh),
                                              bias_row, c0, n0r, m0b, nh)
        else:
            y_s5, h_t = _s5_seq(z0.reshape(bsz, t, -1), s5_h0.reshape(bsz, 1, -1), s5p, s5_width)
            g3 = gates.reshape(bsz, t, 2 * nh)
            y_ml, c_t, n_t, m_t = _mlstm_chunked(z0.reshape(bsz, t, -1), g3, jnp.swapaxes(g3, 1, 2),
                                                 bias_col, bias_row, c0, n0r, m0b, nh)
        h_t = h_t.reshape(bsz, 2, s5_groups, s5_state)
        states = (h_t[:, 0], h_t[:, 1], c_t, n_t[:, :, 0, :], m_t[:, :, 0, 0])
        h1 = _post(x2d, y_s5.reshape(n, s5_width), (y_ml.reshape(n, ml_width), 0), wo0, g_mlp[0],
                   wup[0], wdn[0], g_fin, final_norm=False)
        q, k, v, kbf, qi, kidx, kidxbf, vt, wit = _proj1(h1, g_mix[1], w1, wvt, wwit)
        o = attend(q, k, v, kbf, qi, kidx, kidxbf, vt, wit)
        y = _post(h1, o, (o, 1), wo1, g_mlp[1], wup[1], wdn[1], g_fin, final_norm=True)
        rows = (k.reshape(bsz, t, -1, HEAD_DIM), v.reshape(bsz, t, -1, HEAD_DIM), kidx.reshape(bsz, t, IDX_DIM))
        return y.reshape(bsz, t, d), states, rows

    def attend_prompt(q, k, v, kbf, qi, kidx, kidxbf, vt, wit):
        r3 = lambda z: z.reshape(bp, tp, -1)
        return _dsa_prompt(r3(q), r3(qi), wit, r3(kidxbf), r3(kbf), vt).reshape(bp * tp, -1)

    def attend_decode(q, k, v, kbf, qi, kidx, kidxbf, vt, wit):
        qi3 = qi.astype(F32).reshape(db, IDX_HEADS, IDX_DIM)
        wi3 = wit.T.reshape(db, IDX_HEADS, 1)
        keys, newkey = _dsa_score(page_table, qi3, wi3, kidx.reshape(db, 1, IDX_DIM), cache_kidx)
        tail = jnp.where(lax.broadcasted_iota(I32, (db, LANES), 1) == 0, newkey[:, 0, :], -jnp.inf)
        keys = jnp.concatenate([keys[:, 0, :], tail], axis=1)
        thr, j0 = _dsa_select(keys)
        nkv = kvw // HEAD_DIM
        keys_rows = jnp.repeat(keys[:, :n_past], nkv, axis=1)[:, None, :]
        o = _dsa_decode(page_table, q.astype(F32).reshape(db, nheads, HEAD_DIM), keys_rows,
                        keys[:, None, n_past:], thr[:, None, :], j0[:, None, :],
                        k.reshape(db, nkv, HEAD_DIM), v.reshape(db, nkv, HEAD_DIM),
                        cache_k.reshape(-1, HEAD_DIM), cache_v.reshape(-1, HEAD_DIM))
        return o.reshape(db, nheads * HEAD_DIM)

    zeros = lambda *shape: jnp.zeros(shape, F32)
    y_p, st_p, rows_p = trunk(x_prompt.reshape(bp * tp, d), bp, tp, zeros(bp, 2 * nstate),
                              zeros(bp, nh, HEAD_DIM, HEAD_DIM), zeros(bp, nh, HEAD_DIM), zeros(bp, nh),
                              attend_prompt)
    s5_h0 = jnp.concatenate([state_s5_re.reshape(db, nstate), state_s5_im.reshape(db, nstate)], axis=1)
    y_s, st_s, rows_s = trunk(x_sample.reshape(db, d), db, 1, s5_h0, state_mlstm_c, state_mlstm_n,
                              state_mlstm_m, attend_decode)
    return (y_p, y_s) + st_p + rows_p + st_s + rows_s
```

```python
import functools
import math

import jax
import jax.numpy as jnp
from jax import lax
from jax.experimental import pallas as pl
from jax.experimental.pallas import tpu as pltpu

F32, BF16, I32 = jnp.float32, jnp.bfloat16, jnp.int32

EPS = 1e-6
LANES = 128
PAGE = 128
S5_GROUP = 16
S5_STATE = 64
HEAD_DIM = 128
IDX_DIM = 64
IDX_HEADS = 8
TOPK = 256
Q_BLOCK = 128
KEY_CHUNK = 1024
COUNT_CHUNK = 512
SORT_TILES = 256
MLSTM_CHUNK = 128
S5_CHUNK = 256
ROW_TILE = 512
FF_CHUNK = 4096
PAGES_PER_STEP = 32
INT_MIN = -2 ** 31
NEG_BIG = -1e30
VMEM_LIMIT = 56 * 1024 * 1024


def _cparams(*sem):
    return pltpu.CompilerParams(dimension_semantics=sem, vmem_limit_bytes=VMEM_LIMIT)


def _rms(x, g):
    return x * lax.rsqrt(jnp.mean(x * x, axis=-1, keepdims=True) + EPS) * g


def _dot(a, b):
    return jnp.dot(a, b, preferred_element_type=F32)


def _dot_nt(a, b):
    return lax.dot_general(a, b, (((1,), (1,)), ((), ())), preferred_element_type=F32)


def _dot_tn(a, b):
    return lax.dot_general(a, b, (((0,), (0,)), ((), ())), preferred_element_type=F32)


def _proj0_kernel(x_ref, g_ref, w_ref, o_ref):
    xn = _rms(x_ref[...], g_ref[...]).astype(BF16)
    o_ref[...] = _dot(xn, w_ref[...])


def _proj0(x, g, w):
    n, d = x.shape
    tm = min(ROW_TILE, n)
    wtot = w.shape[1]
    return pl.pallas_call(
        _proj0_kernel,
        grid=(n // tm,),
        in_specs=[pl.BlockSpec((tm, d), lambda i: (i, 0)),
                  pl.BlockSpec((1, d), lambda i: (0, 0)),
                  pl.BlockSpec((d, wtot), lambda i: (0, 0))],
        out_specs=pl.BlockSpec((tm, wtot), lambda i: (i, 0)),
        out_shape=jax.ShapeDtypeStruct((n, wtot), F32),
        compiler_params=_cparams("parallel"),
        name="proj0",
    )(x, g, w)


_Q1, _K1, _V1, _QI1, _KI1, _END1 = 0, 1024, 1280, 1536, 2048, 2176
LOG2E = 1.4426950408889634


def _proj1_kernel(x_ref, g_ref, w_ref, wvt_ref, wwit_ref, q_ref, k_ref, v_ref, kbf_ref, qi_ref,
                  kidx_ref, kidxbf_ref, vt_ref, wit_ref):
    xn = _rms(x_ref[...], g_ref[...]).astype(BF16)
    z = _dot(xn, w_ref[...])
    q_ref[...] = (z[:, _Q1:_K1] * (HEAD_DIM ** -0.5 * LOG2E)).astype(BF16)
    k = z[:, _K1:_V1]
    k_ref[...] = k
    v_ref[...] = z[:, _V1:_QI1]
    kbf_ref[...] = k.astype(BF16)
    qi_ref[...] = z[:, _QI1:_KI1].astype(BF16)
    kidx = z[:, _KI1:_KI1 + IDX_DIM]
    kidx_ref[...] = kidx
    kidxbf_ref[...] = kidx.astype(BF16)
    vt_ref[...] = _dot_nt(wvt_ref[...], xn).astype(BF16)
    wit_ref[...] = _dot_nt(wwit_ref[...], xn) * ((IDX_DIM ** -0.5) * (IDX_HEADS ** -0.5))


def _proj1(x, g, w, wvt, wwit):
    n, d = x.shape
    tm = min(ROW_TILE, n)
    kvw = wvt.shape[0]
    row = lambda width: pl.BlockSpec((tm, width), lambda i: (i, 0))
    col = lambda height: pl.BlockSpec((height, tm), lambda i: (0, i))
    full = lambda a: pl.BlockSpec(a.shape, lambda i: (0, 0))
    shp = lambda width, dt: jax.ShapeDtypeStruct((n, width), dt)
    return pl.pallas_call(
        _proj1_kernel,
        grid=(n // tm,),
        in_specs=[row(d), pl.BlockSpec((1, d), lambda i: (0, 0)), full(w), full(wvt), full(wwit)],
        out_specs=[row(1024), row(kvw), row(kvw), row(kvw), row(512), row(IDX_DIM), row(IDX_DIM),
                   col(kvw), col(IDX_HEADS)],
        out_shape=[shp(1024, BF16), shp(kvw, F32), shp(kvw, F32), shp(kvw, BF16), shp(512, BF16),
                   shp(IDX_DIM, F32), shp(IDX_DIM, BF16),
                   jax.ShapeDtypeStruct((kvw, n), BF16), jax.ShapeDtypeStruct((IDX_HEADS, n), F32)],
        compiler_params=_cparams("parallel"),
        name="proj1",
    )(x, g, w, wvt, wwit)


def _post_kernel(h_ref, ya_ref, yb_ref, wo_ref, g_ref, wup_ref, wdn_ref, gf_ref, out_ref,
                 h1_s, xn_s, acc_s, *, final_norm):
    j = pl.program_id(1)
    half = ya_ref.shape[1]

    @pl.when(j == 0)
    def _():
        h1 = h_ref[...] + _dot(ya_ref[...], wo_ref[:half, :]) + _dot(yb_ref[...], wo_ref[half:, :])
        h1_s[...] = h1
        xn_s[...] = _rms(h1, g_ref[...]).astype(BF16)
        acc_s[...] = jnp.zeros_like(acc_s)

    r = jnp.maximum(_dot(xn_s[...], wup_ref[...]), 0.0)
    acc_s[...] += _dot((r * r).astype(BF16), wdn_ref[...])

    @pl.when(j == pl.num_programs(1) - 1)
    def _():
        o = h1_s[...] + acc_s[...]
        if final_norm:
            o = _rms(o, gf_ref[...])
        out_ref[...] = o


def _post(h, ya, yb_spec_arg, wo, g, wup, wdn, gf, *, final_norm):
    n, d = h.shape
    tm = min(ROW_TILE, n)
    yb, yb_col = yb_spec_arg
    half = d // 2
    dff = wup.shape[1]
    once = pl.Buffered(1) if dff == FF_CHUNK else None
    return pl.pallas_call(
        functools.partial(_post_kernel, final_norm=final_norm),
        grid=(n // tm, dff // FF_CHUNK),
        in_specs=[pl.BlockSpec((tm, d), lambda i, j: (i, 0)),
                  pl.BlockSpec((tm, half), lambda i, j: (i, 0)),
                  pl.BlockSpec((tm, half), lambda i, j: (i, yb_col)),
                  pl.BlockSpec((d, d), lambda i, j: (0, 0), pipeline_mode=once),
                  pl.BlockSpec((1, d), lambda i, j: (0, 0)),
                  pl.BlockSpec((d, FF_CHUNK), lambda i, j: (0, j), pipeline_mode=once),
                  pl.BlockSpec((FF_CHUNK, d), lambda i, j: (j, 0), pipeline_mode=once),
                  pl.BlockSpec((1, d), lambda i, j: (0, 0))],
        out_specs=pl.BlockSpec((tm, d), lambda i, j: (i, 0)),
        out_shape=jax.ShapeDtypeStruct((n, d), F32),
        scratch_shapes=[pltpu.VMEM((tm, d), F32), pltpu.VMEM((tm, d), BF16), pltpu.VMEM((tm, d), F32)],
        compiler_params=_cparams("parallel", "arbitrary"),
        name="post_final" if final_norm else "post",
    )(h, ya, yb, wo, g, wup, wdn, gf)


S5_SPLIT = 2


def _s5_input(u, wb_ref):
    width, nstate = wb_ref.shape[0], wb_ref.shape[1] // 2
    wi, ns = width // S5_SPLIT, nstate // S5_SPLIT
    u_bf = u.astype(BF16)
    part = lambda base: jnp.concatenate(
        [_dot(u_bf[:, r * wi:(r + 1) * wi], wb_ref[r * wi:(r + 1) * wi, base + r * ns:base + (r + 1) * ns])
         for r in range(S5_SPLIT)], axis=1)
    return part(0), part(nstate)


def _s5_output(hre, him, u, wc_ref, d_ref, wg_ref, bg_ref):
    nstate, width = hre.shape[1], wc_ref.shape[1]
    wi, ns = width // S5_SPLIT, nstate // S5_SPLIT
    hre_bf, him_bf = hre.astype(BF16), him.astype(BF16)
    y = jnp.concatenate(
        [_dot(hre_bf[:, r * ns:(r + 1) * ns], wc_ref[r * ns:(r + 1) * ns, r * wi:(r + 1) * wi])
         + _dot(him_bf[:, r * ns:(r + 1) * ns], wc_ref[nstate + r * ns:nstate + (r + 1) * ns, r * wi:(r + 1) * wi])
         for r in range(S5_SPLIT)], axis=1)
    y = jax.nn.gelu(y + d_ref[...] * u)
    gate = jax.nn.sigmoid(_dot(y.astype(BF16), wg_ref[...]) + bg_ref[...])
    return (y * gate).astype(BF16)


def _s5_seq_kernel(u_ref, h0_ref, lam_ref, wb_ref, wc_ref, d_ref, wg_ref, bg_ref, y_ref, ht_ref,
                   hre_s, him_s, carry_s):
    bsz, _, half = hre_s.shape

    @pl.when(pl.program_id(0) == 0)
    def _():
        carry_s[...] = h0_ref[...]

    for b in range(bsz):
        hre_s[b], him_s[b] = _s5_input(u_ref[b], wb_ref)
    a_re = lam_ref[0:1, :]
    a_im = lam_ref[1:2, :]

    def step(t, carry):
        out = []
        for b in range(bsz):
            h_re, h_im = carry[2 * b], carry[2 * b + 1]
            n_re = a_re * h_re - a_im * h_im + hre_s[b, pl.ds(t, 1), :]
            n_im = a_re * h_im + a_im * h_re + him_s[b, pl.ds(t, 1), :]
            hre_s[b, pl.ds(t, 1), :] = n_re
            him_s[b, pl.ds(t, 1), :] = n_im
            out += [n_re, n_im]
        return tuple(out)

    init = tuple(carry_s[b][:, lo:lo + half] for b in range(bsz) for lo in (0, half))
    final = lax.fori_loop(0, hre_s.shape[1], step, init, unroll=8)
    for b in range(bsz):
        carry_s[b, :, :half] = final[2 * b]
        carry_s[b, :, half:] = final[2 * b + 1]
        y_ref[b] = _s5_output(hre_s[b], him_s[b], u_ref[b], wc_ref, d_ref, wg_ref, bg_ref)
    ht_ref[...] = carry_s[...]


def _s5_step_kernel(u_ref, h0_ref, lam_ref, wb_ref, wc_ref, d_ref, wg_ref, bg_ref, y_ref, ht_ref):
    half = lam_ref.shape[1]
    u = u_ref[...]
    bu_re, bu_im = _s5_input(u, wb_ref)
    a_re = lam_ref[0:1, :]
    a_im = lam_ref[1:2, :]
    h_re = h0_ref[:, :half]
    h_im = h0_ref[:, half:]
    n_re = a_re * h_re - a_im * h_im + bu_re
    n_im = a_re * h_im + a_im * h_re + bu_im
    ht_ref[:, :half] = n_re
    ht_ref[:, half:] = n_im
    y_ref[...] = _s5_output(n_re, n_im, u, wc_ref, d_ref, wg_ref, bg_ref)


def _s5_param_specs(width, nstate, imap):
    return [pl.BlockSpec((2, nstate), imap),
            pl.BlockSpec((width, 2 * nstate), imap),
            pl.BlockSpec((2 * nstate, width), imap),
            pl.BlockSpec((1, width), imap),
            pl.BlockSpec((width, width), imap),
            pl.BlockSpec((1, width), imap)]


def _s5_seq(z0, h0, params, width):
    bsz, t, _ = z0.shape
    nstate = params[0].shape[1]
    ts = min(S5_CHUNK, t)
    const = lambda c: (0, 0)
    return pl.pallas_call(
        _s5_seq_kernel,
        grid=(t // ts,),
        in_specs=[pl.BlockSpec((bsz, ts, width), lambda c: (0, c, 0)),
                  pl.BlockSpec((bsz, 1, 2 * nstate), lambda c: (0, 0, 0))]
                 + _s5_param_specs(width, nstate, const),
        out_specs=[pl.BlockSpec((bsz, ts, width), lambda c: (0, c, 0)),
                   pl.BlockSpec((bsz, 1, 2 * nstate), lambda c: (0, 0, 0))],
        out_shape=[jax.ShapeDtypeStruct((bsz, t, width), BF16),
                   jax.ShapeDtypeStruct((bsz, 1, 2 * nstate), F32)],
        scratch_shapes=[pltpu.VMEM((bsz, ts, nstate), F32), pltpu.VMEM((bsz, ts, nstate), F32),
                        pltpu.VMEM((bsz, 1, 2 * nstate), F32)],
        compiler_params=_cparams("arbitrary"),
        name="s5_scan",
    )(z0, h0, *params)


def _s5_step(z0, h0, params, width):
    rows = z0.shape[0]
    nstate = params[0].shape[1]
    const = lambda i: (0, 0)
    return pl.pallas_call(
        _s5_step_kernel,
        grid=(1,),
        in_specs=[pl.BlockSpec((rows, width), const), pl.BlockSpec((rows, 2 * nstate), const)]
                 + _s5_param_specs(width, nstate, const),
        out_specs=[pl.BlockSpec((rows, width), const), pl.BlockSpec((rows, 2 * nstate), const)],
        out_shape=[jax.ShapeDtypeStruct((rows, width), BF16),
                   jax.ShapeDtypeStruct((rows, 2 * nstate), F32)],
        compiler_params=_cparams("arbitrary"),
        name="s5_step",
    )(z0, h0, *params)


def _mlstm_chunk_kernel(q_ref, k_ref, v_ref, o_ref, gcol_ref, grow_ref, bcol_ref, brow_ref,
                        c0_ref, n0_ref, m0_ref, y_ref, c_ref, n_ref, m_ref):
    nh = c_ref.shape[0]
    ch = q_ref.shape[0]

    @pl.when(pl.program_id(1) == 0)
    def _():
        c_ref[...] = c0_ref[...]
        n_ref[...] = n0_ref[...]
        m_ref[...] = m0_ref[...]

    gcol = gcol_ref[...] + brow_ref[...]
    grow = grow_ref[...] + bcol_ref[...]
    t_idx = lax.broadcasted_iota(I32, (ch, ch), 0)
    s_idx = lax.broadcasted_iota(I32, (ch, ch), 1)
    causal = t_idx >= s_idx
    for h in range(nh):
        sl = slice(h * HEAD_DIM, (h + 1) * HEAD_DIM)
        q = q_ref[:, sl]
        k = k_ref[:, sl] * (HEAD_DIM ** -0.5)
        v = v_ref[:, sl]
        q_bf, k_bf, v_bf = q.astype(BF16), k.astype(BF16), v.astype(BF16)
        i_col = gcol[:, h:h + 1]
        i_row = grow[h:h + 1, :]
        lf_col = jax.nn.log_sigmoid(gcol[:, nh + h:nh + h + 1])
        lf_row = jax.nn.log_sigmoid(grow[nh + h:nh + h + 1, :])
        b_col = jnp.sum(jnp.where(causal, lf_row, 0.0), axis=1, keepdims=True)
        b_row = jnp.sum(jnp.where(causal, 0.0, lf_col) , axis=0, keepdims=True)
        b_row = b_row + lf_row
        m_prev = m_ref[h][:, 0:1]
        dmat = jnp.where(causal, b_col - b_row + i_row, -jnp.inf)
        a_col = b_col + m_prev
        mj = jnp.maximum(a_col, jnp.max(dmat, axis=1, keepdims=True))
        w_intra = jnp.exp(dmat - mj)
        w_inter = jnp.exp(a_col - mj)
        s = _dot_nt(q_bf, k_bf) * w_intra
        c_prev = c_ref[h]
        n_prev = n_ref[h]
        num = _dot(s.astype(BF16), v_bf) + w_inter * _dot(q_bf, c_prev.astype(BF16))
        den = jnp.sum(s, axis=1, keepdims=True) + w_inter * jnp.sum(q * n_prev, axis=1, keepdims=True)
        hout = num / jnp.maximum(jnp.abs(den), jnp.exp(-mj))
        y_ref[:, sl] = (jax.nn.sigmoid(o_ref[:, sl]) * hout).astype(BF16)
        m_new = mj[ch - 1:ch, :]
        b_last = b_col[ch - 1:ch, :]
        w_end = jnp.exp(b_last - b_col + i_col - m_new)
        decay = jnp.exp(b_last + m_prev - m_new)
        kw = k * w_end
        c_ref[h] = decay * c_prev + _dot_tn(kw.astype(BF16), v_bf)
        n_ref[h] = decay * n_prev + jnp.sum(kw, axis=0, keepdims=True)
        m_ref[h] = jnp.broadcast_to(m_new, (1, LANES))


def _mlstm_chunked(z0, gcol, grow, bias_col, bias_row, c0, n0, m0, nh):
    bsz, t, _ = z0.shape
    width = nh * HEAD_DIM
    ch = MLSTM_CHUNK
    zspec = lambda blk: pl.BlockSpec((None, ch, width), lambda b, c: (b, c, blk))
    state = lambda shape: pl.BlockSpec((None,) + shape, lambda b, c: (b,) + (0,) * len(shape))
    return pl.pallas_call(
        _mlstm_chunk_kernel,
        grid=(bsz, t // ch),
        in_specs=[zspec(1), zspec(2), zspec(3), zspec(4),
                  pl.BlockSpec((None, ch, 2 * nh), lambda b, c: (b, c, 0)),
                  pl.BlockSpec((None, 2 * nh, ch), lambda b, c: (b, 0, c)),
                  pl.BlockSpec((2 * nh, 1), lambda b, c: (0, 0)),
                  pl.BlockSpec((1, 2 * nh), lambda b, c: (0, 0)),
                  state((nh, HEAD_DIM, HEAD_DIM)), state((nh, 1, HEAD_DIM)), state((nh, 1, LANES))],
        out_specs=[pl.BlockSpec((None, ch, width), lambda b, c: (b, c, 0)),
                   state((nh, HEAD_DIM, HEAD_DIM)), state((nh, 1, HEAD_DIM)), state((nh, 1, LANES))],
        out_shape=[jax.ShapeDtypeStruct((bsz, t, width), BF16),
                   jax.ShapeDtypeStruct((bsz, nh, HEAD_DIM, HEAD_DIM), F32),
                   jax.ShapeDtypeStruct((bsz, nh, 1, HEAD_DIM), F32),
                   jax.ShapeDtypeStruct((bsz, nh, 1, LANES), F32)],
        compiler_params=_cparams("parallel", "arbitrary"),
        name="mlstm_chunk",
    )(z0, z0, z0, z0, gcol, grow, bias_col, bias_row, c0, n0, m0)


def _to_column(row):
    n = row.shape[1]
    eye = lax.broadcasted_iota(I32, (n, n), 0) == lax.broadcasted_iota(I32, (n, n), 1)
    return jnp.sum(jnp.where(eye, row, 0.0), axis=1, keepdims=True)


def _mlstm_step_kernel(q_ref, k_ref, v_ref, o_ref, g_ref, brow_ref, c0_ref, n0_ref, m0_ref,
                       y_ref, c_ref, n_ref, m_ref):
    nh = c_ref.shape[0]
    g = g_ref[...] + brow_ref[...]
    for h in range(nh):
        sl = slice(h * HEAD_DIM, (h + 1) * HEAD_DIM)
        q = q_ref[:, sl]
        k = k_ref[:, sl] * (HEAD_DIM ** -0.5)
        v = v_ref[:, sl]
        i_pre = g[:, h:h + 1]
        lf = jax.nn.log_sigmoid(g[:, nh + h:nh + h + 1])
        m_prev = m0_ref[h][:, 0:1]
        c_prev = c0_ref[h]
        n_prev = n0_ref[h]
        a = lf + m_prev
        mj = jnp.maximum(a, i_pre)
        w_intra = jnp.exp(i_pre - mj)
        w_inter = jnp.exp(a - mj)
        s = jnp.sum(q * k, axis=1, keepdims=True) * w_intra
        q_col = _to_column(q)
        k_col = _to_column(k)
        num = s * v + w_inter * jnp.sum(q_col * c_prev, axis=0, keepdims=True)
        den = s + w_inter * jnp.sum(q * n_prev, axis=1, keepdims=True)
        hout = num / jnp.maximum(jnp.abs(den), jnp.exp(-mj))
        y_ref[:, sl] = (jax.nn.sigmoid(o_ref[:, sl]) * hout).astype(BF16)
        w_end = jnp.exp(i_pre - mj)
        decay = jnp.exp(a - mj)
        c_ref[h] = decay * c_prev + (w_end * k_col) * v
        n_ref[h] = decay * n_prev + w_end * k
        m_ref[h] = jnp.broadcast_to(mj, (1, LANES))


def _mlstm_step(z0, g, bias_row, c0, n0, m0, nh):
    bsz = z0.shape[0]
    width = nh * HEAD_DIM
    zspec = lambda blk: pl.BlockSpec((None, 1, width), lambda b: (b, 0, blk))
    state = lambda shape: pl.BlockSpec((None,) + shape, lambda b: (b,) + (0,) * len(shape))
    return pl.pallas_call(
        _mlstm_step_kernel,
        grid=(bsz,),
        in_specs=[zspec(1), zspec(2), zspec(3), zspec(4),
                  pl.BlockSpec((None, 1, 2 * nh), lambda b: (b, 0, 0)),
                  pl.BlockSpec((1, 2 * nh), lambda b: (0, 0)),
                  state((nh, HEAD_DIM, HEAD_DIM)), state((nh, 1, HEAD_DIM)), state((nh, 1, LANES))],
        out_specs=[pl.BlockSpec((None, 1, width), lambda b: (b, 0, 0)),
                   state((nh, HEAD_DIM, HEAD_DIM)), state((nh, 1, HEAD_DIM)), state((nh, 1, LANES))],
        out_shape=[jax.ShapeDtypeStruct((bsz, 1, width), BF16),
                   jax.ShapeDtypeStruct((bsz, nh, HEAD_DIM, HEAD_DIM), F32),
                   jax.ShapeDtypeStruct((bsz, nh, 1, HEAD_DIM), F32),
                   jax.ShapeDtypeStruct((bsz, nh, 1, LANES), F32)],
        compiler_params=_cparams("parallel"),
        name="mlstm_step",
    )(z0, z0, z0, z0, g, bias_row, c0, n0, m0)


def _key_to_float(key):
    bits = key ^ ((key >> 31) & jnp.int32(0x7FFFFFFF))
    return lax.bitcast_convert_type(bits, F32)


def _kth_by_bit_search(count, shape):
    def bit_step(it, key):
        cand = key + (jnp.int32(1) << (31 - it))
        cand_f = _key_to_float(cand)
        cnt = count(lambda tile, col: tile >= cand_f)
        return jnp.where(cnt >= TOPK, cand, key)

    key = lax.fori_loop(0, 32, bit_step, jnp.full(shape, INT_MIN, I32))
    has_thr = key > INT_MIN
    return jnp.where(has_thr, _key_to_float(jnp.where(has_thr, key, 0)), -jnp.inf)


def _select_threshold(count, shape, width, j0_s, thr):
    has_thr = thr > -jnp.inf
    need = TOPK - count(lambda tile, col: tile > thr)
    n_eq = count(lambda tile, col: tile == thr)
    j0_s[...] = jnp.where(has_thr, jnp.int32(width), jnp.int32(-1))
    surplus = jnp.max(jnp.where(has_thr & (n_eq > need), 1, 0))
    nbits = max(1, (width - 1).bit_length())

    @pl.when(surplus > 0)
    def _():
        def idx_step(it, j0):
            cand = j0 | (jnp.int32(1) << (nbits - 1 - it))
            cnt = count(lambda tile, col: (tile == thr) & (col < cand))
            return jnp.where(cnt < need, cand, j0)

        j0 = lax.fori_loop(0, nbits, idx_step, jnp.zeros(shape, I32))
        j0_s[...] = jnp.where(has_thr, j0, jnp.int32(-1))

    return thr, j0_s[...]


def _network_pass(load, store, stages, groups):
    for grp in groups:
        vals = [load(i) for i in grp]
        pos = {gi: n for n, gi in enumerate(grp)}
        for size, dist in stages:
            for gi in grp:
                gl = gi ^ dist
                if gl > gi:
                    a, b = pos[gi], pos[gl]
                    hi, lo = jnp.maximum(vals[a], vals[b]), jnp.minimum(vals[a], vals[b])
                    vals[a], vals[b] = (hi, lo) if (gi & size) == 0 else (lo, hi)
        for n, gi in enumerate(grp):
            store(gi, vals[n])


_NET_GROUP = 16
_LOW_GROUPS = [[m * _NET_GROUP + t for t in range(_NET_GROUP)] for m in range(SORT_TILES // _NET_GROUP)]
_HIGH_GROUPS = [[m + (SORT_TILES // _NET_GROUP) * t for t in range(_NET_GROUP)]
                for m in range(SORT_TILES // _NET_GROUP)]


def _tile(ref, base, i):
    return ref.at[pl.ds(base + 8 * i, 8), :]


def _sort_block_desc(src, src_base, dst):
    assert SORT_TILES == 256 and _NET_GROUP == 16
    low = lambda size: [(size, d) for d in (8, 4, 2, 1) if d < size]
    first = [st for size in (2, 4, 8, 16) for st in low(size)]
    _network_pass(lambda i: _tile(src, src_base, i)[...],
                  lambda i, v: _tile(dst, 0, i).__setitem__(Ellipsis, v), first, _LOW_GROUPS)
    ld = lambda i: _tile(dst, 0, i)[...]
    st = lambda i, v: _tile(dst, 0, i).__setitem__(Ellipsis, v)
    for size in (32, 64, 128, 256):
        _network_pass(ld, st, [(size, d) for d in (128, 64, 32, 16) if d < size], _HIGH_GROUPS)
        _network_pass(ld, st, low(size), _LOW_GROUPS)


def _merge_top(run, other, tmp, shift=None):
    def ld(i):
        o = _tile(other, 0, SORT_TILES - 1 - i)[...]
        if shift is not None:
            o = pltpu.roll(o, shift, axis=0)
        return jnp.maximum(_tile(run, 0, i)[...], o)

    _network_pass(ld, lambda i, v: _tile(tmp, 0, i).__setitem__(Ellipsis, v),
                  [(SORT_TILES, d) for d in (128, 64, 32, 16)], _HIGH_GROUPS)
    _network_pass(lambda i: _tile(tmp, 0, i)[...], lambda i, v: _tile(run, 0, i).__setitem__(Ellipsis, v),
                  [(SORT_TILES, d) for d in (8, 4, 2, 1)], _LOW_GROUPS)


def _mask_bias(scores, cols, thr, j0):
    sel = (scores > thr) | ((scores == thr) & (cols <= j0))
    return jnp.where(sel, 0.0, NEG_BIG)


def _dsa_prompt_kernel(q_ref, qi_ref, wit_ref, ki_ref, k_ref, vt_ref, o_ref, sc_s, j0_s, m_s, acc_s,
                       run_s, blk_s, tmp_s):
    qb = q_ref.shape[0]
    kc_len = KEY_CHUNK
    nheads = q_ref.shape[1] // HEAD_DIM
    nkv = k_ref.shape[1] // HEAD_DIM
    rep = nheads // nkv
    i = pl.program_id(1)
    nchunks = ((i + 1) * qb + kc_len - 1) // kc_len
    qpos = i * qb + lax.broadcasted_iota(I32, (1, qb), 1)
    sub = lax.broadcasted_iota(I32, (kc_len, qb), 0)

    qi = qi_ref[...]
    wit = wit_ref[...]
    qi_all = jnp.concatenate([qi[:, h * IDX_DIM:(h + 1) * IDX_DIM] for h in range(IDX_HEADS)], axis=0)

    def score_chunk(c, _):
        off = pl.multiple_of(c * kc_len, kc_len)
        ki = ki_ref[pl.ds(off, kc_len), :]
        logits = _dot_nt(ki, qi_all)
        sc = jnp.zeros((kc_len, qb), F32)
        for h in range(IDX_HEADS):
            sc = sc + jnp.maximum(logits[:, h * qb:(h + 1) * qb], 0.0) * wit[h:h + 1, :]
        sc_s[pl.ds(off, kc_len), :] = jnp.where(sub + off <= qpos, sc, -jnp.inf)
        return 0

    lax.fori_loop(0, nchunks, score_chunk, 0)

    sort_rows = 8 * SORT_TILES
    nblocks = ((i + 1) * qb + sort_rows - 1) // sort_rows

    @pl.when(nchunks * kc_len < nblocks * sort_rows)
    def _():
        sc_s[pl.ds(pl.multiple_of(nchunks * kc_len, kc_len), kc_len), :] = jnp.full((kc_len, qb), -jnp.inf, F32)

    run_s[...] = jnp.full_like(run_s, -jnp.inf)

    def sort_block(b, _):
        _sort_block_desc(sc_s, pl.multiple_of(b * sort_rows, sort_rows), blk_s)
        _merge_top(run_s, blk_s, tmp_s)
        return 0

    lax.fori_loop(0, nblocks, sort_block, 0)
    for shift in (4, 2):
        _merge_top(run_s, run_s, tmp_s, shift)
    top = [jnp.maximum(_tile(run_s, 0, t)[...], pltpu.roll(_tile(run_s, 0, SORT_TILES - 1 - t)[...], 1, axis=0))
           for t in range(SORT_TILES)]
    while len(top) > 1:
        top = [jnp.minimum(a, b) for a, b in zip(top[0::2], top[1::2])]
    kth = top[0][0:1, :]

    def count(pred):
        def body(c, acc):
            off = pl.multiple_of(c * COUNT_CHUNK, COUNT_CHUNK)
            ind = jnp.where(pred(sc_s[pl.ds(off, COUNT_CHUNK), :], sub[:COUNT_CHUNK] + off), 1, 0)
            return acc + jnp.sum(ind.reshape(COUNT_CHUNK // 8, 8, qb), axis=0)

        ncount = ((i + 1) * qb + COUNT_CHUNK - 1) // COUNT_CHUNK
        acc = lax.fori_loop(0, ncount, body, jnp.zeros((8, qb), I32))
        return jnp.sum(acc, axis=0, keepdims=True)

    thr, j0 = _select_threshold(count, (1, qb), sc_s.shape[0], j0_s, kth)

    m_s[...] = jnp.full_like(m_s, NEG_BIG)
    acc_s[...] = jnp.zeros_like(acc_s)
    q = q_ref[...]
    q_g = [jnp.concatenate([q[:, (g * rep + r) * HEAD_DIM:(g * rep + r + 1) * HEAD_DIM]
                            for r in range(rep)], axis=0) for g in range(nkv)]
    ones_rows = jnp.ones((acc_s.shape[1] - HEAD_DIM, kc_len), BF16)

    def attend_chunk(c, _):
        off = pl.multiple_of(c * kc_len, kc_len)
        bias = _mask_bias(sc_s[pl.ds(off, kc_len), :], sub + off, thr, j0)
        bias = jnp.concatenate([bias] * rep, axis=1)
        for g in range(nkv):
            kc = k_ref[pl.ds(off, kc_len), g * HEAD_DIM:(g + 1) * HEAD_DIM]
            att = _dot_nt(kc, q_g[g]) + bias
            m_old = m_s[g]
            m_new = jnp.maximum(m_old, jnp.max(att, axis=0, keepdims=True))
            p = jnp.exp2(att - m_new).astype(BF16)
            vt = jnp.concatenate([vt_ref[g * HEAD_DIM:(g + 1) * HEAD_DIM, pl.ds(off, kc_len)], ones_rows],
                                 axis=0)
            acc_s[g] = jnp.exp2(m_old - m_new) * acc_s[g] + _dot(vt, p)
            m_s[g] = m_new
        return 0

    lax.fori_loop(0, nchunks, attend_chunk, 0)

    for g in range(nkv):
        acc = acc_s[g]
        out = acc[:HEAD_DIM, :] / acc[HEAD_DIM:HEAD_DIM + 1, :]
        for r in range(rep):
            hd = g * rep + r
            o_ref[:, hd * HEAD_DIM:(hd + 1) * HEAD_DIM] = out[:, r * qb:(r + 1) * qb].T.astype(BF16)


def _dsa_prompt(q, qi, wit, kidx, kbf, vt):
    bsz, t, width = q.shape
    kvw = kbf.shape[2]
    nkv = kvw // HEAD_DIM
    rep = width // HEAD_DIM // nkv
    qb = Q_BLOCK
    nq = t // qb
    assert t % (8 * SORT_TILES) == 0 and 8 * SORT_TILES == 2 * KEY_CHUNK and TOPK == SORT_TILES
    ones_rows = 16
    return pl.pallas_call(
        _dsa_prompt_kernel,
        grid=(bsz, nq),
        in_specs=[pl.BlockSpec((None, qb, width), lambda b, i: (b, i, 0)),
                  pl.BlockSpec((None, qb, qi.shape[2]), lambda b, i: (b, i, 0)),
                  pl.BlockSpec((IDX_HEADS, qb), lambda b, i: (0, b * nq + i)),
                  pl.BlockSpec((None, t, IDX_DIM), lambda b, i: (b, 0, 0)),
                  pl.BlockSpec((None, t, kvw), lambda b, i: (b, 0, 0)),
                  pl.BlockSpec((kvw, t), lambda b, i: (0, b))],
        out_specs=pl.BlockSpec((None, qb, width), lambda b, i: (b, i, 0)),
        out_shape=jax.ShapeDtypeStruct((bsz, t, width), BF16),
        scratch_shapes=[pltpu.VMEM((t, qb), F32), pltpu.VMEM((1, qb), I32),
                        pltpu.VMEM((nkv, 1, rep * qb), F32),
                        pltpu.VMEM((nkv, HEAD_DIM + ones_rows, rep * qb), F32),
                        pltpu.VMEM((8 * SORT_TILES, qb), F32), pltpu.VMEM((8 * SORT_TILES, qb), F32),
                        pltpu.VMEM((8 * SORT_TILES, qb), F32)],
        compiler_params=_cparams("parallel", "arbitrary"),
        name="dsa_prompt",
    )(q, qi, wit, kidx, kbf, vt)


def _dsa_score_kernel(pt_ref, qi_ref, wi_ref, knew_ref, *rest):
    pages, (keys_ref, newkey_ref) = rest[:PAGES_PER_STEP], rest[PAGES_PER_STEP:]
    qi = qi_ref[...]
    wi = wi_ref[...]
    ki_t = jnp.concatenate([page_ref[...].astype(BF16) for page_ref in pages], axis=1)
    logits = _dot(qi.astype(BF16), ki_t)
    keys_ref[...] = jnp.sum(jnp.maximum(logits, 0.0) * wi, axis=0, keepdims=True)
    logit_new = jnp.sum(qi * knew_ref[...], axis=1, keepdims=True)
    sc_new = jnp.sum(jnp.maximum(logit_new, 0.0) * wi, axis=0, keepdims=True)
    newkey_ref[...] = jnp.broadcast_to(sc_new, (1, LANES))


def _dsa_score(page_table, qi, wi, kidx_new, cache_kidx_t):
    bsz, npages = page_table.shape
    steps = npages // PAGES_PER_STEP
    page_spec = lambda p: pl.BlockSpec(
        (None, IDX_DIM, PAGE), lambda b, j, pt: (pt[b, j * PAGES_PER_STEP + p], 0, 0))
    per_seq = lambda shape: pl.BlockSpec((None,) + shape, lambda b, j, pt: (b, 0, 0))
    return pl.pallas_call(
        _dsa_score_kernel,
        grid_spec=pltpu.PrefetchScalarGridSpec(
            num_scalar_prefetch=1,
            grid=(bsz, steps),
            in_specs=[per_seq((IDX_HEADS, IDX_DIM)), per_seq((IDX_HEADS, 1)), per_seq((1, IDX_DIM))]
                     + [page_spec(p) for p in range(PAGES_PER_STEP)],
            out_specs=[pl.BlockSpec((None, 1, PAGES_PER_STEP * PAGE), lambda b, j, pt: (b, 0, j)),
                       per_seq((1, LANES))]),
        out_shape=[jax.ShapeDtypeStruct((bsz, 1, npages * PAGE), F32),
                   jax.ShapeDtypeStruct((bsz, 1, LANES), F32)],
        compiler_params=_cparams("parallel", "arbitrary"),
        name="dsa_decode_score",
    )(page_table, qi, wi, kidx_new, *([cache_kidx_t] * PAGES_PER_STEP))


def _dsa_select_kernel(keys_ref, thr_ref, j0_ref, j0_s):
    rows, width = keys_ref.shape
    lane = lax.broadcasted_iota(I32, (rows, LANES), 1)

    def count(pred):
        def body(t, acc):
            off = pl.multiple_of(t * LANES, LANES)
            return acc + jnp.where(pred(keys_ref[:, pl.ds(off, LANES)], lane + off), 1, 0)

        acc = lax.fori_loop(0, width // LANES, body, jnp.zeros((rows, LANES), I32))
        return jnp.sum(acc, axis=1, keepdims=True)

    thr, j0 = _select_threshold(count, (rows, 1), width, j0_s, _kth_by_bit_search(count, (rows, 1)))
    thr_ref[...] = jnp.broadcast_to(thr, thr_ref.shape)
    j0_ref[...] = jnp.broadcast_to(j0, j0_ref.shape)


def _dsa_select(keys):
    rows, width = keys.shape
    const = lambda i: (0, 0)
    return pl.pallas_call(
        _dsa_select_kernel,
        grid=(1,),
        in_specs=[pl.BlockSpec((rows, width), const)],
        out_specs=[pl.BlockSpec((rows, LANES), const), pl.BlockSpec((rows, LANES), const)],
        out_shape=[jax.ShapeDtypeStruct((rows, LANES), F32), jax.ShapeDtypeStruct((rows, LANES), I32)],
        scratch_shapes=[pltpu.VMEM((rows, 1), I32)],
        compiler_params=_cparams("arbitrary"),
        name="dsa_decode_select",
    )(keys)


def _dsa_decode_kernel(pt_ref, q_ref, keys_ref, tail_ref, thr_ref, j0_ref, knew_ref, vnew_ref, *rest,
                       n_past, nkv):
    j = pl.program_id(1)
    nheads = q_ref.shape[0]
    rep = nheads // nkv
    kpages, vpages = rest[:PAGES_PER_STEP], rest[PAGES_PER_STEP:2 * PAGES_PER_STEP]
    o_ref, m_s, l_s, acc_s = rest[2 * PAGES_PER_STEP:]

    @pl.when(j == 0)
    def _():
        m_s[...] = jnp.full_like(m_s, NEG_BIG)
        l_s[...] = jnp.zeros_like(l_s)
        acc_s[...] = jnp.zeros_like(acc_s)

    q = q_ref[...]
    thr = thr_ref[:, 0:1]
    j0 = j0_ref[:, 0:1]
    width = PAGES_PER_STEP * PAGE * nkv
    row = lax.broadcasted_iota(I32, (1, width), 1) + j * width
    head_kv = lax.broadcasted_iota(I32, (nheads, 1), 0) // rep

    def update(att, value_fn):
        m_old = m_s[...]
        m_new = jnp.maximum(m_old, jnp.max(att, axis=1, keepdims=True))
        alpha = jnp.exp2(m_old - m_new)
        p = jnp.exp2(att - m_new)
        l_s[...] = alpha * l_s[...] + jnp.sum(p, axis=1, keepdims=True)
        acc_s[...] = alpha * acc_s[...] + value_fn(p)
        m_s[...] = m_new

    kcat = jnp.concatenate([r[...].astype(BF16) for r in kpages], axis=0)
    vcat = jnp.concatenate([r[...].astype(BF16) for r in vpages], axis=0)
    bias = _mask_bias(keys_ref[...], row // nkv, thr, j0)
    bias = jnp.where(row % nkv == head_kv, bias, NEG_BIG)
    update(_dot_nt(q.astype(BF16), kcat) + bias, lambda pr: _dot(pr.astype(BF16), vcat))

    @pl.when(j == pl.num_programs(1) - 1)
    def _():
        k_new = jnp.zeros_like(q)
        v_new = jnp.zeros_like(q)
        for g in range(nkv):
            k_new = jnp.where(head_kv == g, knew_ref[g:g + 1, :], k_new)
            v_new = jnp.where(head_kv == g, vnew_ref[g:g + 1, :], v_new)
        bias_new = _mask_bias(tail_ref[:, 0:1], jnp.int32(n_past), thr, j0)
        att_new = jnp.sum(q * k_new, axis=1, keepdims=True) + bias_new
        update(att_new, lambda pr: pr * v_new)
        o_ref[...] = (acc_s[...] / l_s[...]).astype(BF16)


def _dsa_decode(page_table, q, keys, tail, thr, j0, k_new, v_new, cache_k, cache_v):
    bsz, npages = page_table.shape
    nheads = q.shape[1]
    nkv = k_new.shape[1]
    steps = npages // PAGES_PER_STEP
    page_spec = lambda p: pl.BlockSpec(
        (PAGE * nkv, HEAD_DIM), lambda b, j, pt: (pt[b, j * PAGES_PER_STEP + p], 0))
    per_seq = lambda shape: pl.BlockSpec((None,) + shape, lambda b, j, pt: (b, 0, 0))
    keys_spec = pl.BlockSpec((None, 1, PAGES_PER_STEP * PAGE * nkv), lambda b, j, pt: (b, 0, j))
    return pl.pallas_call(
        functools.partial(_dsa_decode_kernel, n_past=npages * PAGE, nkv=nkv),
        grid_spec=pltpu.PrefetchScalarGridSpec(
            num_scalar_prefetch=1,
            grid=(bsz, steps),
            in_specs=[per_seq((nheads, HEAD_DIM)), keys_spec, per_seq((1, LANES)), per_seq((1, LANES)),
                      per_seq((1, LANES)), per_seq((nkv, HEAD_DIM)), per_seq((nkv, HEAD_DIM))]
                     + [page_spec(p) for p in range(PAGES_PER_STEP)] * 2,
            out_specs=per_seq((nheads, HEAD_DIM)),
            scratch_shapes=[pltpu.VMEM((nheads, 1), F32), pltpu.VMEM((nheads, 1), F32),
                            pltpu.VMEM((nheads, HEAD_DIM), F32)]),
        out_shape=jax.ShapeDtypeStruct((bsz, nheads, HEAD_DIM), BF16),
        compiler_params=_cparams("parallel", "arbitrary"),
        name="dsa_decode_attend",
    )(page_table, q, keys, tail, thr, j0, k_new, v_new,
      *([cache_k] * PAGES_PER_STEP), *([cache_v] * PAGES_PER_STEP))


def _pad_cols(w, width):
    return jnp.pad(w, ((0, 0), (0, width - w.shape[1])))


def _block_diag(blocks):
    g, r, c = blocks.shape
    eye = jnp.eye(g, dtype=blocks.dtype)
    return (blocks[:, :, None, :] * eye[:, None, :, None]).reshape(g * r, g * c)


def _s5_params(lam_re, lam_im, log_dt, b_re, b_im, c_re, c_im, d_skip, w_glu, b_glu):
    dt = jnp.exp(log_dt)[:, None]
    mag = jnp.exp(lam_re * dt)
    bar_re = mag * jnp.cos(lam_im * dt)
    bar_im = mag * jnp.sin(lam_im * dt)
    inv = 1.0 / (lam_re * lam_re + lam_im * lam_im)
    coef_re = (((bar_re - 1.0) * lam_re + bar_im * lam_im) * inv)[..., None]
    coef_im = ((bar_im * lam_re - (bar_re - 1.0) * lam_im) * inv)[..., None]
    bb_re = coef_re * b_re - coef_im * b_im
    bb_im = coef_re * b_im + coef_im * b_re
    lam_rows = jnp.stack([bar_re.reshape(-1), bar_im.reshape(-1)])
    to_in = lambda z: _block_diag(jnp.swapaxes(z, 1, 2))
    wb = jnp.concatenate([to_in(bb_re), to_in(bb_im)], axis=1).astype(BF16)
    to_out = lambda z: _block_diag(jnp.swapaxes(z, 1, 2))
    wc = jnp.concatenate([to_out(c_re), to_out(-c_im)], axis=0).astype(BF16)
    return (lam_rows, wb, wc, d_skip[None, :], w_glu.astype(BF16), b_glu[None, :])


def kernel(x_prompt, x_sample, state_s5_re, state_s5_im, state_mlstm_c, state_mlstm_n, state_mlstm_m,
           cache_k, cache_v, cache_kidx, page_table, norm_mix, norm_mlp, norm_final, w_in0, s5_lam_re,
           s5_lam_im, s5_log_dt, s5_b_re, s5_b_im, s5_c_re, s5_c_im, s5_d, w_glu, b_glu, b_igate,
           b_fgate, w_out0, w_in1, w_out1, w_up, w_down):
    bp, tp, d = x_prompt.shape
    db, ts, _ = x_sample.shape
    assert ts == 1, "the decode path handles one new token per sequence"
    s5_groups, s5_state = s5_lam_re.shape
    s5_width = s5_groups * S5_GROUP
    nstate = s5_groups * s5_state
    nh = b_igate.shape[0]
    ml_width = nh * HEAD_DIM
    assert s5_width == ml_width == 512 and d == 1024
    n_past = page_table.shape[1] * PAGE
    kvw = cache_k.shape[2] * cache_k.shape[3]
    nheads = w_out1.shape[0] // HEAD_DIM

    gate_cols = s5_width + 4 * ml_width
    w0 = jnp.concatenate([w_in0[:, :gate_cols], _pad_cols(w_in0[:, gate_cols:], LANES)], axis=1).astype(BF16)
    s5p = _s5_params(s5_lam_re, s5_lam_im, s5_log_dt, s5_b_re, s5_b_im, s5_c_re, s5_c_im, s5_d, w_glu, b_glu)
    gate_bias = jnp.concatenate([b_igate, b_fgate])
    bias_row, bias_col = gate_bias[None, :], gate_bias[:, None]
    ki0 = 1024 + 2 * kvw + IDX_HEADS * IDX_DIM
    w1 = jnp.concatenate([w_in1[:, :ki0], _pad_cols(w_in1[:, ki0:ki0 + IDX_DIM], LANES)], axis=1).astype(BF16)
    wvt = w_in1[:, 1024 + kvw:1024 + 2 * kvw].T.astype(BF16)
    wwit = w_in1[:, ki0 + IDX_DIM:].T.astype(BF16)
    wo0, wo1 = w_out0.astype(BF16), w_out1.astype(BF16)
    wup, wdn = w_up.astype(BF16), w_down.astype(BF16)
    g_mix, g_mlp, g_fin = norm_mix[:, None, :], norm_mlp[:, None, :], norm_final[None, :]

    def trunk(x2d, bsz, t, s5_h0, c0, n0, m0, attend):
        n = bsz * t
        z0 = _proj0(x2d, g_mix[0], w0)
        gates = z0[:, gate_cols:gate_cols + 2 * nh]
        m0b = jnp.broadcast_to(m0[:, :, None, None], (bsz, nh, 1, LANES))
        n0r = n0[:, :, None, :]
        if t == 1:
            y_s5, h_t = _s5_step(z0, s5_h0, s5p, s5_width)
            y_ml, c_t, n_t, m_t = _mlstm_step(z0.reshape(bsz, 1, -1), gates.reshape(bsz, 1, 2 * nh),
                                              bias_row, c0, n0r, m0b, nh)
        else:
            y_s5, h_t = _s5_seq(z0.reshape(bsz, t, -1), s5_h0.reshape(bsz, 1, -1), s5p, s5_width)
            g3 = gates.reshape(bsz, t, 2 * nh)
            y_ml, c_t, n_t, m_t = _mlstm_chunked(z0.reshape(bsz, t, -1), g3, jnp.swapaxes(g3, 1, 2),
                                                 bias_col, bias_row, c0, n0r, m0b, nh)
        h_t = h_t.reshape(bsz, 2, s5_groups, s5_state)
        states = (h_t[:, 0], h_t[:, 1], c_t, n_t[:, :, 0, :], m_t[:, :, 0, 0])
        h1 = _post(x2d, y_s5.reshape(n, s5_width), (y_ml.reshape(n, ml_width), 0), wo0, g_mlp[0],
                   wup[0], wdn[0], g_fin, final_norm=False)
        q, k, v, kbf, qi, kidx, kidxbf, vt, wit = _proj1(h1, g_mix[1], w1, wvt, wwit)
        o = attend(q, k, v, kbf, qi, kidx, kidxbf, vt, wit)
        y = _post(h1, o, (o, 1), wo1, g_mlp[1], wup[1], wdn[1], g_fin, final_norm=True)
        rows = (k.reshape(bsz, t, -1, HEAD_DIM), v.reshape(bsz, t, -1, HEAD_DIM), kidx.reshape(bsz, t, IDX_DIM))
        return y.reshape(bsz, t, d), states, rows

    def attend_prompt(q, k, v, kbf, qi, kidx, kidxbf, vt, wit):
        r3 = lambda z: z.reshape(bp, tp, -1)
        return _dsa_prompt(r3(q), r3(qi), wit, r3(kidxbf), r3(kbf), vt).reshape(bp * tp, -1)

    def attend_decode(q, k, v, kbf, qi, kidx, kidxbf, vt, wit):
        qi3 = qi.astype(F32).reshape(db, IDX_HEADS, IDX_DIM)
        wi3 = wit.T.reshape(db, IDX_HEADS, 1)
        keys, newkey = _dsa_score(page_table, qi3, wi3, kidx.reshape(db, 1, IDX_DIM),
                                  jnp.swapaxes(cache_kidx, 1, 2))
        tail = jnp.where(lax.broadcasted_iota(I32, (db, LANES), 1) == 0, newkey[:, 0, :], -jnp.inf)
        keys = jnp.concatenate([keys[:, 0, :], tail], axis=1)
        thr, j0 = _dsa_select(keys)
        nkv = kvw // HEAD_DIM
        keys_rows = jnp.repeat(keys[:, :n_past], nkv, axis=1)[:, None, :]
        o = _dsa_decode(page_table, q.astype(F32).reshape(db, nheads, HEAD_DIM), keys_rows,
                        keys[:, None, n_past:], thr[:, None, :], j0[:, None, :],
                        k.reshape(db, nkv, HEAD_DIM), v.reshape(db, nkv, HEAD_DIM),
                        cache_k.reshape(-1, HEAD_DIM), cache_v.reshape(-1, HEAD_DIM))
        return o.reshape(db, nheads * HEAD_DIM)

    zeros = lambda *shape: jnp.zeros(shape, F32)
    y_p, st_p, rows_p = trunk(x_prompt.reshape(bp * tp, d), bp, tp, zeros(bp, 2 * nstate),
                              zeros(bp, nh, HEAD_DIM, HEAD_DIM), zeros(bp, nh, HEAD_DIM), zeros(bp, nh),
                              attend_prompt)
    s5_h0 = jnp.concatenate([state_s5_re.reshape(db, nstate), state_s5_im.reshape(db, nstate)], axis=1)
    y_s, st_s, rows_s = trunk(x_sample.reshape(db, d), db, 1, s5_h0, state_mlstm_c, state_mlstm_n,
                              state_mlstm_m, attend_decode)
    return (y_p, y_s) + st_p + rows_p + st_s + rows_s
```

```python
import functools
import math

import jax
import jax.numpy as jnp
from jax import lax
from jax.experimental import pallas as pl
from jax.experimental.pallas import tpu as pltpu

F32, BF16, I32 = jnp.float32, jnp.bfloat16, jnp.int32

EPS = 1e-6
LANES = 128
PAGE = 128
S5_GROUP = 16
S5_STATE = 64
HEAD_DIM = 128
IDX_DIM = 64
IDX_HEADS = 8
TOPK = 256
Q_BLOCK = 128
KEY_CHUNK = 1024
COUNT_CHUNK = 512
SORT_TILES = 256
MLSTM_CHUNK = 128
S5_CHUNK = 256
ROW_TILE = 512
FF_CHUNK = 4096
PAGES_PER_STEP = 32
INT_MIN = -2 ** 31
NEG_BIG = -1e30
VMEM_LIMIT = 56 * 1024 * 1024


def _cparams(*sem):
    return pltpu.CompilerParams(dimension_semantics=sem, vmem_limit_bytes=VMEM_LIMIT)


def _rms(x, g):
    return x * lax.rsqrt(jnp.mean(x * x, axis=-1, keepdims=True) + EPS) * g


def _dot(a, b):
    return jnp.dot(a, b, preferred_element_type=F32)


def _dot_nt(a, b):
    return lax.dot_general(a, b, (((1,), (1,)), ((), ())), preferred_element_type=F32)


def _dot_tn(a, b):
    return lax.dot_general(a, b, (((0,), (0,)), ((), ())), preferred_element_type=F32)


def _proj0_kernel(x_ref, g_ref, w_ref, o_ref):
    xn = _rms(x_ref[...], g_ref[...]).astype(BF16)
    o_ref[...] = _dot(xn, w_ref[...])


def _proj0(x, g, w):
    n, d = x.shape
    tm = min(ROW_TILE, n)
    wtot = w.shape[1]
    return pl.pallas_call(
        _proj0_kernel,
        grid=(n // tm,),
        in_specs=[pl.BlockSpec((tm, d), lambda i: (i, 0)),
                  pl.BlockSpec((1, d), lambda i: (0, 0)),
                  pl.BlockSpec((d, wtot), lambda i: (0, 0))],
        out_specs=pl.BlockSpec((tm, wtot), lambda i: (i, 0)),
        out_shape=jax.ShapeDtypeStruct((n, wtot), F32),
        compiler_params=_cparams("parallel"),
        name="proj0",
    )(x, g, w)


_Q1, _K1, _V1, _QI1, _KI1, _END1 = 0, 1024, 1280, 1536, 2048, 2176
LOG2E = 1.4426950408889634


def _proj1_kernel(x_ref, g_ref, w_ref, wvt_ref, wwit_ref, q_ref, k_ref, v_ref, kbf_ref, qi_ref,
                  kidx_ref, kidxbf_ref, vt_ref, wit_ref):
    xn = _rms(x_ref[...], g_ref[...]).astype(BF16)
    z = _dot(xn, w_ref[...])
    q_ref[...] = (z[:, _Q1:_K1] * (HEAD_DIM ** -0.5 * LOG2E)).astype(BF16)
    k = z[:, _K1:_V1]
    k_ref[...] = k
    v_ref[...] = z[:, _V1:_QI1]
    kbf_ref[...] = k.astype(BF16)
    qi_ref[...] = z[:, _QI1:_KI1].astype(BF16)
    kidx = z[:, _KI1:_KI1 + IDX_DIM]
    kidx_ref[...] = kidx
    kidxbf_ref[...] = kidx.astype(BF16)
    vt_ref[...] = _dot_nt(wvt_ref[...], xn).astype(BF16)
    wit_ref[...] = _dot_nt(wwit_ref[...], xn) * ((IDX_DIM ** -0.5) * (IDX_HEADS ** -0.5))


def _proj1(x, g, w, wvt, wwit):
    n, d = x.shape
    tm = min(ROW_TILE, n)
    kvw = wvt.shape[0]
    row = lambda width: pl.BlockSpec((tm, width), lambda i: (i, 0))
    col = lambda height: pl.BlockSpec((height, tm), lambda i: (0, i))
    full = lambda a: pl.BlockSpec(a.shape, lambda i: (0, 0))
    shp = lambda width, dt: jax.ShapeDtypeStruct((n, width), dt)
    return pl.pallas_call(
        _proj1_kernel,
        grid=(n // tm,),
        in_specs=[row(d), pl.BlockSpec((1, d), lambda i: (0, 0)), full(w), full(wvt), full(wwit)],
        out_specs=[row(1024), row(kvw), row(kvw), row(kvw), row(512), row(IDX_DIM), row(IDX_DIM),
                   col(kvw), col(IDX_HEADS)],
        out_shape=[shp(1024, BF16), shp(kvw, F32), shp(kvw, F32), shp(kvw, BF16), shp(512, BF16),
                   shp(IDX_DIM, F32), shp(IDX_DIM, BF16),
                   jax.ShapeDtypeStruct((kvw, n), BF16), jax.ShapeDtypeStruct((IDX_HEADS, n), F32)],
        compiler_params=_cparams("parallel"),
        name="proj1",
    )(x, g, w, wvt, wwit)


def _post_kernel(h_ref, ya_ref, yb_ref, wo_ref, g_ref, wup_ref, wdn_ref, gf_ref, out_ref,
                 h1_s, xn_s, acc_s, *, final_norm):
    j = pl.program_id(1)
    half = ya_ref.shape[1]

    @pl.when(j == 0)
    def _():
        h1 = h_ref[...] + _dot(ya_ref[...], wo_ref[:half, :]) + _dot(yb_ref[...], wo_ref[half:, :])
        h1_s[...] = h1
        xn_s[...] = _rms(h1, g_ref[...]).astype(BF16)
        acc_s[...] = jnp.zeros_like(acc_s)

    r = jnp.maximum(_dot(xn_s[...], wup_ref[...]), 0.0)
    acc_s[...] += _dot((r * r).astype(BF16), wdn_ref[...])

    @pl.when(j == pl.num_programs(1) - 1)
    def _():
        o = h1_s[...] + acc_s[...]
        if final_norm:
            o = _rms(o, gf_ref[...])
        out_ref[...] = o


def _post(h, ya, yb_spec_arg, wo, g, wup, wdn, gf, *, final_norm):
    n, d = h.shape
    tm = min(ROW_TILE, n)
    yb, yb_col = yb_spec_arg
    half = d // 2
    dff = wup.shape[1]
    once = pl.Buffered(1) if dff == FF_CHUNK else None
    return pl.pallas_call(
        functools.partial(_post_kernel, final_norm=final_norm),
        grid=(n // tm, dff // FF_CHUNK),
        in_specs=[pl.BlockSpec((tm, d), lambda i, j: (i, 0)),
                  pl.BlockSpec((tm, half), lambda i, j: (i, 0)),
                  pl.BlockSpec((tm, half), lambda i, j: (i, yb_col)),
                  pl.BlockSpec((d, d), lambda i, j: (0, 0), pipeline_mode=once),
                  pl.BlockSpec((1, d), lambda i, j: (0, 0)),
                  pl.BlockSpec((d, FF_CHUNK), lambda i, j: (0, j), pipeline_mode=once),
                  pl.BlockSpec((FF_CHUNK, d), lambda i, j: (j, 0), pipeline_mode=once),
                  pl.BlockSpec((1, d), lambda i, j: (0, 0))],
        out_specs=pl.BlockSpec((tm, d), lambda i, j: (i, 0)),
        out_shape=jax.ShapeDtypeStruct((n, d), F32),
        scratch_shapes=[pltpu.VMEM((tm, d), F32), pltpu.VMEM((tm, d), BF16), pltpu.VMEM((tm, d), F32)],
        compiler_params=_cparams("parallel", "arbitrary"),
        name="post_final" if final_norm else "post",
    )(h, ya, yb, wo, g, wup, wdn, gf)


S5_SPLIT = 2


def _s5_input(u, wb_ref):
    width, nstate = wb_ref.shape[0], wb_ref.shape[1] // 2
    wi, ns = width // S5_SPLIT, nstate // S5_SPLIT
    u_bf = u.astype(BF16)
    part = lambda base: jnp.concatenate(
        [_dot(u_bf[:, r * wi:(r + 1) * wi], wb_ref[r * wi:(r + 1) * wi, base + r * ns:base + (r + 1) * ns])
         for r in range(S5_SPLIT)], axis=1)
    return part(0), part(nstate)


def _s5_output(hre, him, u, wc_ref, d_ref, wg_ref, bg_ref):
    nstate, width = hre.shape[1], wc_ref.shape[1]
    wi, ns = width // S5_SPLIT, nstate // S5_SPLIT
    hre_bf, him_bf = hre.astype(BF16), him.astype(BF16)
    y = jnp.concatenate(
        [_dot(hre_bf[:, r * ns:(r + 1) * ns], wc_ref[r * ns:(r + 1) * ns, r * wi:(r + 1) * wi])
         + _dot(him_bf[:, r * ns:(r + 1) * ns], wc_ref[nstate + r * ns:nstate + (r + 1) * ns, r * wi:(r + 1) * wi])
         for r in range(S5_SPLIT)], axis=1)
    y = jax.nn.gelu(y + d_ref[...] * u)
    gate = jax.nn.sigmoid(_dot(y.astype(BF16), wg_ref[...]) + bg_ref[...])
    return (y * gate).astype(BF16)


def _s5_seq_kernel(u_ref, h0_ref, lam_ref, wb_ref, wc_ref, d_ref, wg_ref, bg_ref, y_ref, ht_ref,
                   hre_s, him_s, carry_s):
    bsz, _, half = hre_s.shape

    @pl.when(pl.program_id(0) == 0)
    def _():
        carry_s[...] = h0_ref[...]

    for b in range(bsz):
        hre_s[b], him_s[b] = _s5_input(u_ref[b], wb_ref)
    a_re = lam_ref[0:1, :]
    a_im = lam_ref[1:2, :]

    def step(t, carry):
        out = []
        for b in range(bsz):
            h_re, h_im = carry[2 * b], carry[2 * b + 1]
            n_re = a_re * h_re - a_im * h_im + hre_s[b, pl.ds(t, 1), :]
            n_im = a_re * h_im + a_im * h_re + him_s[b, pl.ds(t, 1), :]
            hre_s[b, pl.ds(t, 1), :] = n_re
            him_s[b, pl.ds(t, 1), :] = n_im
            out += [n_re, n_im]
        return tuple(out)

    init = tuple(carry_s[b][:, lo:lo + half] for b in range(bsz) for lo in (0, half))
    final = lax.fori_loop(0, hre_s.shape[1], step, init, unroll=8)
    for b in range(bsz):
        carry_s[b, :, :half] = final[2 * b]
        carry_s[b, :, half:] = final[2 * b + 1]
        y_ref[b] = _s5_output(hre_s[b], him_s[b], u_ref[b], wc_ref, d_ref, wg_ref, bg_ref)
    ht_ref[...] = carry_s[...]


def _s5_step_kernel(u_ref, h0_ref, lam_ref, wb_ref, wc_ref, d_ref, wg_ref, bg_ref, y_ref, ht_ref):
    half = lam_ref.shape[1]
    u = u_ref[...]
    bu_re, bu_im = _s5_input(u, wb_ref)
    a_re = lam_ref[0:1, :]
    a_im = lam_ref[1:2, :]
    h_re = h0_ref[:, :half]
    h_im = h0_ref[:, half:]
    n_re = a_re * h_re - a_im * h_im + bu_re
    n_im = a_re * h_im + a_im * h_re + bu_im
    ht_ref[:, :half] = n_re
    ht_ref[:, half:] = n_im
    y_ref[...] = _s5_output(n_re, n_im, u, wc_ref, d_ref, wg_ref, bg_ref)


def _s5_param_specs(width, nstate, imap):
    return [pl.BlockSpec((2, nstate), imap),
            pl.BlockSpec((width, 2 * nstate), imap),
            pl.BlockSpec((2 * nstate, width), imap),
            pl.BlockSpec((1, width), imap),
            pl.BlockSpec((width, width), imap),
            pl.BlockSpec((1, width), imap)]


def _s5_seq(z0, h0, params, width):
    bsz, t, _ = z0.shape
    nstate = params[0].shape[1]
    ts = min(S5_CHUNK, t)
    const = lambda c: (0, 0)
    return pl.pallas_call(
        _s5_seq_kernel,
        grid=(t // ts,),
        in_specs=[pl.BlockSpec((bsz, ts, width), lambda c: (0, c, 0)),
                  pl.BlockSpec((bsz, 1, 2 * nstate), lambda c: (0, 0, 0))]
                 + _s5_param_specs(width, nstate, const),
        out_specs=[pl.BlockSpec((bsz, ts, width), lambda c: (0, c, 0)),
                   pl.BlockSpec((bsz, 1, 2 * nstate), lambda c: (0, 0, 0))],
        out_shape=[jax.ShapeDtypeStruct((bsz, t, width), BF16),
                   jax.ShapeDtypeStruct((bsz, 1, 2 * nstate), F32)],
        scratch_shapes=[pltpu.VMEM((bsz, ts, nstate), F32), pltpu.VMEM((bsz, ts, nstate), F32),
                        pltpu.VMEM((bsz, 1, 2 * nstate), F32)],
        compiler_params=_cparams("arbitrary"),
        name="s5_scan",
    )(z0, h0, *params)


def _s5_step(z0, h0, params, width):
    rows = z0.shape[0]
    nstate = params[0].shape[1]
    const = lambda i: (0, 0)
    return pl.pallas_call(
        _s5_step_kernel,
        grid=(1,),
        in_specs=[pl.BlockSpec((rows, width), const), pl.BlockSpec((rows, 2 * nstate), const)]
                 + _s5_param_specs(width, nstate, const),
        out_specs=[pl.BlockSpec((rows, width), const), pl.BlockSpec((rows, 2 * nstate), const)],
        out_shape=[jax.ShapeDtypeStruct((rows, width), BF16),
                   jax.ShapeDtypeStruct((rows, 2 * nstate), F32)],
        compiler_params=_cparams("arbitrary"),
        name="s5_step",
    )(z0, h0, *params)


def _mlstm_chunk_kernel(q_ref, k_ref, v_ref, o_ref, gcol_ref, grow_ref, bcol_ref, brow_ref,
                        c0_ref, n0_ref, m0_ref, y_ref, c_ref, n_ref, m_ref):
    nh = c_ref.shape[0]
    ch = q_ref.shape[0]

    @pl.when(pl.program_id(1) == 0)
    def _():
        c_ref[...] = c0_ref[...]
        n_ref[...] = n0_ref[...]
        m_ref[...] = m0_ref[...]

    gcol = gcol_ref[...] + brow_ref[...]
    grow = grow_ref[...] + bcol_ref[...]
    t_idx = lax.broadcasted_iota(I32, (ch, ch), 0)
    s_idx = lax.broadcasted_iota(I32, (ch, ch), 1)
    causal = t_idx >= s_idx
    heads = range(nh)
    sl = [slice(h * HEAD_DIM, (h + 1) * HEAD_DIM) for h in heads]
    q = [q_ref[:, sl[h]] for h in heads]
    k = [k_ref[:, sl[h]] * (HEAD_DIM ** -0.5) for h in heads]
    v_bf = [v_ref[:, sl[h]].astype(BF16) for h in heads]
    q_bf = [x.astype(BF16) for x in q]
    k_bf = [x.astype(BF16) for x in k]
    i_col = [gcol[:, h:h + 1] for h in heads]
    i_row = [grow[h:h + 1, :] for h in heads]
    lf_col = [jax.nn.log_sigmoid(gcol[:, nh + h:nh + h + 1]) for h in heads]
    lf_row = [jax.nn.log_sigmoid(grow[nh + h:nh + h + 1, :]) for h in heads]
    b_col = [jnp.sum(jnp.where(causal, lf_row[h], 0.0), axis=1, keepdims=True) for h in heads]
    b_row = [jnp.sum(jnp.where(causal, 0.0, lf_col[h]), axis=0, keepdims=True) + lf_row[h] for h in heads]
    m_prev = [m_ref[h][:, 0:1] for h in heads]
    dmat = [jnp.where(causal, b_col[h] - b_row[h] + i_row[h], -jnp.inf) for h in heads]
    a_col = [b_col[h] + m_prev[h] for h in heads]
    mj = [jnp.maximum(a_col[h], jnp.max(dmat[h], axis=1, keepdims=True)) for h in heads]
    w_intra = [jnp.exp(dmat[h] - mj[h]) for h in heads]
    w_inter = [jnp.exp(a_col[h] - mj[h]) for h in heads]
    s = [_dot_nt(q_bf[h], k_bf[h]) * w_intra[h] for h in heads]
    c_prev = [c_ref[h] for h in heads]
    n_prev = [n_ref[h] for h in heads]
    num = [_dot(s[h].astype(BF16), v_bf[h]) + w_inter[h] * _dot(q_bf[h], c_prev[h].astype(BF16)) for h in heads]
    den = [jnp.sum(s[h], axis=1, keepdims=True) + w_inter[h] * jnp.sum(q[h] * n_prev[h], axis=1, keepdims=True)
           for h in heads]
    for h in heads:
        hout = num[h] / jnp.maximum(jnp.abs(den[h]), jnp.exp(-mj[h]))
        y_ref[:, sl[h]] = (jax.nn.sigmoid(o_ref[:, sl[h]]) * hout).astype(BF16)
    m_new = [mj[h][ch - 1:ch, :] for h in heads]
    b_last = [b_col[h][ch - 1:ch, :] for h in heads]
    kw = [k[h] * jnp.exp(b_last[h] - b_col[h] + i_col[h] - m_new[h]) for h in heads]
    decay = [jnp.exp(b_last[h] + m_prev[h] - m_new[h]) for h in heads]
    for h in heads:
        c_ref[h] = decay[h] * c_prev[h] + _dot_tn(kw[h].astype(BF16), v_bf[h])
        n_ref[h] = decay[h] * n_prev[h] + jnp.sum(kw[h], axis=0, keepdims=True)
        m_ref[h] = jnp.broadcast_to(m_new[h], (1, LANES))


def _mlstm_chunked(z0, gcol, grow, bias_col, bias_row, c0, n0, m0, nh):
    bsz, t, _ = z0.shape
    width = nh * HEAD_DIM
    ch = MLSTM_CHUNK
    zspec = lambda blk: pl.BlockSpec((None, ch, width), lambda b, c: (b, c, blk))
    state = lambda shape: pl.BlockSpec((None,) + shape, lambda b, c: (b,) + (0,) * len(shape))
    return pl.pallas_call(
        _mlstm_chunk_kernel,
        grid=(bsz, t // ch),
        in_specs=[zspec(1), zspec(2), zspec(3), zspec(4),
                  pl.BlockSpec((None, ch, 2 * nh), lambda b, c: (b, c, 0)),
                  pl.BlockSpec((None, 2 * nh, ch), lambda b, c: (b, 0, c)),
                  pl.BlockSpec((2 * nh, 1), lambda b, c: (0, 0)),
                  pl.BlockSpec((1, 2 * nh), lambda b, c: (0, 0)),
                  state((nh, HEAD_DIM, HEAD_DIM)), state((nh, 1, HEAD_DIM)), state((nh, 1, LANES))],
        out_specs=[pl.BlockSpec((None, ch, width), lambda b, c: (b, c, 0)),
                   state((nh, HEAD_DIM, HEAD_DIM)), state((nh, 1, HEAD_DIM)), state((nh, 1, LANES))],
        out_shape=[jax.ShapeDtypeStruct((bsz, t, width), BF16),
                   jax.ShapeDtypeStruct((bsz, nh, HEAD_DIM, HEAD_DIM), F32),
                   jax.ShapeDtypeStruct((bsz, nh, 1, HEAD_DIM), F32),
                   jax.ShapeDtypeStruct((bsz, nh, 1, LANES), F32)],
        compiler_params=_cparams("parallel", "arbitrary"),
        name="mlstm_chunk",
    )(z0, z0, z0, z0, gcol, grow, bias_col, bias_row, c0, n0, m0)


def _to_column(row):
    n = row.shape[1]
    eye = lax.broadcasted_iota(I32, (n, n), 0) == lax.broadcasted_iota(I32, (n, n), 1)
    return jnp.sum(jnp.where(eye, row, 0.0), axis=1, keepdims=True)


def _mlstm_step_kernel(q_ref, k_ref, v_ref, o_ref, g_ref, brow_ref, c0_ref, n0_ref, m0_ref,
                       y_ref, c_ref, n_ref, m_ref):
    nh = c_ref.shape[0]
    g = g_ref[...] + brow_ref[...]
    for h in range(nh):
        sl = slice(h * HEAD_DIM, (h + 1) * HEAD_DIM)
        q = q_ref[:, sl]
        k = k_ref[:, sl] * (HEAD_DIM ** -0.5)
        v = v_ref[:, sl]
        i_pre = g[:, h:h + 1]
        lf = jax.nn.log_sigmoid(g[:, nh + h:nh + h + 1])
        m_prev = m0_ref[h][:, 0:1]
        c_prev = c0_ref[h]
        n_prev = n0_ref[h]
        a = lf + m_prev
        mj = jnp.maximum(a, i_pre)
        w_intra = jnp.exp(i_pre - mj)
        w_inter = jnp.exp(a - mj)
        s = jnp.sum(q * k, axis=1, keepdims=True) * w_intra
        q_col = _to_column(q)
        k_col = _to_column(k)
        num = s * v + w_inter * jnp.sum(q_col * c_prev, axis=0, keepdims=True)
        den = s + w_inter * jnp.sum(q * n_prev, axis=1, keepdims=True)
        hout = num / jnp.maximum(jnp.abs(den), jnp.exp(-mj))
        y_ref[:, sl] = (jax.nn.sigmoid(o_ref[:, sl]) * hout).astype(BF16)
        w_end = jnp.exp(i_pre - mj)
        decay = jnp.exp(a - mj)
        c_ref[h] = decay * c_prev + (w_end * k_col) * v
        n_ref[h] = decay * n_prev + w_end * k
        m_ref[h] = jnp.broadcast_to(mj, (1, LANES))


def _mlstm_step(z0, g, bias_row, c0, n0, m0, nh):
    bsz = z0.shape[0]
    width = nh * HEAD_DIM
    zspec = lambda blk: pl.BlockSpec((None, 1, width), lambda b: (b, 0, blk))
    state = lambda shape: pl.BlockSpec((None,) + shape, lambda b: (b,) + (0,) * len(shape))
    return pl.pallas_call(
        _mlstm_step_kernel,
        grid=(bsz,),
        in_specs=[zspec(1), zspec(2), zspec(3), zspec(4),
                  pl.BlockSpec((None, 1, 2 * nh), lambda b: (b, 0, 0)),
                  pl.BlockSpec((1, 2 * nh), lambda b: (0, 0)),
                  state((nh, HEAD_DIM, HEAD_DIM)), state((nh, 1, HEAD_DIM)), state((nh, 1, LANES))],
        out_specs=[pl.BlockSpec((None, 1, width), lambda b: (b, 0, 0)),
                   state((nh, HEAD_DIM, HEAD_DIM)), state((nh, 1, HEAD_DIM)), state((nh, 1, LANES))],
        out_shape=[jax.ShapeDtypeStruct((bsz, 1, width), BF16),
                   jax.ShapeDtypeStruct((bsz, nh, HEAD_DIM, HEAD_DIM), F32),
                   jax.ShapeDtypeStruct((bsz, nh, 1, HEAD_DIM), F32),
                   jax.ShapeDtypeStruct((bsz, nh, 1, LANES), F32)],
        compiler_params=_cparams("parallel"),
        name="mlstm_step",
    )(z0, z0, z0, z0, g, bias_row, c0, n0, m0)


def _key_to_float(key):
    bits = key ^ ((key >> 31) & jnp.int32(0x7FFFFFFF))
    return lax.bitcast_convert_type(bits, F32)


def _kth_by_bit_search(count, shape):
    def bit_step(it, key):
        cand = key + (jnp.int32(1) << (31 - it))
        cand_f = _key_to_float(cand)
        cnt = count(lambda tile, col: tile >= cand_f)
        return jnp.where(cnt >= TOPK, cand, key)

    key = lax.fori_loop(0, 32, bit_step, jnp.full(shape, INT_MIN, I32))
    has_thr = key > INT_MIN
    return jnp.where(has_thr, _key_to_float(jnp.where(has_thr, key, 0)), -jnp.inf)


def _select_threshold(count, shape, width, j0_s, thr):
    has_thr = thr > -jnp.inf
    need = TOPK - count(lambda tile, col: tile > thr)
    n_eq = count(lambda tile, col: tile == thr)
    j0_s[...] = jnp.where(has_thr, jnp.int32(width), jnp.int32(-1))
    surplus = jnp.max(jnp.where(has_thr & (n_eq > need), 1, 0))
    nbits = max(1, (width - 1).bit_length())

    @pl.when(surplus > 0)
    def _():
        def idx_step(it, j0):
            cand = j0 | (jnp.int32(1) << (nbits - 1 - it))
            cnt = count(lambda tile, col: (tile == thr) & (col < cand))
            return jnp.where(cnt < need, cand, j0)

        j0 = lax.fori_loop(0, nbits, idx_step, jnp.zeros(shape, I32))
        j0_s[...] = jnp.where(has_thr, j0, jnp.int32(-1))

    return thr, j0_s[...]


def _network_pass(load, store, stages, groups):
    for grp in groups:
        vals = [load(i) for i in grp]
        pos = {gi: n for n, gi in enumerate(grp)}
        for size, dist in stages:
            for gi in grp:
                gl = gi ^ dist
                if gl > gi:
                    a, b = pos[gi], pos[gl]
                    hi, lo = jnp.maximum(vals[a], vals[b]), jnp.minimum(vals[a], vals[b])
                    vals[a], vals[b] = (hi, lo) if (gi & size) == 0 else (lo, hi)
        for n, gi in enumerate(grp):
            store(gi, vals[n])


_NET_GROUP = 16
_LOW_GROUPS = [[m * _NET_GROUP + t for t in range(_NET_GROUP)] for m in range(SORT_TILES // _NET_GROUP)]
_HIGH_GROUPS = [[m + (SORT_TILES // _NET_GROUP) * t for t in range(_NET_GROUP)]
                for m in range(SORT_TILES // _NET_GROUP)]


def _tile(ref, base, i):
    return ref.at[pl.ds(base + 8 * i, 8), :]


def _sort_block_desc(src, src_base, dst):
    assert SORT_TILES == 256 and _NET_GROUP == 16
    low = lambda size: [(size, d) for d in (8, 4, 2, 1) if d < size]
    first = [st for size in (2, 4, 8, 16) for st in low(size)]
    _network_pass(lambda i: _tile(src, src_base, i)[...],
                  lambda i, v: _tile(dst, 0, i).__setitem__(Ellipsis, v), first, _LOW_GROUPS)
    ld = lambda i: _tile(dst, 0, i)[...]
    st = lambda i, v: _tile(dst, 0, i).__setitem__(Ellipsis, v)
    for size in (32, 64, 128, 256):
        _network_pass(ld, st, [(size, d) for d in (128, 64, 32, 16) if d < size], _HIGH_GROUPS)
        _network_pass(ld, st, low(size), _LOW_GROUPS)


def _merge_top(run, other, tmp, shift=None):
    def ld(i):
        o = _tile(other, 0, SORT_TILES - 1 - i)[...]
        if shift is not None:
            o = pltpu.roll(o, shift, axis=0)
        return jnp.maximum(_tile(run, 0, i)[...], o)

    _network_pass(ld, lambda i, v: _tile(tmp, 0, i).__setitem__(Ellipsis, v),
                  [(SORT_TILES, d) for d in (128, 64, 32, 16)], _HIGH_GROUPS)
    _network_pass(lambda i: _tile(tmp, 0, i)[...], lambda i, v: _tile(run, 0, i).__setitem__(Ellipsis, v),
                  [(SORT_TILES, d) for d in (8, 4, 2, 1)], _LOW_GROUPS)


def _mask_bias(scores, cols, thr, j0):
    sel = (scores > thr) | ((scores == thr) & (cols <= j0))
    return jnp.where(sel, 0.0, NEG_BIG)


def _dsa_prompt_kernel(q_ref, qi_ref, wit_ref, ki_ref, k_ref, vt_ref, o_ref, sc_s, j0_s, m_s, acc_s,
                       run_s, blk_s, tmp_s):
    qb = q_ref.shape[0]
    kc_len = KEY_CHUNK
    nheads = q_ref.shape[1] // HEAD_DIM
    nkv = k_ref.shape[1] // HEAD_DIM
    rep = nheads // nkv
    i = pl.program_id(1)
    nchunks = ((i + 1) * qb + kc_len - 1) // kc_len
    qpos = i * qb + lax.broadcasted_iota(I32, (1, qb), 1)
    sub = lax.broadcasted_iota(I32, (kc_len, qb), 0)

    qi = qi_ref[...]
    wit = wit_ref[...]
    qi_all = jnp.concatenate([qi[:, h * IDX_DIM:(h + 1) * IDX_DIM] for h in range(IDX_HEADS)], axis=0)

    def score_chunk(c, _):
        off = pl.multiple_of(c * kc_len, kc_len)
        ki = ki_ref[pl.ds(off, kc_len), :]
        logits = _dot_nt(ki, qi_all)
        sc = jnp.zeros((kc_len, qb), F32)
        for h in range(IDX_HEADS):
            sc = sc + jnp.maximum(logits[:, h * qb:(h + 1) * qb], 0.0) * wit[h:h + 1, :]
        sc_s[pl.ds(off, kc_len), :] = jnp.where(sub + off <= qpos, sc, -jnp.inf)
        return 0

    lax.fori_loop(0, nchunks, score_chunk, 0)

    sort_rows = 8 * SORT_TILES
    nblocks = ((i + 1) * qb + sort_rows - 1) // sort_rows

    @pl.when(nchunks * kc_len < nblocks * sort_rows)
    def _():
        sc_s[pl.ds(pl.multiple_of(nchunks * kc_len, kc_len), kc_len), :] = jnp.full((kc_len, qb), -jnp.inf, F32)

    run_s[...] = jnp.full_like(run_s, -jnp.inf)

    def sort_block(b, _):
        _sort_block_desc(sc_s, pl.multiple_of(b * sort_rows, sort_rows), blk_s)
        _merge_top(run_s, blk_s, tmp_s)
        return 0

    lax.fori_loop(0, nblocks, sort_block, 0)
    for shift in (4, 2):
        _merge_top(run_s, run_s, tmp_s, shift)
    top = [jnp.maximum(_tile(run_s, 0, t)[...], pltpu.roll(_tile(run_s, 0, SORT_TILES - 1 - t)[...], 1, axis=0))
           for t in range(SORT_TILES)]
    while len(top) > 1:
        top = [jnp.minimum(a, b) for a, b in zip(top[0::2], top[1::2])]
    kth = top[0][0:1, :]

    def count(pred):
        def body(c, acc):
            off = pl.multiple_of(c * COUNT_CHUNK, COUNT_CHUNK)
            ind = jnp.where(pred(sc_s[pl.ds(off, COUNT_CHUNK), :], sub[:COUNT_CHUNK] + off), 1, 0)
            return acc + jnp.sum(ind.reshape(COUNT_CHUNK // 8, 8, qb), axis=0)

        ncount = ((i + 1) * qb + COUNT_CHUNK - 1) // COUNT_CHUNK
        acc = lax.fori_loop(0, ncount, body, jnp.zeros((8, qb), I32))
        return jnp.sum(acc, axis=0, keepdims=True)

    thr, j0 = _select_threshold(count, (1, qb), sc_s.shape[0], j0_s, kth)

    m_s[...] = jnp.full_like(m_s, NEG_BIG)
    acc_s[...] = jnp.zeros_like(acc_s)
    q = q_ref[...]
    q_g = [jnp.concatenate([q[:, (g * rep + r) * HEAD_DIM:(g * rep + r + 1) * HEAD_DIM]
                            for r in range(rep)], axis=0) for g in range(nkv)]
    ones_rows = jnp.ones((acc_s.shape[1] - HEAD_DIM, kc_len), BF16)

    def attend_chunk(c, _):
        off = pl.multiple_of(c * kc_len, kc_len)
        bias = _mask_bias(sc_s[pl.ds(off, kc_len), :], sub + off, thr, j0)
        bias = jnp.concatenate([bias] * rep, axis=1)
        for g in range(nkv):
            kc = k_ref[pl.ds(off, kc_len), g * HEAD_DIM:(g + 1) * HEAD_DIM]
            att = _dot_nt(kc, q_g[g]) + bias
            m_old = m_s[g]
            m_new = jnp.maximum(m_old, jnp.max(att, axis=0, keepdims=True))
            p = jnp.exp2(att - m_new).astype(BF16)
            vt = jnp.concatenate([vt_ref[g * HEAD_DIM:(g + 1) * HEAD_DIM, pl.ds(off, kc_len)], ones_rows],
                                 axis=0)
            acc_s[g] = jnp.exp2(m_old - m_new) * acc_s[g] + _dot(vt, p)
            m_s[g] = m_new
        return 0

    lax.fori_loop(0, nchunks, attend_chunk, 0)

    for g in range(nkv):
        acc = acc_s[g]
        out = acc[:HEAD_DIM, :] / acc[HEAD_DIM:HEAD_DIM + 1, :]
        for r in range(rep):
            hd = g * rep + r
            o_ref[:, hd * HEAD_DIM:(hd + 1) * HEAD_DIM] = out[:, r * qb:(r + 1) * qb].T.astype(BF16)


def _dsa_prompt(q, qi, wit, kidx, kbf, vt):
    bsz, t, width = q.shape
    kvw = kbf.shape[2]
    nkv = kvw // HEAD_DIM
    rep = width // HEAD_DIM // nkv
    qb = Q_BLOCK
    nq = t // qb
    assert t % (8 * SORT_TILES) == 0 and 8 * SORT_TILES == 2 * KEY_CHUNK and TOPK == SORT_TILES
    ones_rows = 16
    return pl.pallas_call(
        _dsa_prompt_kernel,
        grid=(bsz, nq),
        in_specs=[pl.BlockSpec((None, qb, width), lambda b, i: (b, i, 0)),
                  pl.BlockSpec((None, qb, qi.shape[2]), lambda b, i: (b, i, 0)),
                  pl.BlockSpec((IDX_HEADS, qb), lambda b, i: (0, b * nq + i)),
                  pl.BlockSpec((None, t, IDX_DIM), lambda b, i: (b, 0, 0)),
                  pl.BlockSpec((None, t, kvw), lambda b, i: (b, 0, 0)),
                  pl.BlockSpec((kvw, t), lambda b, i: (0, b))],
        out_specs=pl.BlockSpec((None, qb, width), lambda b, i: (b, i, 0)),
        out_shape=jax.ShapeDtypeStruct((bsz, t, width), BF16),
        scratch_shapes=[pltpu.VMEM((t, qb), F32), pltpu.VMEM((1, qb), I32),
                        pltpu.VMEM((nkv, 1, rep * qb), F32),
                        pltpu.VMEM((nkv, HEAD_DIM + ones_rows, rep * qb), F32),
                        pltpu.VMEM((8 * SORT_TILES, qb), F32), pltpu.VMEM((8 * SORT_TILES, qb), F32),
                        pltpu.VMEM((8 * SORT_TILES, qb), F32)],
        compiler_params=_cparams("parallel", "arbitrary"),
        name="dsa_prompt",
    )(q, qi, wit, kidx, kbf, vt)


def _dsa_score_kernel(pt_ref, qi_ref, wi_ref, knew_ref, *rest):
    pages, (keys_ref, newkey_ref) = rest[:PAGES_PER_STEP], rest[PAGES_PER_STEP:]
    qi = qi_ref[...]
    wi = wi_ref[...]
    ki_t = jnp.concatenate([page_ref[...].astype(BF16) for page_ref in pages], axis=1)
    logits = _dot(qi.astype(BF16), ki_t)
    keys_ref[...] = jnp.sum(jnp.maximum(logits, 0.0) * wi, axis=0, keepdims=True)
    logit_new = jnp.sum(qi * knew_ref[...], axis=1, keepdims=True)
    sc_new = jnp.sum(jnp.maximum(logit_new, 0.0) * wi, axis=0, keepdims=True)
    newkey_ref[...] = jnp.broadcast_to(sc_new, (1, LANES))


def _dsa_score(page_table, qi, wi, kidx_new, cache_kidx_t):
    bsz, npages = page_table.shape
    steps = npages // PAGES_PER_STEP
    page_spec = lambda p: pl.BlockSpec(
        (None, IDX_DIM, PAGE), lambda b, j, pt: (pt[b, j * PAGES_PER_STEP + p], 0, 0))
    per_seq = lambda shape: pl.BlockSpec((None,) + shape, lambda b, j, pt: (b, 0, 0))
    return pl.pallas_call(
        _dsa_score_kernel,
        grid_spec=pltpu.PrefetchScalarGridSpec(
            num_scalar_prefetch=1,
            grid=(bsz, steps),
            in_specs=[per_seq((IDX_HEADS, IDX_DIM)), per_seq((IDX_HEADS, 1)), per_seq((1, IDX_DIM))]
                     + [page_spec(p) for p in range(PAGES_PER_STEP)],
            out_specs=[pl.BlockSpec((None, 1, PAGES_PER_STEP * PAGE), lambda b, j, pt: (b, 0, j)),
                       per_seq((1, LANES))]),
        out_shape=[jax.ShapeDtypeStruct((bsz, 1, npages * PAGE), F32),
                   jax.ShapeDtypeStruct((bsz, 1, LANES), F32)],
        compiler_params=_cparams("parallel", "arbitrary"),
        name="dsa_decode_score",
    )(page_table, qi, wi, kidx_new, *([cache_kidx_t] * PAGES_PER_STEP))


def _dsa_select_kernel(keys_ref, thr_ref, j0_ref, j0_s):
    rows, width = keys_ref.shape
    lane = lax.broadcasted_iota(I32, (rows, LANES), 1)

    def count(pred):
        def body(t, acc):
            off = pl.multiple_of(t * LANES, LANES)
            return acc + jnp.where(pred(keys_ref[:, pl.ds(off, LANES)], lane + off), 1, 0)

        acc = lax.fori_loop(0, width // LANES, body, jnp.zeros((rows, LANES), I32))
        return jnp.sum(acc, axis=1, keepdims=True)

    thr, j0 = _select_threshold(count, (rows, 1), width, j0_s, _kth_by_bit_search(count, (rows, 1)))
    thr_ref[...] = jnp.broadcast_to(thr, thr_ref.shape)
    j0_ref[...] = jnp.broadcast_to(j0, j0_ref.shape)


def _dsa_select(keys):
    rows, width = keys.shape
    const = lambda i: (0, 0)
    return pl.pallas_call(
        _dsa_select_kernel,
        grid=(1,),
        in_specs=[pl.BlockSpec((rows, width), const)],
        out_specs=[pl.BlockSpec((rows, LANES), const), pl.BlockSpec((rows, LANES), const)],
        out_shape=[jax.ShapeDtypeStruct((rows, LANES), F32), jax.ShapeDtypeStruct((rows, LANES), I32)],
        scratch_shapes=[pltpu.VMEM((rows, 1), I32)],
        compiler_params=_cparams("arbitrary"),
        name="dsa_decode_select",
    )(keys)


def _dsa_decode_kernel(pt_ref, q_ref, keys_ref, tail_ref, thr_ref, j0_ref, knew_ref, vnew_ref, *rest,
                       n_past, nkv):
    j = pl.program_id(1)
    nheads = q_ref.shape[0]
    rep = nheads // nkv
    kpages, vpages = rest[:PAGES_PER_STEP], rest[PAGES_PER_STEP:2 * PAGES_PER_STEP]
    o_ref, m_s, l_s, acc_s = rest[2 * PAGES_PER_STEP:]

    @pl.when(j == 0)
    def _():
        m_s[...] = jnp.full_like(m_s, NEG_BIG)
        l_s[...] = jnp.zeros_like(l_s)
        acc_s[...] = jnp.zeros_like(acc_s)

    q = q_ref[...]
    thr = thr_ref[:, 0:1]
    j0 = j0_ref[:, 0:1]
    width = PAGES_PER_STEP * PAGE * nkv
    row = lax.broadcasted_iota(I32, (1, width), 1) + j * width
    head_kv = lax.broadcasted_iota(I32, (nheads, 1), 0) // rep

    def update(att, value_fn):
        m_old = m_s[...]
        m_new = jnp.maximum(m_old, jnp.max(att, axis=1, keepdims=True))
        alpha = jnp.exp2(m_old - m_new)
        p = jnp.exp2(att - m_new)
        l_s[...] = alpha * l_s[...] + jnp.sum(p, axis=1, keepdims=True)
        acc_s[...] = alpha * acc_s[...] + value_fn(p)
        m_s[...] = m_new

    kcat = jnp.concatenate([r[...].astype(BF16) for r in kpages], axis=0)
    vcat = jnp.concatenate([r[...].astype(BF16) for r in vpages], axis=0)
    bias = _mask_bias(keys_ref[...], row // nkv, thr, j0)
    bias = jnp.where(row % nkv == head_kv, bias, NEG_BIG)
    update(_dot_nt(q.astype(BF16), kcat) + bias, lambda pr: _dot(pr.astype(BF16), vcat))

    @pl.when(j == pl.num_programs(1) - 1)
    def _():
        k_new = jnp.zeros_like(q)
        v_new = jnp.zeros_like(q)
        for g in range(nkv):
            k_new = jnp.where(head_kv == g, knew_ref[g:g + 1, :], k_new)
            v_new = jnp.where(head_kv == g, vnew_ref[g:g + 1, :], v_new)
        bias_new = _mask_bias(tail_ref[:, 0:1], jnp.int32(n_past), thr, j0)
        att_new = jnp.sum(q * k_new, axis=1, keepdims=True) + bias_new
        update(att_new, lambda pr: pr * v_new)
        o_ref[...] = (acc_s[...] / l_s[...]).astype(BF16)


def _dsa_decode(page_table, q, keys, tail, thr, j0, k_new, v_new, cache_k, cache_v):
    bsz, npages = page_table.shape
    nheads = q.shape[1]
    nkv = k_new.shape[1]
    steps = npages // PAGES_PER_STEP
    page_spec = lambda p: pl.BlockSpec(
        (PAGE * nkv, HEAD_DIM), lambda b, j, pt: (pt[b, j * PAGES_PER_STEP + p], 0))
    per_seq = lambda shape: pl.BlockSpec((None,) + shape, lambda b, j, pt: (b, 0, 0))
    keys_spec = pl.BlockSpec((None, 1, PAGES_PER_STEP * PAGE * nkv), lambda b, j, pt: (b, 0, j))
    return pl.pallas_call(
        functools.partial(_dsa_decode_kernel, n_past=npages * PAGE, nkv=nkv),
        grid_spec=pltpu.PrefetchScalarGridSpec(
            num_scalar_prefetch=1,
            grid=(bsz, steps),
            in_specs=[per_seq((nheads, HEAD_DIM)), keys_spec, per_seq((1, LANES)), per_seq((1, LANES)),
                      per_seq((1, LANES)), per_seq((nkv, HEAD_DIM)), per_seq((nkv, HEAD_DIM))]
                     + [page_spec(p) for p in range(PAGES_PER_STEP)] * 2,
            out_specs=per_seq((nheads, HEAD_DIM)),
            scratch_shapes=[pltpu.VMEM((nheads, 1), F32), pltpu.VMEM((nheads, 1), F32),
                            pltpu.VMEM((nheads, HEAD_DIM), F32)]),
        out_shape=jax.ShapeDtypeStruct((bsz, nheads, HEAD_DIM), BF16),
        compiler_params=_cparams("parallel", "arbitrary"),
        name="dsa_decode_attend",
    )(page_table, q, keys, tail, thr, j0, k_new, v_new,
      *([cache_k] * PAGES_PER_STEP), *([cache_v] * PAGES_PER_STEP))


def _pad_cols(w, width):
    return jnp.pad(w, ((0, 0), (0, width - w.shape[1])))


def _block_diag(blocks):
    g, r, c = blocks.shape
    eye = jnp.eye(g, dtype=blocks.dtype)
    return (blocks[:, :, None, :] * eye[:, None, :, None]).reshape(g * r, g * c)


def _s5_params(lam_re, lam_im, log_dt, b_re, b_im, c_re, c_im, d_skip, w_glu, b_glu):
    dt = jnp.exp(log_dt)[:, None]
    mag = jnp.exp(lam_re * dt)
    bar_re = mag * jnp.cos(lam_im * dt)
    bar_im = mag * jnp.sin(lam_im * dt)
    inv = 1.0 / (lam_re * lam_re + lam_im * lam_im)
    coef_re = (((bar_re - 1.0) * lam_re + bar_im * lam_im) * inv)[..., None]
    coef_im = ((bar_im * lam_re - (bar_re - 1.0) * lam_im) * inv)[..., None]
    bb_re = coef_re * b_re - coef_im * b_im
    bb_im = coef_re * b_im + coef_im * b_re
    lam_rows = jnp.stack([bar_re.reshape(-1), bar_im.reshape(-1)])
    to_in = lambda z: _block_diag(jnp.swapaxes(z, 1, 2))
    wb = jnp.concatenate([to_in(bb_re), to_in(bb_im)], axis=1).astype(BF16)
    to_out = lambda z: _block_diag(jnp.swapaxes(z, 1, 2))
    wc = jnp.concatenate([to_out(c_re), to_out(-c_im)], axis=0).astype(BF16)
    return (lam_rows, wb, wc, d_skip[None, :], w_glu.astype(BF16), b_glu[None, :])


def kernel(x_prompt, x_sample, state_s5_re, state_s5_im, state_mlstm_c, state_mlstm_n, state_mlstm_m,
           cache_k, cache_v, cache_kidx, page_table, norm_mix, norm_mlp, norm_final, w_in0, s5_lam_re,
           s5_lam_im, s5_log_dt, s5_b_re, s5_b_im, s5_c_re, s5_c_im, s5_d, w_glu, b_glu, b_igate,
           b_fgate, w_out0, w_in1, w_out1, w_up, w_down):
    bp, tp, d = x_prompt.shape
    db, ts, _ = x_sample.shape
    assert ts == 1, "the decode path handles one new token per sequence"
    s5_groups, s5_state = s5_lam_re.shape
    s5_width = s5_groups * S5_GROUP
    nstate = s5_groups * s5_state
    nh = b_igate.shape[0]
    ml_width = nh * HEAD_DIM
    assert s5_width == ml_width == 512 and d == 1024
    n_past = page_table.shape[1] * PAGE
    kvw = cache_k.shape[2] * cache_k.shape[3]
    nheads = w_out1.shape[0] // HEAD_DIM

    gate_cols = s5_width + 4 * ml_width
    w0 = jnp.concatenate([w_in0[:, :gate_cols], _pad_cols(w_in0[:, gate_cols:], LANES)], axis=1).astype(BF16)
    s5p = _s5_params(s5_lam_re, s5_lam_im, s5_log_dt, s5_b_re, s5_b_im, s5_c_re, s5_c_im, s5_d, w_glu, b_glu)
    gate_bias = jnp.concatenate([b_igate, b_fgate])
    bias_row, bias_col = gate_bias[None, :], gate_bias[:, None]
    ki0 = 1024 + 2 * kvw + IDX_HEADS * IDX_DIM
    w1 = jnp.concatenate([w_in1[:, :ki0], _pad_cols(w_in1[:, ki0:ki0 + IDX_DIM], LANES)], axis=1).astype(BF16)
    wvt = w_in1[:, 1024 + kvw:1024 + 2 * kvw].T.astype(BF16)
    wwit = w_in1[:, ki0 + IDX_DIM:].T.astype(BF16)
    wo0, wo1 = w_out0.astype(BF16), w_out1.astype(BF16)
    wup, wdn = w_up.astype(BF16), w_down.astype(BF16)
    g_mix, g_mlp, g_fin = norm_mix[:, None, :], norm_mlp[:, None, :], norm_final[None, :]

    def trunk(x2d, bsz, t, s5_h0, c0, n0, m0, attend):
        n = bsz * t
        z0 = _proj0(x2d, g_mix[0], w0)
        gates = z0[:, gate_cols:gate_cols + 2 * nh]
        m0b = jnp.broadcast_to(m0[:, :, None, None], (bsz, nh, 1, LANES))
        n0r = n0[:, :, None, :]
        if t == 1:
            y_s5, h_t = _s5_step(z0, s5_h0, s5p, s5_width)
            y_ml, c_t, n_t, m_t = _mlstm_step(z0.reshape(bsz, 1, -1), gates.reshape(bsz, 1, 2 * nh),
                                              bias_row, c0, n0r, m0b, nh)
        else:
            y_s5, h_t = _s5_seq(z0.reshape(bsz, t, -1), s5_h0.reshape(bsz, 1, -1), s5p, s5_width)
            g3 = gates.reshape(bsz, t, 2 * nh)
            y_ml, c_t, n_t, m_t = _mlstm_chunked(z0.reshape(bsz, t, -1), g3, jnp.swapaxes(g3, 1, 2),
                                                 bias_col, bias_row, c0, n0r, m0b, nh)
        h_t = h_t.reshape(bsz, 2, s5_groups, s5_state)
        states = (h_t[:, 0], h_t[:, 1], c_t, n_t[:, :, 0, :], m_t[:, :, 0, 0])
        h1 = _post(x2d, y_s5.reshape(n, s5_width), (y_ml.reshape(n, ml_width), 0), wo0, g_mlp[0],
                   wup[0], wdn[0], g_fin, final_norm=False)
        q, k, v, kbf, qi, kidx, kidxbf, vt, wit = _proj1(h1, g_mix[1], w1, wvt, wwit)
        o = attend(q, k, v, kbf, qi, kidx, kidxbf, vt, wit)
        y = _post(h1, o, (o, 1), wo1, g_mlp[1], wup[1], wdn[1], g_fin, final_norm=True)
        rows = (k.reshape(bsz, t, -1, HEAD_DIM), v.reshape(bsz, t, -1, HEAD_DIM), kidx.reshape(bsz, t, IDX_DIM))
        return y.reshape(bsz, t, d), states, rows

    def attend_prompt(q, k, v, kbf, qi, kidx, kidxbf, vt, wit):
        r3 = lambda z: z.reshape(bp, tp, -1)
        return _dsa_prompt(r3(q), r3(qi), wit, r3(kidxbf), r3(kbf), vt).reshape(bp * tp, -1)

    def attend_decode(q, k, v, kbf, qi, kidx, kidxbf, vt, wit):
        qi3 = qi.astype(F32).reshape(db, IDX_HEADS, IDX_DIM)
        wi3 = wit.T.reshape(db, IDX_HEADS, 1)
        keys, newkey = _dsa_score(page_table, qi3, wi3, kidx.reshape(db, 1, IDX_DIM),
                                  jnp.swapaxes(cache_kidx, 1, 2))
        tail = jnp.where(lax.broadcasted_iota(I32, (db, LANES), 1) == 0, newkey[:, 0, :], -jnp.inf)
        keys = jnp.concatenate([keys[:, 0, :], tail], axis=1)
        thr, j0 = _dsa_select(keys)
        nkv = kvw // HEAD_DIM
        keys_rows = jnp.repeat(keys[:, :n_past], nkv, axis=1)[:, None, :]
        o = _dsa_decode(page_table, q.astype(F32).reshape(db, nheads, HEAD_DIM), keys_rows,
                        keys[:, None, n_past:], thr[:, None, :], j0[:, None, :],
                        k.reshape(db, nkv, HEAD_DIM), v.reshape(db, nkv, HEAD_DIM),
                        cache_k.reshape(-1, HEAD_DIM), cache_v.reshape(-1, HEAD_DIM))
        return o.reshape(db, nheads * HEAD_DIM)

    zeros = lambda *shape: jnp.zeros(shape, F32)
    y_p, st_p, rows_p = trunk(x_prompt.reshape(bp * tp, d), bp, tp, zeros(bp, 2 * nstate),
                              zeros(bp, nh, HEAD_DIM, HEAD_DIM), zeros(bp, nh, HEAD_DIM), zeros(bp, nh),
                              attend_prompt)
    s5_h0 = jnp.concatenate([state_s5_re.reshape(db, nstate), state_s5_im.reshape(db, nstate)], axis=1)
    y_s, st_s, rows_s = trunk(x_sample.reshape(db, d), db, 1, s5_h0, state_mlstm_c, state_mlstm_n,
                              state_mlstm_m, attend_decode)
    return (y_p, y_s) + st_p + rows_p + st_s + rows_s
```

```python
import functools
import math

import jax
import jax.numpy as jnp
from jax import lax
from jax.experimental import pallas as pl
from jax.experimental.pallas import tpu as pltpu

F32, BF16, I32 = jnp.float32, jnp.bfloat16, jnp.int32

EPS = 1e-6
LANES = 128
PAGE = 128
S5_GROUP = 16
S5_STATE = 64
HEAD_DIM = 128
IDX_DIM = 64
IDX_HEADS = 8
TOPK = 256
Q_BLOCK = 128
KEY_CHUNK = 1024
COUNT_CHUNK = 512
SORT_TILES = 256
MLSTM_CHUNK = 128
S5_CHUNK = 256
ROW_TILE = 512
FF_CHUNK = 4096
PAGES_PER_STEP = 32
INT_MIN = -2 ** 31
NEG_BIG = -1e30
VMEM_LIMIT = 56 * 1024 * 1024


def _cparams(*sem):
    return pltpu.CompilerParams(dimension_semantics=sem, vmem_limit_bytes=VMEM_LIMIT)


def _rms(x, g):
    return x * lax.rsqrt(jnp.mean(x * x, axis=-1, keepdims=True) + EPS) * g


def _dot(a, b):
    return jnp.dot(a, b, preferred_element_type=F32)


def _dot_nt(a, b):
    return lax.dot_general(a, b, (((1,), (1,)), ((), ())), preferred_element_type=F32)


def _dot_tn(a, b):
    return lax.dot_general(a, b, (((0,), (0,)), ((), ())), preferred_element_type=F32)


def _proj0_kernel(x_ref, g_ref, w_ref, o_ref):
    xn = _rms(x_ref[...], g_ref[...]).astype(BF16)
    o_ref[...] = _dot(xn, w_ref[...])


def _proj0(x, g, w):
    n, d = x.shape
    tm = min(ROW_TILE, n)
    wtot = w.shape[1]
    return pl.pallas_call(
        _proj0_kernel,
        grid=(n // tm,),
        in_specs=[pl.BlockSpec((tm, d), lambda i: (i, 0)),
                  pl.BlockSpec((1, d), lambda i: (0, 0)),
                  pl.BlockSpec((d, wtot), lambda i: (0, 0))],
        out_specs=pl.BlockSpec((tm, wtot), lambda i: (i, 0)),
        out_shape=jax.ShapeDtypeStruct((n, wtot), F32),
        compiler_params=_cparams("parallel"),
        name="proj0",
    )(x, g, w)


_Q1, _K1, _V1, _QI1, _KI1, _END1 = 0, 1024, 1280, 1536, 2048, 2176
LOG2E = 1.4426950408889634


def _proj1_kernel(x_ref, g_ref, w_ref, wvt_ref, wwit_ref, q_ref, k_ref, v_ref, kbf_ref, qi_ref,
                  kidx_ref, kidxbf_ref, vt_ref, wit_ref):
    xn = _rms(x_ref[...], g_ref[...]).astype(BF16)
    z = _dot(xn, w_ref[...])
    q_ref[...] = (z[:, _Q1:_K1] * (HEAD_DIM ** -0.5 * LOG2E)).astype(BF16)
    k = z[:, _K1:_V1]
    k_ref[...] = k
    v_ref[...] = z[:, _V1:_QI1]
    kbf_ref[...] = k.astype(BF16)
    qi_ref[...] = z[:, _QI1:_KI1].astype(BF16)
    kidx = z[:, _KI1:_KI1 + IDX_DIM]
    kidx_ref[...] = kidx
    kidxbf_ref[...] = kidx.astype(BF16)
    vt_ref[...] = _dot_nt(wvt_ref[...], xn).astype(BF16)
    wit_ref[...] = _dot_nt(wwit_ref[...], xn) * ((IDX_DIM ** -0.5) * (IDX_HEADS ** -0.5))


def _proj1(x, g, w, wvt, wwit):
    n, d = x.shape
    tm = min(ROW_TILE, n)
    kvw = wvt.shape[0]
    row = lambda width: pl.BlockSpec((tm, width), lambda i: (i, 0))
    col = lambda height: pl.BlockSpec((height, tm), lambda i: (0, i))
    full = lambda a: pl.BlockSpec(a.shape, lambda i: (0, 0))
    shp = lambda width, dt: jax.ShapeDtypeStruct((n, width), dt)
    return pl.pallas_call(
        _proj1_kernel,
        grid=(n // tm,),
        in_specs=[row(d), pl.BlockSpec((1, d), lambda i: (0, 0)), full(w), full(wvt), full(wwit)],
        out_specs=[row(1024), row(kvw), row(kvw), row(kvw), row(512), row(IDX_DIM), row(IDX_DIM),
                   col(kvw), col(IDX_HEADS)],
        out_shape=[shp(1024, BF16), shp(kvw, F32), shp(kvw, F32), shp(kvw, BF16), shp(512, BF16),
                   shp(IDX_DIM, F32), shp(IDX_DIM, BF16),
                   jax.ShapeDtypeStruct((kvw, n), BF16), jax.ShapeDtypeStruct((IDX_HEADS, n), F32)],
        compiler_params=_cparams("parallel"),
        name="proj1",
    )(x, g, w, wvt, wwit)


def _post_kernel(h_ref, ya_ref, yb_ref, wo_ref, g_ref, wup_ref, wdn_ref, gf_ref, out_ref,
                 h1_s, xn_s, acc_s, *, final_norm):
    j = pl.program_id(1)
    half = ya_ref.shape[1]

    @pl.when(j == 0)
    def _():
        h1 = h_ref[...] + _dot(ya_ref[...], wo_ref[:half, :]) + _dot(yb_ref[...], wo_ref[half:, :])
        h1_s[...] = h1
        xn_s[...] = _rms(h1, g_ref[...]).astype(BF16)
        acc_s[...] = jnp.zeros_like(acc_s)

    r = jnp.maximum(_dot(xn_s[...], wup_ref[...]), 0.0)
    acc_s[...] += _dot((r * r).astype(BF16), wdn_ref[...])

    @pl.when(j == pl.num_programs(1) - 1)
    def _():
        o = h1_s[...] + acc_s[...]
        if final_norm:
            o = _rms(o, gf_ref[...])
        out_ref[...] = o


def _post(h, ya, yb_spec_arg, wo, g, wup, wdn, gf, *, final_norm):
    n, d = h.shape
    tm = min(ROW_TILE, n)
    yb, yb_col = yb_spec_arg
    half = d // 2
    dff = wup.shape[1]
    once = pl.Buffered(1) if dff == FF_CHUNK else None
    return pl.pallas_call(
        functools.partial(_post_kernel, final_norm=final_norm),
        grid=(n // tm, dff // FF_CHUNK),
        in_specs=[pl.BlockSpec((tm, d), lambda i, j: (i, 0)),
                  pl.BlockSpec((tm, half), lambda i, j: (i, 0)),
                  pl.BlockSpec((tm, half), lambda i, j: (i, yb_col)),
                  pl.BlockSpec((d, d), lambda i, j: (0, 0), pipeline_mode=once),
                  pl.BlockSpec((1, d), lambda i, j: (0, 0)),
                  pl.BlockSpec((d, FF_CHUNK), lambda i, j: (0, j), pipeline_mode=once),
                  pl.BlockSpec((FF_CHUNK, d), lambda i, j: (j, 0), pipeline_mode=once),
                  pl.BlockSpec((1, d), lambda i, j: (0, 0))],
        out_specs=pl.BlockSpec((tm, d), lambda i, j: (i, 0)),
        out_shape=jax.ShapeDtypeStruct((n, d), F32),
        scratch_shapes=[pltpu.VMEM((tm, d), F32), pltpu.VMEM((tm, d), BF16), pltpu.VMEM((tm, d), F32)],
        compiler_params=_cparams("parallel", "arbitrary"),
        name="post_final" if final_norm else "post",
    )(h, ya, yb, wo, g, wup, wdn, gf)


S5_SPLIT = 2


def _s5_input(u, wb_ref):
    width, nstate = wb_ref.shape[0], wb_ref.shape[1] // 2
    wi, ns = width // S5_SPLIT, nstate // S5_SPLIT
    u_bf = u.astype(BF16)
    part = lambda base: jnp.concatenate(
        [_dot(u_bf[:, r * wi:(r + 1) * wi], wb_ref[r * wi:(r + 1) * wi, base + r * ns:base + (r + 1) * ns])
         for r in range(S5_SPLIT)], axis=1)
    return part(0), part(nstate)


def _s5_output(hre, him, u, wc_ref, d_ref, wg_ref, bg_ref):
    nstate, width = hre.shape[1], wc_ref.shape[1]
    wi, ns = width // S5_SPLIT, nstate // S5_SPLIT
    hre_bf, him_bf = hre.astype(BF16), him.astype(BF16)
    y = jnp.concatenate(
        [_dot(hre_bf[:, r * ns:(r + 1) * ns], wc_ref[r * ns:(r + 1) * ns, r * wi:(r + 1) * wi])
         + _dot(him_bf[:, r * ns:(r + 1) * ns], wc_ref[nstate + r * ns:nstate + (r + 1) * ns, r * wi:(r + 1) * wi])
         for r in range(S5_SPLIT)], axis=1)
    y = jax.nn.gelu(y + d_ref[...] * u)
    gate = jax.nn.sigmoid(_dot(y.astype(BF16), wg_ref[...]) + bg_ref[...])
    return (y * gate).astype(BF16)


def _s5_seq_kernel(u_ref, h0_ref, lam_ref, wb_ref, wc_ref, d_ref, wg_ref, bg_ref, y_ref, ht_ref,
                   hre_s, him_s, carry_s):
    bsz, _, half = hre_s.shape

    @pl.when(pl.program_id(0) == 0)
    def _():
        carry_s[...] = h0_ref[...]

    for b in range(bsz):
        hre_s[b], him_s[b] = _s5_input(u_ref[b], wb_ref)
    a_re = lam_ref[0:1, :]
    a_im = lam_ref[1:2, :]

    def step(t, carry):
        out = []
        for b in range(bsz):
            h_re, h_im = carry[2 * b], carry[2 * b + 1]
            n_re = a_re * h_re - a_im * h_im + hre_s[b, pl.ds(t, 1), :]
            n_im = a_re * h_im + a_im * h_re + him_s[b, pl.ds(t, 1), :]
            hre_s[b, pl.ds(t, 1), :] = n_re
            him_s[b, pl.ds(t, 1), :] = n_im
            out += [n_re, n_im]
        return tuple(out)

    init = tuple(carry_s[b][:, lo:lo + half] for b in range(bsz) for lo in (0, half))
    final = lax.fori_loop(0, hre_s.shape[1], step, init, unroll=8)
    for b in range(bsz):
        carry_s[b, :, :half] = final[2 * b]
        carry_s[b, :, half:] = final[2 * b + 1]
        y_ref[b] = _s5_output(hre_s[b], him_s[b], u_ref[b], wc_ref, d_ref, wg_ref, bg_ref)
    ht_ref[...] = carry_s[...]


def _s5_step_kernel(u_ref, h0_ref, lam_ref, wb_ref, wc_ref, d_ref, wg_ref, bg_ref, y_ref, ht_ref):
    half = lam_ref.shape[1]
    u = u_ref[...]
    bu_re, bu_im = _s5_input(u, wb_ref)
    a_re = lam_ref[0:1, :]
    a_im = lam_ref[1:2, :]
    h_re = h0_ref[:, :half]
    h_im = h0_ref[:, half:]
    n_re = a_re * h_re - a_im * h_im + bu_re
    n_im = a_re * h_im + a_im * h_re + bu_im
    ht_ref[:, :half] = n_re
    ht_ref[:, half:] = n_im
    y_ref[...] = _s5_output(n_re, n_im, u, wc_ref, d_ref, wg_ref, bg_ref)


def _s5_param_specs(width, nstate, imap):
    return [pl.BlockSpec((2, nstate), imap),
            pl.BlockSpec((width, 2 * nstate), imap),
            pl.BlockSpec((2 * nstate, width), imap),
            pl.BlockSpec((1, width), imap),
            pl.BlockSpec((width, width), imap),
            pl.BlockSpec((1, width), imap)]


def _s5_seq(z0, h0, params, width):
    bsz, t, _ = z0.shape
    nstate = params[0].shape[1]
    ts = min(S5_CHUNK, t)
    const = lambda c: (0, 0)
    return pl.pallas_call(
        _s5_seq_kernel,
        grid=(t // ts,),
        in_specs=[pl.BlockSpec((bsz, ts, width), lambda c: (0, c, 0)),
                  pl.BlockSpec((bsz, 1, 2 * nstate), lambda c: (0, 0, 0))]
                 + _s5_param_specs(width, nstate, const),
        out_specs=[pl.BlockSpec((bsz, ts, width), lambda c: (0, c, 0)),
                   pl.BlockSpec((bsz, 1, 2 * nstate), lambda c: (0, 0, 0))],
        out_shape=[jax.ShapeDtypeStruct((bsz, t, width), BF16),
                   jax.ShapeDtypeStruct((bsz, 1, 2 * nstate), F32)],
        scratch_shapes=[pltpu.VMEM((bsz, ts, nstate), F32), pltpu.VMEM((bsz, ts, nstate), F32),
                        pltpu.VMEM((bsz, 1, 2 * nstate), F32)],
        compiler_params=_cparams("arbitrary"),
        name="s5_scan",
    )(z0, h0, *params)


def _s5_step(z0, h0, params, width):
    rows = z0.shape[0]
    nstate = params[0].shape[1]
    const = lambda i: (0, 0)
    return pl.pallas_call(
        _s5_step_kernel,
        grid=(1,),
        in_specs=[pl.BlockSpec((rows, width), const), pl.BlockSpec((rows, 2 * nstate), const)]
                 + _s5_param_specs(width, nstate, const),
        out_specs=[pl.BlockSpec((rows, width), const), pl.BlockSpec((rows, 2 * nstate), const)],
        out_shape=[jax.ShapeDtypeStruct((rows, width), BF16),
                   jax.ShapeDtypeStruct((rows, 2 * nstate), F32)],
        compiler_params=_cparams("arbitrary"),
        name="s5_step",
    )(z0, h0, *params)


def _mlstm_chunk_kernel(q_ref, k_ref, v_ref, o_ref, gcol_ref, grow_ref, bcol_ref, brow_ref,
                        c0_ref, n0_ref, m0_ref, y_ref, c_ref, n_ref, m_ref):
    nh = c_ref.shape[0]
    ch = q_ref.shape[0]

    @pl.when(pl.program_id(1) == 0)
    def _():
        c_ref[...] = c0_ref[...]
        n_ref[...] = n0_ref[...]
        m_ref[...] = m0_ref[...]

    gcol = gcol_ref[...] + brow_ref[...]
    grow = grow_ref[...] + bcol_ref[...]
    t_idx = lax.broadcasted_iota(I32, (ch, ch), 0)
    s_idx = lax.broadcasted_iota(I32, (ch, ch), 1)
    causal = t_idx >= s_idx
    heads = range(nh)
    sl = [slice(h * HEAD_DIM, (h + 1) * HEAD_DIM) for h in heads]
    q = [q_ref[:, sl[h]] for h in heads]
    k = [k_ref[:, sl[h]] * (HEAD_DIM ** -0.5) for h in heads]
    v_bf = [v_ref[:, sl[h]].astype(BF16) for h in heads]
    q_bf = [x.astype(BF16) for x in q]
    k_bf = [x.astype(BF16) for x in k]
    i_col = [gcol[:, h:h + 1] for h in heads]
    i_row = [grow[h:h + 1, :] for h in heads]
    lf_col = [jax.nn.log_sigmoid(gcol[:, nh + h:nh + h + 1]) for h in heads]
    lf_row = [jax.nn.log_sigmoid(grow[nh + h:nh + h + 1, :]) for h in heads]
    b_col = [jnp.sum(jnp.where(causal, lf_row[h], 0.0), axis=1, keepdims=True) for h in heads]
    b_row = [jnp.sum(jnp.where(causal, 0.0, lf_col[h]), axis=0, keepdims=True) + lf_row[h] for h in heads]
    m_prev = [m_ref[h][:, 0:1] for h in heads]
    dmat = [jnp.where(causal, b_col[h] - b_row[h] + i_row[h], -jnp.inf) for h in heads]
    a_col = [b_col[h] + m_prev[h] for h in heads]
    mj = [jnp.maximum(a_col[h], jnp.max(dmat[h], axis=1, keepdims=True)) for h in heads]
    w_intra = [jnp.exp(dmat[h] - mj[h]) for h in heads]
    w_inter = [jnp.exp(a_col[h] - mj[h]) for h in heads]
    s = [_dot_nt(q_bf[h], k_bf[h]) * w_intra[h] for h in heads]
    c_prev = [c_ref[h] for h in heads]
    n_prev = [n_ref[h] for h in heads]
    num = [_dot(s[h].astype(BF16), v_bf[h]) + w_inter[h] * _dot(q_bf[h], c_prev[h].astype(BF16)) for h in heads]
    den = [jnp.sum(s[h], axis=1, keepdims=True) + w_inter[h] * jnp.sum(q[h] * n_prev[h], axis=1, keepdims=True)
           for h in heads]
    for h in heads:
        hout = num[h] / jnp.maximum(jnp.abs(den[h]), jnp.exp(-mj[h]))
        y_ref[:, sl[h]] = (jax.nn.sigmoid(o_ref[:, sl[h]]) * hout).astype(BF16)
    m_new = [mj[h][ch - 1:ch, :] for h in heads]
    b_last = [b_col[h][ch - 1:ch, :] for h in heads]
    kw = [k[h] * jnp.exp(b_last[h] - b_col[h] + i_col[h] - m_new[h]) for h in heads]
    decay = [jnp.exp(b_last[h] + m_prev[h] - m_new[h]) for h in heads]
    for h in heads:
        c_ref[h] = decay[h] * c_prev[h] + _dot_tn(kw[h].astype(BF16), v_bf[h])
        n_ref[h] = decay[h] * n_prev[h] + jnp.sum(kw[h], axis=0, keepdims=True)
        m_ref[h] = jnp.broadcast_to(m_new[h], (1, LANES))


def _mlstm_chunked(z0, gcol, grow, bias_col, bias_row, c0, n0, m0, nh):
    bsz, t, _ = z0.shape
    width = nh * HEAD_DIM
    ch = MLSTM_CHUNK
    zspec = lambda blk: pl.BlockSpec((None, ch, width), lambda b, c: (b, c, blk))
    state = lambda shape: pl.BlockSpec((None,) + shape, lambda b, c: (b,) + (0,) * len(shape))
    return pl.pallas_call(
        _mlstm_chunk_kernel,
        grid=(bsz, t // ch),
        in_specs=[zspec(1), zspec(2), zspec(3), zspec(4),
                  pl.BlockSpec((None, ch, 2 * nh), lambda b, c: (b, c, 0)),
                  pl.BlockSpec((None, 2 * nh, ch), lambda b, c: (b, 0, c)),
                  pl.BlockSpec((2 * nh, 1), lambda b, c: (0, 0)),
                  pl.BlockSpec((1, 2 * nh), lambda b, c: (0, 0)),
                  state((nh, HEAD_DIM, HEAD_DIM)), state((nh, 1, HEAD_DIM)), state((nh, 1, LANES))],
        out_specs=[pl.BlockSpec((None, ch, width), lambda b, c: (b, c, 0)),
                   state((nh, HEAD_DIM, HEAD_DIM)), state((nh, 1, HEAD_DIM)), state((nh, 1, LANES))],
        out_shape=[jax.ShapeDtypeStruct((bsz, t, width), BF16),
                   jax.ShapeDtypeStruct((bsz, nh, HEAD_DIM, HEAD_DIM), F32),
                   jax.ShapeDtypeStruct((bsz, nh, 1, HEAD_DIM), F32),
                   jax.ShapeDtypeStruct((bsz, nh, 1, LANES), F32)],
        compiler_params=_cparams("parallel", "arbitrary"),
        name="mlstm_chunk",
    )(z0, z0, z0, z0, gcol, grow, bias_col, bias_row, c0, n0, m0)


def _to_column(row):
    n = row.shape[1]
    eye = lax.broadcasted_iota(I32, (n, n), 0) == lax.broadcasted_iota(I32, (n, n), 1)
    return jnp.sum(jnp.where(eye, row, 0.0), axis=1, keepdims=True)


def _mlstm_step_kernel(q_ref, k_ref, v_ref, o_ref, g_ref, brow_ref, c0_ref, n0_ref, m0_ref,
                       y_ref, c_ref, n_ref, m_ref):
    nh = c_ref.shape[0]
    g = g_ref[...] + brow_ref[...]
    for h in range(nh):
        sl = slice(h * HEAD_DIM, (h + 1) * HEAD_DIM)
        q = q_ref[:, sl]
        k = k_ref[:, sl] * (HEAD_DIM ** -0.5)
        v = v_ref[:, sl]
        i_pre = g[:, h:h + 1]
        lf = jax.nn.log_sigmoid(g[:, nh + h:nh + h + 1])
        m_prev = m0_ref[h][:, 0:1]
        c_prev = c0_ref[h]
        n_prev = n0_ref[h]
        a = lf + m_prev
        mj = jnp.maximum(a, i_pre)
        w_intra = jnp.exp(i_pre - mj)
        w_inter = jnp.exp(a - mj)
        s = jnp.sum(q * k, axis=1, keepdims=True) * w_intra
        q_col = _to_column(q)
        k_col = _to_column(k)
        num = s * v + w_inter * jnp.sum(q_col * c_prev, axis=0, keepdims=True)
        den = s + w_inter * jnp.sum(q * n_prev, axis=1, keepdims=True)
        hout = num / jnp.maximum(jnp.abs(den), jnp.exp(-mj))
        y_ref[:, sl] = (jax.nn.sigmoid(o_ref[:, sl]) * hout).astype(BF16)
        w_end = jnp.exp(i_pre - mj)
        decay = jnp.exp(a - mj)
        c_ref[h] = decay * c_prev + (w_end * k_col) * v
        n_ref[h] = decay * n_prev + w_end * k
        m_ref[h] = jnp.broadcast_to(mj, (1, LANES))


def _mlstm_step(z0, g, bias_row, c0, n0, m0, nh):
    bsz = z0.shape[0]
    width = nh * HEAD_DIM
    zspec = lambda blk: pl.BlockSpec((None, 1, width), lambda b: (b, 0, blk))
    state = lambda shape: pl.BlockSpec((None,) + shape, lambda b: (b,) + (0,) * len(shape))
    return pl.pallas_call(
        _mlstm_step_kernel,
        grid=(bsz,),
        in_specs=[zspec(1), zspec(2), zspec(3), zspec(4),
                  pl.BlockSpec((None, 1, 2 * nh), lambda b: (b, 0, 0)),
                  pl.BlockSpec((1, 2 * nh), lambda b: (0, 0)),
                  state((nh, HEAD_DIM, HEAD_DIM)), state((nh, 1, HEAD_DIM)), state((nh, 1, LANES))],
        out_specs=[pl.BlockSpec((None, 1, width), lambda b: (b, 0, 0)),
                   state((nh, HEAD_DIM, HEAD_DIM)), state((nh, 1, HEAD_DIM)), state((nh, 1, LANES))],
        out_shape=[jax.ShapeDtypeStruct((bsz, 1, width), BF16),
                   jax.ShapeDtypeStruct((bsz, nh, HEAD_DIM, HEAD_DIM), F32),
                   jax.ShapeDtypeStruct((bsz, nh, 1, HEAD_DIM), F32),
                   jax.ShapeDtypeStruct((bsz, nh, 1, LANES), F32)],
        compiler_params=_cparams("parallel"),
        name="mlstm_step",
    )(z0, z0, z0, z0, g, bias_row, c0, n0, m0)


def _key_to_float(key):
    bits = key ^ ((key >> 31) & jnp.int32(0x7FFFFFFF))
    return lax.bitcast_convert_type(bits, F32)


def _kth_by_bit_search(count, shape):
    def bit_step(it, key):
        cand = key + (jnp.int32(1) << (31 - it))
        cand_f = _key_to_float(cand)
        cnt = count(lambda tile, col: tile >= cand_f)
        return jnp.where(cnt >= TOPK, cand, key)

    key = lax.fori_loop(0, 32, bit_step, jnp.full(shape, INT_MIN, I32))
    has_thr = key > INT_MIN
    return jnp.where(has_thr, _key_to_float(jnp.where(has_thr, key, 0)), -jnp.inf)


def _select_threshold(count, shape, width, j0_s, thr):
    has_thr = thr > -jnp.inf
    need = TOPK - count(lambda tile, col: tile > thr)
    n_eq = count(lambda tile, col: tile == thr)
    j0_s[...] = jnp.where(has_thr, jnp.int32(width), jnp.int32(-1))
    surplus = jnp.max(jnp.where(has_thr & (n_eq > need), 1, 0))
    nbits = max(1, (width - 1).bit_length())

    @pl.when(surplus > 0)
    def _():
        def idx_step(it, j0):
            cand = j0 | (jnp.int32(1) << (nbits - 1 - it))
            cnt = count(lambda tile, col: (tile == thr) & (col < cand))
            return jnp.where(cnt < need, cand, j0)

        j0 = lax.fori_loop(0, nbits, idx_step, jnp.zeros(shape, I32))
        j0_s[...] = jnp.where(has_thr, j0, jnp.int32(-1))

    return thr, j0_s[...]


def _network_pass(load, store, stages, groups):
    for grp in groups:
        vals = [load(i) for i in grp]
        pos = {gi: n for n, gi in enumerate(grp)}
        for size, dist in stages:
            for gi in grp:
                gl = gi ^ dist
                if gl > gi:
                    a, b = pos[gi], pos[gl]
                    hi, lo = jnp.maximum(vals[a], vals[b]), jnp.minimum(vals[a], vals[b])
                    vals[a], vals[b] = (hi, lo) if (gi & size) == 0 else (lo, hi)
        for n, gi in enumerate(grp):
            store(gi, vals[n])


_NET_GROUP = 16
_LOW_GROUPS = [[m * _NET_GROUP + t for t in range(_NET_GROUP)] for m in range(SORT_TILES // _NET_GROUP)]
_HIGH_GROUPS = [[m + (SORT_TILES // _NET_GROUP) * t for t in range(_NET_GROUP)]
                for m in range(SORT_TILES // _NET_GROUP)]


def _tile(ref, base, i):
    return ref.at[pl.ds(base + 8 * i, 8), :]


def _sort_block_desc(src, src_base, dst):
    assert SORT_TILES == 256 and _NET_GROUP == 16
    low = lambda size: [(size, d) for d in (8, 4, 2, 1) if d < size]
    first = [st for size in (2, 4, 8, 16) for st in low(size)]
    _network_pass(lambda i: _tile(src, src_base, i)[...],
                  lambda i, v: _tile(dst, 0, i).__setitem__(Ellipsis, v), first, _LOW_GROUPS)
    ld = lambda i: _tile(dst, 0, i)[...]
    st = lambda i, v: _tile(dst, 0, i).__setitem__(Ellipsis, v)
    for size in (32, 64, 128, 256):
        _network_pass(ld, st, [(size, d) for d in (128, 64, 32, 16) if d < size], _HIGH_GROUPS)
        _network_pass(ld, st, low(size), _LOW_GROUPS)


def _merge_top(run, other, tmp, shift=None):
    def ld(i):
        o = _tile(other, 0, SORT_TILES - 1 - i)[...]
        if shift is not None:
            o = pltpu.roll(o, shift, axis=0)
        return jnp.maximum(_tile(run, 0, i)[...], o)

    _network_pass(ld, lambda i, v: _tile(tmp, 0, i).__setitem__(Ellipsis, v),
                  [(SORT_TILES, d) for d in (128, 64, 32, 16)], _HIGH_GROUPS)
    _network_pass(lambda i: _tile(tmp, 0, i)[...], lambda i, v: _tile(run, 0, i).__setitem__(Ellipsis, v),
                  [(SORT_TILES, d) for d in (8, 4, 2, 1)], _LOW_GROUPS)


def _mask_bias(scores, cols, thr, j0):
    sel = (scores > thr) | ((scores == thr) & (cols <= j0))
    return jnp.where(sel, 0.0, NEG_BIG)


def _dsa_prompt_kernel(q_ref, qi_ref, wit_ref, ki_ref, k_ref, vt_ref, o_ref, sc_s, j0_s, m_s, acc_s,
                       run_s, blk_s, tmp_s):
    qb = q_ref.shape[0]
    kc_len = KEY_CHUNK
    nheads = q_ref.shape[1] // HEAD_DIM
    nkv = k_ref.shape[1] // HEAD_DIM
    rep = nheads // nkv
    i = pl.program_id(1)
    nchunks = ((i + 1) * qb + kc_len - 1) // kc_len
    qpos = i * qb + lax.broadcasted_iota(I32, (1, qb), 1)
    sub = lax.broadcasted_iota(I32, (kc_len, qb), 0)

    qi = qi_ref[...]
    wit = wit_ref[...]
    qi_all = jnp.concatenate([qi[:, h * IDX_DIM:(h + 1) * IDX_DIM] for h in range(IDX_HEADS)], axis=0)

    def score_chunk(c, _):
        off = pl.multiple_of(c * kc_len, kc_len)
        ki = ki_ref[pl.ds(off, kc_len), :]
        logits = _dot_nt(ki, qi_all)
        sc = jnp.zeros((kc_len, qb), F32)
        for h in range(IDX_HEADS):
            sc = sc + jnp.maximum(logits[:, h * qb:(h + 1) * qb], 0.0) * wit[h:h + 1, :]
        sc_s[pl.ds(off, kc_len), :] = jnp.where(sub + off <= qpos, sc, -jnp.inf)
        return 0

    lax.fori_loop(0, nchunks, score_chunk, 0)

    sort_rows = 8 * SORT_TILES
    nblocks = ((i + 1) * qb + sort_rows - 1) // sort_rows

    @pl.when(nchunks * kc_len < nblocks * sort_rows)
    def _():
        sc_s[pl.ds(pl.multiple_of(nchunks * kc_len, kc_len), kc_len), :] = jnp.full((kc_len, qb), -jnp.inf, F32)

    run_s[...] = jnp.full_like(run_s, -jnp.inf)

    def sort_block(b, _):
        _sort_block_desc(sc_s, pl.multiple_of(b * sort_rows, sort_rows), blk_s)
        _merge_top(run_s, blk_s, tmp_s)
        return 0

    lax.fori_loop(0, nblocks, sort_block, 0)
    for shift in (4, 2):
        _merge_top(run_s, run_s, tmp_s, shift)
    top = [jnp.maximum(_tile(run_s, 0, t)[...], pltpu.roll(_tile(run_s, 0, SORT_TILES - 1 - t)[...], 1, axis=0))
           for t in range(SORT_TILES)]
    while len(top) > 1:
        top = [jnp.minimum(a, b) for a, b in zip(top[0::2], top[1::2])]
    kth = top[0][0:1, :]

    def count(pred):
        def body(c, acc):
            off = pl.multiple_of(c * COUNT_CHUNK, COUNT_CHUNK)
            ind = jnp.where(pred(sc_s[pl.ds(off, COUNT_CHUNK), :], sub[:COUNT_CHUNK] + off), 1, 0)
            return acc + jnp.sum(ind.reshape(COUNT_CHUNK // 8, 8, qb), axis=0)

        ncount = ((i + 1) * qb + COUNT_CHUNK - 1) // COUNT_CHUNK
        acc = lax.fori_loop(0, ncount, body, jnp.zeros((8, qb), I32))
        return jnp.sum(acc, axis=0, keepdims=True)

    thr, j0 = _select_threshold(count, (1, qb), sc_s.shape[0], j0_s, kth)

    m_s[...] = jnp.full_like(m_s, NEG_BIG)
    acc_s[...] = jnp.zeros_like(acc_s)
    q = q_ref[...]
    q_g = [jnp.concatenate([q[:, (g * rep + r) * HEAD_DIM:(g * rep + r + 1) * HEAD_DIM]
                            for r in range(rep)], axis=0) for g in range(nkv)]
    ones_rows = jnp.ones((acc_s.shape[1] - HEAD_DIM, kc_len), BF16)

    def attend_chunk(c, _):
        off = pl.multiple_of(c * kc_len, kc_len)
        bias = _mask_bias(sc_s[pl.ds(off, kc_len), :], sub + off, thr, j0)
        bias = jnp.concatenate([bias] * rep, axis=1)
        groups = range(nkv)
        att = [_dot_nt(k_ref[pl.ds(off, kc_len), g * HEAD_DIM:(g + 1) * HEAD_DIM], q_g[g]) + bias
               for g in groups]
        m_old = [m_s[g] for g in groups]
        m_new = [jnp.maximum(m_old[g], jnp.max(att[g], axis=0, keepdims=True)) for g in groups]
        p = [jnp.exp2(att[g] - m_new[g]).astype(BF16) for g in groups]
        for g in groups:
            vt = jnp.concatenate([vt_ref[g * HEAD_DIM:(g + 1) * HEAD_DIM, pl.ds(off, kc_len)], ones_rows],
                                 axis=0)
            acc_s[g] = jnp.exp2(m_old[g] - m_new[g]) * acc_s[g] + _dot(vt, p[g])
            m_s[g] = m_new[g]
        return 0

    lax.fori_loop(0, nchunks, attend_chunk, 0)

    for g in range(nkv):
        acc = acc_s[g]
        out = acc[:HEAD_DIM, :] / acc[HEAD_DIM:HEAD_DIM + 1, :]
        for r in range(rep):
            hd = g * rep + r
            o_ref[:, hd * HEAD_DIM:(hd + 1) * HEAD_DIM] = out[:, r * qb:(r + 1) * qb].T.astype(BF16)


def _dsa_prompt(q, qi, wit, kidx, kbf, vt):
    bsz, t, width = q.shape
    kvw = kbf.shape[2]
    nkv = kvw // HEAD_DIM
    rep = width // HEAD_DIM // nkv
    qb = Q_BLOCK
    nq = t // qb
    assert t % (8 * SORT_TILES) == 0 and 8 * SORT_TILES == 2 * KEY_CHUNK and TOPK == SORT_TILES
    ones_rows = 16
    return pl.pallas_call(
        _dsa_prompt_kernel,
        grid=(bsz, nq),
        in_specs=[pl.BlockSpec((None, qb, width), lambda b, i: (b, i, 0)),
                  pl.BlockSpec((None, qb, qi.shape[2]), lambda b, i: (b, i, 0)),
                  pl.BlockSpec((IDX_HEADS, qb), lambda b, i: (0, b * nq + i)),
                  pl.BlockSpec((None, t, IDX_DIM), lambda b, i: (b, 0, 0)),
                  pl.BlockSpec((None, t, kvw), lambda b, i: (b, 0, 0)),
                  pl.BlockSpec((kvw, t), lambda b, i: (0, b))],
        out_specs=pl.BlockSpec((None, qb, width), lambda b, i: (b, i, 0)),
        out_shape=jax.ShapeDtypeStruct((bsz, t, width), BF16),
        scratch_shapes=[pltpu.VMEM((t, qb), F32), pltpu.VMEM((1, qb), I32),
                        pltpu.VMEM((nkv, 1, rep * qb), F32),
                        pltpu.VMEM((nkv, HEAD_DIM + ones_rows, rep * qb), F32),
                        pltpu.VMEM((8 * SORT_TILES, qb), F32), pltpu.VMEM((8 * SORT_TILES, qb), F32),
                        pltpu.VMEM((8 * SORT_TILES, qb), F32)],
        compiler_params=_cparams("parallel", "arbitrary"),
        name="dsa_prompt",
    )(q, qi, wit, kidx, kbf, vt)


def _dsa_score_kernel(pt_ref, qi_ref, wi_ref, knew_ref, *rest):
    pages, (keys_ref, newkey_ref) = rest[:PAGES_PER_STEP], rest[PAGES_PER_STEP:]
    qi = qi_ref[...]
    wi = wi_ref[...]
    ki_t = jnp.concatenate([page_ref[...].astype(BF16) for page_ref in pages], axis=1)
    logits = _dot(qi.astype(BF16), ki_t)
    keys_ref[...] = jnp.sum(jnp.maximum(logits, 0.0) * wi, axis=0, keepdims=True)
    logit_new = jnp.sum(qi * knew_ref[...], axis=1, keepdims=True)
    sc_new = jnp.sum(jnp.maximum(logit_new, 0.0) * wi, axis=0, keepdims=True)
    newkey_ref[...] = jnp.broadcast_to(sc_new, (1, LANES))


def _dsa_score(page_table, qi, wi, kidx_new, cache_kidx_t):
    bsz, npages = page_table.shape
    steps = npages // PAGES_PER_STEP
    page_spec = lambda p: pl.BlockSpec(
        (None, IDX_DIM, PAGE), lambda b, j, pt: (pt[b, j * PAGES_PER_STEP + p], 0, 0))
    per_seq = lambda shape: pl.BlockSpec((None,) + shape, lambda b, j, pt: (b, 0, 0))
    return pl.pallas_call(
        _dsa_score_kernel,
        grid_spec=pltpu.PrefetchScalarGridSpec(
            num_scalar_prefetch=1,
            grid=(bsz, steps),
            in_specs=[per_seq((IDX_HEADS, IDX_DIM)), per_seq((IDX_HEADS, 1)), per_seq((1, IDX_DIM))]
                     + [page_spec(p) for p in range(PAGES_PER_STEP)],
            out_specs=[pl.BlockSpec((None, 1, PAGES_PER_STEP * PAGE), lambda b, j, pt: (b, 0, j)),
                       per_seq((1, LANES))]),
        out_shape=[jax.ShapeDtypeStruct((bsz, 1, npages * PAGE), F32),
                   jax.ShapeDtypeStruct((bsz, 1, LANES), F32)],
        compiler_params=_cparams("parallel", "arbitrary"),
        name="dsa_decode_score",
    )(page_table, qi, wi, kidx_new, *([cache_kidx_t] * PAGES_PER_STEP))


def _dsa_select_kernel(keys_ref, thr_ref, j0_ref, j0_s):
    rows, width = keys_ref.shape
    lane = lax.broadcasted_iota(I32, (rows, LANES), 1)

    def count(pred):
        def body(t, acc):
            off = pl.multiple_of(t * LANES, LANES)
            return acc + jnp.where(pred(keys_ref[:, pl.ds(off, LANES)], lane + off), 1, 0)

        acc = lax.fori_loop(0, width // LANES, body, jnp.zeros((rows, LANES), I32))
        return jnp.sum(acc, axis=1, keepdims=True)

    thr, j0 = _select_threshold(count, (rows, 1), width, j0_s, _kth_by_bit_search(count, (rows, 1)))
    thr_ref[...] = jnp.broadcast_to(thr, thr_ref.shape)
    j0_ref[...] = jnp.broadcast_to(j0, j0_ref.shape)


def _dsa_select(keys):
    rows, width = keys.shape
    const = lambda i: (0, 0)
    return pl.pallas_call(
        _dsa_select_kernel,
        grid=(1,),
        in_specs=[pl.BlockSpec((rows, width), const)],
        out_specs=[pl.BlockSpec((rows, LANES), const), pl.BlockSpec((rows, LANES), const)],
        out_shape=[jax.ShapeDtypeStruct((rows, LANES), F32), jax.ShapeDtypeStruct((rows, LANES), I32)],
        scratch_shapes=[pltpu.VMEM((rows, 1), I32)],
        compiler_params=_cparams("arbitrary"),
        name="dsa_decode_select",
    )(keys)


def _dsa_decode_kernel(pt_ref, q_ref, keys_ref, tail_ref, thr_ref, j0_ref, knew_ref, vnew_ref, *rest,
                       n_past, nkv):
    j = pl.program_id(1)
    nheads = q_ref.shape[0]
    rep = nheads // nkv
    kpages, vpages = rest[:PAGES_PER_STEP], rest[PAGES_PER_STEP:2 * PAGES_PER_STEP]
    o_ref, m_s, l_s, acc_s = rest[2 * PAGES_PER_STEP:]

    @pl.when(j == 0)
    def _():
        m_s[...] = jnp.full_like(m_s, NEG_BIG)
        l_s[...] = jnp.zeros_like(l_s)
        acc_s[...] = jnp.zeros_like(acc_s)

    q = q_ref[...]
    thr = thr_ref[:, 0:1]
    j0 = j0_ref[:, 0:1]
    width = PAGES_PER_STEP * PAGE * nkv
    row = lax.broadcasted_iota(I32, (1, width), 1) + j * width
    head_kv = lax.broadcasted_iota(I32, (nheads, 1), 0) // rep

    def update(att, value_fn):
        m_old = m_s[...]
        m_new = jnp.maximum(m_old, jnp.max(att, axis=1, keepdims=True))
        alpha = jnp.exp2(m_old - m_new)
        p = jnp.exp2(att - m_new)
        l_s[...] = alpha * l_s[...] + jnp.sum(p, axis=1, keepdims=True)
        acc_s[...] = alpha * acc_s[...] + value_fn(p)
        m_s[...] = m_new

    kcat = jnp.concatenate([r[...].astype(BF16) for r in kpages], axis=0)
    vcat = jnp.concatenate([r[...].astype(BF16) for r in vpages], axis=0)
    bias = _mask_bias(keys_ref[...], row // nkv, thr, j0)
    bias = jnp.where(row % nkv == head_kv, bias, NEG_BIG)
    update(_dot_nt(q.astype(BF16), kcat) + bias, lambda pr: _dot(pr.astype(BF16), vcat))

    @pl.when(j == pl.num_programs(1) - 1)
    def _():
        k_new = jnp.zeros_like(q)
        v_new = jnp.zeros_like(q)
        for g in range(nkv):
            k_new = jnp.where(head_kv == g, knew_ref[g:g + 1, :], k_new)
            v_new = jnp.where(head_kv == g, vnew_ref[g:g + 1, :], v_new)
        bias_new = _mask_bias(tail_ref[:, 0:1], jnp.int32(n_past), thr, j0)
        att_new = jnp.sum(q * k_new, axis=1, keepdims=True) + bias_new
        update(att_new, lambda pr: pr * v_new)
        o_ref[...] = (acc_s[...] / l_s[...]).astype(BF16)


def _dsa_decode(page_table, q, keys, tail, thr, j0, k_new, v_new, cache_k, cache_v):
    bsz, npages = page_table.shape
    nheads = q.shape[1]
    nkv = k_new.shape[1]
    steps = npages // PAGES_PER_STEP
    page_spec = lambda p: pl.BlockSpec(
        (PAGE * nkv, HEAD_DIM), lambda b, j, pt: (pt[b, j * PAGES_PER_STEP + p], 0))
    per_seq = lambda shape: pl.BlockSpec((None,) + shape, lambda b, j, pt: (b, 0, 0))
    keys_spec = pl.BlockSpec((None, 1, PAGES_PER_STEP * PAGE * nkv), lambda b, j, pt: (b, 0, j))
    return pl.pallas_call(
        functools.partial(_dsa_decode_kernel, n_past=npages * PAGE, nkv=nkv),
        grid_spec=pltpu.PrefetchScalarGridSpec(
            num_scalar_prefetch=1,
            grid=(bsz, steps),
            in_specs=[per_seq((nheads, HEAD_DIM)), keys_spec, per_seq((1, LANES)), per_seq((1, LANES)),
                      per_seq((1, LANES)), per_seq((nkv, HEAD_DIM)), per_seq((nkv, HEAD_DIM))]
                     + [page_spec(p) for p in range(PAGES_PER_STEP)] * 2,
            out_specs=per_seq((nheads, HEAD_DIM)),
            scratch_shapes=[pltpu.VMEM((nheads, 1), F32), pltpu.VMEM((nheads, 1), F32),
                            pltpu.VMEM((nheads, HEAD_DIM), F32)]),
        out_shape=jax.ShapeDtypeStruct((bsz, nheads, HEAD_DIM), BF16),
        compiler_params=_cparams("parallel", "arbitrary"),
        name="dsa_decode_attend",
    )(page_table, q, keys, tail, thr, j0, k_new, v_new,
      *([cache_k] * PAGES_PER_STEP), *([cache_v] * PAGES_PER_STEP))


def _pad_cols(w, width):
    return jnp.pad(w, ((0, 0), (0, width - w.shape[1])))


def _block_diag(blocks):
    g, r, c = blocks.shape
    eye = jnp.eye(g, dtype=blocks.dtype)
    return (blocks[:, :, None, :] * eye[:, None, :, None]).reshape(g * r, g * c)


def _s5_params(lam_re, lam_im, log_dt, b_re, b_im, c_re, c_im, d_skip, w_glu, b_glu):
    dt = jnp.exp(log_dt)[:, None]
    mag = jnp.exp(lam_re * dt)
    bar_re = mag * jnp.cos(lam_im * dt)
    bar_im = mag * jnp.sin(lam_im * dt)
    inv = 1.0 / (lam_re * lam_re + lam_im * lam_im)
    coef_re = (((bar_re - 1.0) * lam_re + bar_im * lam_im) * inv)[..., None]
    coef_im = ((bar_im * lam_re - (bar_re - 1.0) * lam_im) * inv)[..., None]
    bb_re = coef_re * b_re - coef_im * b_im
    bb_im = coef_re * b_im + coef_im * b_re
    lam_rows = jnp.stack([bar_re.reshape(-1), bar_im.reshape(-1)])
    to_in = lambda z: _block_diag(jnp.swapaxes(z, 1, 2))
    wb = jnp.concatenate([to_in(bb_re), to_in(bb_im)], axis=1).astype(BF16)
    to_out = lambda z: _block_diag(jnp.swapaxes(z, 1, 2))
    wc = jnp.concatenate([to_out(c_re), to_out(-c_im)], axis=0).astype(BF16)
    return (lam_rows, wb, wc, d_skip[None, :], w_glu.astype(BF16), b_glu[None, :])


def kernel(x_prompt, x_sample, state_s5_re, state_s5_im, state_mlstm_c, state_mlstm_n, state_mlstm_m,
           cache_k, cache_v, cache_kidx, page_table, norm_mix, norm_mlp, norm_final, w_in0, s5_lam_re,
           s5_lam_im, s5_log_dt, s5_b_re, s5_b_im, s5_c_re, s5_c_im, s5_d, w_glu, b_glu, b_igate,
           b_fgate, w_out0, w_in1, w_out1, w_up, w_down):
    bp, tp, d = x_prompt.shape
    db, ts, _ = x_sample.shape
    assert ts == 1, "the decode path handles one new token per sequence"
    s5_groups, s5_state = s5_lam_re.shape
    s5_width = s5_groups * S5_GROUP
    nstate = s5_groups * s5_state
    nh = b_igate.shape[0]
    ml_width = nh * HEAD_DIM
    assert s5_width == ml_width == 512 and d == 1024
    n_past = page_table.shape[1] * PAGE
    kvw = cache_k.shape[2] * cache_k.shape[3]
    nheads = w_out1.shape[0] // HEAD_DIM

    gate_cols = s5_width + 4 * ml_width
    w0 = jnp.concatenate([w_in0[:, :gate_cols], _pad_cols(w_in0[:, gate_cols:], LANES)], axis=1).astype(BF16)
    s5p = _s5_params(s5_lam_re, s5_lam_im, s5_log_dt, s5_b_re, s5_b_im, s5_c_re, s5_c_im, s5_d, w_glu, b_glu)
    gate_bias = jnp.concatenate([b_igate, b_fgate])
    bias_row, bias_col = gate_bias[None, :], gate_bias[:, None]
    ki0 = 1024 + 2 * kvw + IDX_HEADS * IDX_DIM
    w1 = jnp.concatenate([w_in1[:, :ki0], _pad_cols(w_in1[:, ki0:ki0 + IDX_DIM], LANES)], axis=1).astype(BF16)
    wvt = w_in1[:, 1024 + kvw:1024 + 2 * kvw].T.astype(BF16)
    wwit = w_in1[:, ki0 + IDX_DIM:].T.astype(BF16)
    wo0, wo1 = w_out0.astype(BF16), w_out1.astype(BF16)
    wup, wdn = w_up.astype(BF16), w_down.astype(BF16)
    g_mix, g_mlp, g_fin = norm_mix[:, None, :], norm_mlp[:, None, :], norm_final[None, :]

    def trunk(x2d, bsz, t, s5_h0, c0, n0, m0, attend):
        n = bsz * t
        z0 = _proj0(x2d, g_mix[0], w0)
        gates = z0[:, gate_cols:gate_cols + 2 * nh]
        m0b = jnp.broadcast_to(m0[:, :, None, None], (bsz, nh, 1, LANES))
        n0r = n0[:, :, None, :]
        if t == 1:
            y_s5, h_t = _s5_step(z0, s5_h0, s5p, s5_width)
            y_ml, c_t, n_t, m_t = _mlstm_step(z0.reshape(bsz, 1, -1), gates.reshape(bsz, 1, 2 * nh),
                                              bias_row, c0, n0r, m0b, nh)
        else:
            y_s5, h_t = _s5_seq(z0.reshape(bsz, t, -1), s5_h0.reshape(bsz, 1, -1), s5p, s5_width)
            g3 = gates.reshape(bsz, t, 2 * nh)
            y_ml, c_t, n_t, m_t = _mlstm_chunked(z0.reshape(bsz, t, -1), g3, jnp.swapaxes(g3, 1, 2),
                                                 bias_col, bias_row, c0, n0r, m0b, nh)
        h_t = h_t.reshape(bsz, 2, s5_groups, s5_state)
        states = (h_t[:, 0], h_t[:, 1], c_t, n_t[:, :, 0, :], m_t[:, :, 0, 0])
        h1 = _post(x2d, y_s5.reshape(n, s5_width), (y_ml.reshape(n, ml_width), 0), wo0, g_mlp[0],
                   wup[0], wdn[0], g_fin, final_norm=False)
        q, k, v, kbf, qi, kidx, kidxbf, vt, wit = _proj1(h1, g_mix[1], w1, wvt, wwit)
        o = attend(q, k, v, kbf, qi, kidx, kidxbf, vt, wit)
        y = _post(h1, o, (o, 1), wo1, g_mlp[1], wup[1], wdn[1], g_fin, final_norm=True)
        rows = (k.reshape(bsz, t, -1, HEAD_DIM), v.reshape(bsz, t, -1, HEAD_DIM), kidx.reshape(bsz, t, IDX_DIM))
        return y.reshape(bsz, t, d), states, rows

    def attend_prompt(q, k, v, kbf, qi, kidx, kidxbf, vt, wit):
        r3 = lambda z: z.reshape(bp, tp, -1)
        return _dsa_prompt(r3(q), r3(qi), wit, r3(kidxbf), r3(kbf), vt).reshape(bp * tp, -1)

    def attend_decode(q, k, v, kbf, qi, kidx, kidxbf, vt, wit):
        qi3 = qi.astype(F32).reshape(db, IDX_HEADS, IDX_DIM)
        wi3 = wit.T.reshape(db, IDX_HEADS, 1)
        keys, newkey = _dsa_score(page_table, qi3, wi3, kidx.reshape(db, 1, IDX_DIM),
                                  jnp.swapaxes(cache_kidx, 1, 2))
        tail = jnp.where(lax.broadcasted_iota(I32, (db, LANES), 1) == 0, newkey[:, 0, :], -jnp.inf)
        keys = jnp.concatenate([keys[:, 0, :], tail], axis=1)
        thr, j0 = _dsa_select(keys)
        nkv = kvw // HEAD_DIM
        keys_rows = jnp.repeat(keys[:, :n_past], nkv, axis=1)[:, None, :]
        o = _dsa_decode(page_table, q.astype(F32).reshape(db, nheads, HEAD_DIM), keys_rows,
                        keys[:, None, n_past:], thr[:, None, :], j0[:, None, :],
                        k.reshape(db, nkv, HEAD_DIM), v.reshape(db, nkv, HEAD_DIM),
                        cache_k.reshape(-1, HEAD_DIM), cache_v.reshape(-1, HEAD_DIM))
        return o.reshape(db, nheads * HEAD_DIM)

    zeros = lambda *shape: jnp.zeros(shape, F32)
    y_p, st_p, rows_p = trunk(x_prompt.reshape(bp * tp, d), bp, tp, zeros(bp, 2 * nstate),
                              zeros(bp, nh, HEAD_DIM, HEAD_DIM), zeros(bp, nh, HEAD_DIM), zeros(bp, nh),
                              attend_prompt)
    s5_h0 = jnp.concatenate([state_s5_re.reshape(db, nstate), state_s5_im.reshape(db, nstate)], axis=1)
    y_s, st_s, rows_s = trunk(x_sample.reshape(db, d), db, 1, s5_h0, state_mlstm_c, state_mlstm_n,
                              state_mlstm_m, attend_decode)
    return (y_p, y_s) + st_p + rows_p + st_s + rows_s
```

```python
import functools
import math

import jax
import jax.numpy as jnp
from jax import lax
from jax.experimental import pallas as pl
from jax.experimental.pallas import tpu as pltpu

F32, BF16, I32 = jnp.float32, jnp.bfloat16, jnp.int32

EPS = 1e-6
LANES = 128
PAGE = 128
S5_GROUP = 16
S5_STATE = 64
HEAD_DIM = 128
IDX_DIM = 64
IDX_HEADS = 8
TOPK = 256
Q_BLOCK = 128
KEY_CHUNK = 1024
COUNT_CHUNK = 512
SORT_TILES = 256
MLSTM_CHUNK = 128
S5_CHUNK = 256
ROW_TILE = 512
FF_CHUNK = 4096
PAGES_PER_STEP = 64
INT_MIN = -2 ** 31
NEG_BIG = -1e30
VMEM_LIMIT = 56 * 1024 * 1024


def _cparams(*sem):
    return pltpu.CompilerParams(dimension_semantics=sem, vmem_limit_bytes=VMEM_LIMIT)


def _rms(x, g):
    return x * lax.rsqrt(jnp.mean(x * x, axis=-1, keepdims=True) + EPS) * g


def _dot(a, b):
    return jnp.dot(a, b, preferred_element_type=F32)


def _dot_nt(a, b):
    return lax.dot_general(a, b, (((1,), (1,)), ((), ())), preferred_element_type=F32)


def _dot_tn(a, b):
    return lax.dot_general(a, b, (((0,), (0,)), ((), ())), preferred_element_type=F32)


def _proj0_kernel(x_ref, g_ref, w_ref, o_ref):
    xn = _rms(x_ref[...], g_ref[...]).astype(BF16)
    o_ref[...] = _dot(xn, w_ref[...])


def _proj0(x, g, w):
    n, d = x.shape
    tm = min(ROW_TILE, n)
    wtot = w.shape[1]
    return pl.pallas_call(
        _proj0_kernel,
        grid=(n // tm,),
        in_specs=[pl.BlockSpec((tm, d), lambda i: (i, 0)),
                  pl.BlockSpec((1, d), lambda i: (0, 0)),
                  pl.BlockSpec((d, wtot), lambda i: (0, 0))],
        out_specs=pl.BlockSpec((tm, wtot), lambda i: (i, 0)),
        out_shape=jax.ShapeDtypeStruct((n, wtot), F32),
        compiler_params=_cparams("parallel"),
        name="proj0",
    )(x, g, w)


_Q1, _K1, _V1, _QI1, _KI1, _END1 = 0, 1024, 1280, 1536, 2048, 2176
LOG2E = 1.4426950408889634


def _proj1_kernel(x_ref, g_ref, w_ref, wvt_ref, wwit_ref, q_ref, k_ref, v_ref, kbf_ref, qi_ref,
                  kidx_ref, kidxbf_ref, vt_ref, wit_ref):
    xn = _rms(x_ref[...], g_ref[...]).astype(BF16)
    z = _dot(xn, w_ref[...])
    q_ref[...] = (z[:, _Q1:_K1] * (HEAD_DIM ** -0.5 * LOG2E)).astype(BF16)
    k = z[:, _K1:_V1]
    k_ref[...] = k
    v_ref[...] = z[:, _V1:_QI1]
    kbf_ref[...] = k.astype(BF16)
    qi_ref[...] = z[:, _QI1:_KI1].astype(BF16)
    kidx = z[:, _KI1:_KI1 + IDX_DIM]
    kidx_ref[...] = kidx
    kidxbf_ref[...] = kidx.astype(BF16)
    vt_ref[...] = _dot_nt(wvt_ref[...], xn).astype(BF16)
    wit_ref[...] = _dot_nt(wwit_ref[...], xn) * ((IDX_DIM ** -0.5) * (IDX_HEADS ** -0.5))


def _proj1(x, g, w, wvt, wwit):
    n, d = x.shape
    tm = min(ROW_TILE, n)
    kvw = wvt.shape[0]
    row = lambda width: pl.BlockSpec((tm, width), lambda i: (i, 0))
    col = lambda height: pl.BlockSpec((height, tm), lambda i: (0, i))
    full = lambda a: pl.BlockSpec(a.shape, lambda i: (0, 0))
    shp = lambda width, dt: jax.ShapeDtypeStruct((n, width), dt)
    return pl.pallas_call(
        _proj1_kernel,
        grid=(n // tm,),
        in_specs=[row(d), pl.BlockSpec((1, d), lambda i: (0, 0)), full(w), full(wvt), full(wwit)],
        out_specs=[row(1024), row(kvw), row(kvw), row(kvw), row(512), row(IDX_DIM), row(IDX_DIM),
                   col(kvw), col(IDX_HEADS)],
        out_shape=[shp(1024, BF16), shp(kvw, F32), shp(kvw, F32), shp(kvw, BF16), shp(512, BF16),
                   shp(IDX_DIM, F32), shp(IDX_DIM, BF16),
                   jax.ShapeDtypeStruct((kvw, n), BF16), jax.ShapeDtypeStruct((IDX_HEADS, n), F32)],
        compiler_params=_cparams("parallel"),
        name="proj1",
    )(x, g, w, wvt, wwit)


def _post_kernel(h_ref, ya_ref, yb_ref, wo_ref, g_ref, wup_ref, wdn_ref, gf_ref, out_ref,
                 h1_s, xn_s, acc_s, *, final_norm):
    j = pl.program_id(1)
    half = ya_ref.shape[1]

    @pl.when(j == 0)
    def _():
        h1 = h_ref[...] + _dot(ya_ref[...], wo_ref[:half, :]) + _dot(yb_ref[...], wo_ref[half:, :])
        h1_s[...] = h1
        xn_s[...] = _rms(h1, g_ref[...]).astype(BF16)
        acc_s[...] = jnp.zeros_like(acc_s)

    r = jnp.maximum(_dot(xn_s[...], wup_ref[...]), 0.0)
    acc_s[...] += _dot((r * r).astype(BF16), wdn_ref[...])

    @pl.when(j == pl.num_programs(1) - 1)
    def _():
        o = h1_s[...] + acc_s[...]
        if final_norm:
            o = _rms(o, gf_ref[...])
        out_ref[...] = o


def _post(h, ya, yb_spec_arg, wo, g, wup, wdn, gf, *, final_norm):
    n, d = h.shape
    tm = min(ROW_TILE, n)
    yb, yb_col = yb_spec_arg
    half = d // 2
    dff = wup.shape[1]
    once = pl.Buffered(1) if dff == FF_CHUNK else None
    return pl.pallas_call(
        functools.partial(_post_kernel, final_norm=final_norm),
        grid=(n // tm, dff // FF_CHUNK),
        in_specs=[pl.BlockSpec((tm, d), lambda i, j: (i, 0)),
                  pl.BlockSpec((tm, half), lambda i, j: (i, 0)),
                  pl.BlockSpec((tm, half), lambda i, j: (i, yb_col)),
                  pl.BlockSpec((d, d), lambda i, j: (0, 0), pipeline_mode=once),
                  pl.BlockSpec((1, d), lambda i, j: (0, 0)),
                  pl.BlockSpec((d, FF_CHUNK), lambda i, j: (0, j), pipeline_mode=once),
                  pl.BlockSpec((FF_CHUNK, d), lambda i, j: (j, 0), pipeline_mode=once),
                  pl.BlockSpec((1, d), lambda i, j: (0, 0))],
        out_specs=pl.BlockSpec((tm, d), lambda i, j: (i, 0)),
        out_shape=jax.ShapeDtypeStruct((n, d), F32),
        scratch_shapes=[pltpu.VMEM((tm, d), F32), pltpu.VMEM((tm, d), BF16), pltpu.VMEM((tm, d), F32)],
        compiler_params=_cparams("parallel", "arbitrary"),
        name="post_final" if final_norm else "post",
    )(h, ya, yb, wo, g, wup, wdn, gf)


S5_SPLIT = 2


def _s5_input(u, wb_ref):
    width, nstate = wb_ref.shape[0], wb_ref.shape[1] // 2
    wi, ns = width // S5_SPLIT, nstate // S5_SPLIT
    u_bf = u.astype(BF16)
    part = lambda base: jnp.concatenate(
        [_dot(u_bf[:, r * wi:(r + 1) * wi], wb_ref[r * wi:(r + 1) * wi, base + r * ns:base + (r + 1) * ns])
         for r in range(S5_SPLIT)], axis=1)
    return part(0), part(nstate)


def _s5_output(hre, him, u, wc_ref, d_ref, wg_ref, bg_ref):
    nstate, width = hre.shape[1], wc_ref.shape[1]
    wi, ns = width // S5_SPLIT, nstate // S5_SPLIT
    hre_bf, him_bf = hre.astype(BF16), him.astype(BF16)
    y = jnp.concatenate(
        [_dot(hre_bf[:, r * ns:(r + 1) * ns], wc_ref[r * ns:(r + 1) * ns, r * wi:(r + 1) * wi])
         + _dot(him_bf[:, r * ns:(r + 1) * ns], wc_ref[nstate + r * ns:nstate + (r + 1) * ns, r * wi:(r + 1) * wi])
         for r in range(S5_SPLIT)], axis=1)
    y = jax.nn.gelu(y + d_ref[...] * u)
    gate = jax.nn.sigmoid(_dot(y.astype(BF16), wg_ref[...]) + bg_ref[...])
    return (y * gate).astype(BF16)


def _s5_seq_kernel(u_ref, h0_ref, lam_ref, wb_ref, wc_ref, d_ref, wg_ref, bg_ref, y_ref, ht_ref,
                   hre_s, him_s, carry_s):
    bsz, _, half = hre_s.shape

    @pl.when(pl.program_id(0) == 0)
    def _():
        carry_s[...] = h0_ref[...]

    for b in range(bsz):
        hre_s[b], him_s[b] = _s5_input(u_ref[b], wb_ref)
    a_re = lam_ref[0:1, :]
    a_im = lam_ref[1:2, :]

    def step(t, carry):
        out = []
        for b in range(bsz):
            h_re, h_im = carry[2 * b], carry[2 * b + 1]
            n_re = a_re * h_re - a_im * h_im + hre_s[b, pl.ds(t, 1), :]
            n_im = a_re * h_im + a_im * h_re + him_s[b, pl.ds(t, 1), :]
            hre_s[b, pl.ds(t, 1), :] = n_re
            him_s[b, pl.ds(t, 1), :] = n_im
            out += [n_re, n_im]
        return tuple(out)

    init = tuple(carry_s[b][:, lo:lo + half] for b in range(bsz) for lo in (0, half))
    final = lax.fori_loop(0, hre_s.shape[1], step, init, unroll=8)
    for b in range(bsz):
        carry_s[b, :, :half] = final[2 * b]
        carry_s[b, :, half:] = final[2 * b + 1]
        y_ref[b] = _s5_output(hre_s[b], him_s[b], u_ref[b], wc_ref, d_ref, wg_ref, bg_ref)
    ht_ref[...] = carry_s[...]


def _s5_step_kernel(u_ref, h0_ref, lam_ref, wb_ref, wc_ref, d_ref, wg_ref, bg_ref, y_ref, ht_ref):
    half = lam_ref.shape[1]
    u = u_ref[...]
    bu_re, bu_im = _s5_input(u, wb_ref)
    a_re = lam_ref[0:1, :]
    a_im = lam_ref[1:2, :]
    h_re = h0_ref[:, :half]
    h_im = h0_ref[:, half:]
    n_re = a_re * h_re - a_im * h_im + bu_re
    n_im = a_re * h_im + a_im * h_re + bu_im
    ht_ref[:, :half] = n_re
    ht_ref[:, half:] = n_im
    y_ref[...] = _s5_output(n_re, n_im, u, wc_ref, d_ref, wg_ref, bg_ref)


def _s5_param_specs(width, nstate, imap):
    return [pl.BlockSpec((2, nstate), imap),
            pl.BlockSpec((width, 2 * nstate), imap),
            pl.BlockSpec((2 * nstate, width), imap),
            pl.BlockSpec((1, width), imap),
            pl.BlockSpec((width, width), imap),
            pl.BlockSpec((1, width), imap)]


def _s5_seq(z0, h0, params, width):
    bsz, t, _ = z0.shape
    nstate = params[0].shape[1]
    ts = min(S5_CHUNK, t)
    const = lambda c: (0, 0)
    return pl.pallas_call(
        _s5_seq_kernel,
        grid=(t // ts,),
        in_specs=[pl.BlockSpec((bsz, ts, width), lambda c: (0, c, 0)),
                  pl.BlockSpec((bsz, 1, 2 * nstate), lambda c: (0, 0, 0))]
                 + _s5_param_specs(width, nstate, const),
        out_specs=[pl.BlockSpec((bsz, ts, width), lambda c: (0, c, 0)),
                   pl.BlockSpec((bsz, 1, 2 * nstate), lambda c: (0, 0, 0))],
        out_shape=[jax.ShapeDtypeStruct((bsz, t, width), BF16),
                   jax.ShapeDtypeStruct((bsz, 1, 2 * nstate), F32)],
        scratch_shapes=[pltpu.VMEM((bsz, ts, nstate), F32), pltpu.VMEM((bsz, ts, nstate), F32),
                        pltpu.VMEM((bsz, 1, 2 * nstate), F32)],
        compiler_params=_cparams("arbitrary"),
        name="s5_scan",
    )(z0, h0, *params)


def _s5_step(z0, h0, params, width):
    rows = z0.shape[0]
    nstate = params[0].shape[1]
    const = lambda i: (0, 0)
    return pl.pallas_call(
        _s5_step_kernel,
        grid=(1,),
        in_specs=[pl.BlockSpec((rows, width), const), pl.BlockSpec((rows, 2 * nstate), const)]
                 + _s5_param_specs(width, nstate, const),
        out_specs=[pl.BlockSpec((rows, width), const), pl.BlockSpec((rows, 2 * nstate), const)],
        out_shape=[jax.ShapeDtypeStruct((rows, width), BF16),
                   jax.ShapeDtypeStruct((rows, 2 * nstate), F32)],
        compiler_params=_cparams("arbitrary"),
        name="s5_step",
    )(z0, h0, *params)


def _mlstm_chunk_kernel(q_ref, k_ref, v_ref, o_ref, gcol_ref, grow_ref, bcol_ref, brow_ref,
                        c0_ref, n0_ref, m0_ref, y_ref, c_ref, n_ref, m_ref):
    nh = c_ref.shape[0]
    ch = q_ref.shape[0]

    @pl.when(pl.program_id(1) == 0)
    def _():
        c_ref[...] = c0_ref[...]
        n_ref[...] = n0_ref[...]
        m_ref[...] = m0_ref[...]

    gcol = gcol_ref[...] + brow_ref[...]
    grow = grow_ref[...] + bcol_ref[...]
    t_idx = lax.broadcasted_iota(I32, (ch, ch), 0)
    s_idx = lax.broadcasted_iota(I32, (ch, ch), 1)
    causal = t_idx >= s_idx
    heads = range(nh)
    sl = [slice(h * HEAD_DIM, (h + 1) * HEAD_DIM) for h in heads]
    q = [q_ref[:, sl[h]] for h in heads]
    k = [k_ref[:, sl[h]] * (HEAD_DIM ** -0.5) for h in heads]
    v_bf = [v_ref[:, sl[h]].astype(BF16) for h in heads]
    q_bf = [x.astype(BF16) for x in q]
    k_bf = [x.astype(BF16) for x in k]
    i_col = [gcol[:, h:h + 1] for h in heads]
    i_row = [grow[h:h + 1, :] for h in heads]
    lf_col = [jax.nn.log_sigmoid(gcol[:, nh + h:nh + h + 1]) for h in heads]
    lf_row = [jax.nn.log_sigmoid(grow[nh + h:nh + h + 1, :]) for h in heads]
    b_col = [jnp.sum(jnp.where(causal, lf_row[h], 0.0), axis=1, keepdims=True) for h in heads]
    b_row = [jnp.sum(jnp.where(causal, 0.0, lf_col[h]), axis=0, keepdims=True) + lf_row[h] for h in heads]
    m_prev = [m_ref[h][:, 0:1] for h in heads]
    dmat = [jnp.where(causal, b_col[h] - b_row[h] + i_row[h], -jnp.inf) for h in heads]
    a_col = [b_col[h] + m_prev[h] for h in heads]
    mj = [jnp.maximum(a_col[h], jnp.max(dmat[h], axis=1, keepdims=True)) for h in heads]
    w_intra = [jnp.exp(dmat[h] - mj[h]) for h in heads]
    w_inter = [jnp.exp(a_col[h] - mj[h]) for h in heads]
    s = [_dot_nt(q_bf[h], k_bf[h]) * w_intra[h] for h in heads]
    c_prev = [c_ref[h] for h in heads]
    n_prev = [n_ref[h] for h in heads]
    num = [_dot(s[h].astype(BF16), v_bf[h]) + w_inter[h] * _dot(q_bf[h], c_prev[h].astype(BF16)) for h in heads]
    den = [jnp.sum(s[h], axis=1, keepdims=True) + w_inter[h] * jnp.sum(q[h] * n_prev[h], axis=1, keepdims=True)
           for h in heads]
    for h in heads:
        hout = num[h] / jnp.maximum(jnp.abs(den[h]), jnp.exp(-mj[h]))
        y_ref[:, sl[h]] = (jax.nn.sigmoid(o_ref[:, sl[h]]) * hout).astype(BF16)
    m_new = [mj[h][ch - 1:ch, :] for h in heads]
    b_last = [b_col[h][ch - 1:ch, :] for h in heads]
    kw = [k[h] * jnp.exp(b_last[h] - b_col[h] + i_col[h] - m_new[h]) for h in heads]
    decay = [jnp.exp(b_last[h] + m_prev[h] - m_new[h]) for h in heads]
    for h in heads:
        c_ref[h] = decay[h] * c_prev[h] + _dot_tn(kw[h].astype(BF16), v_bf[h])
        n_ref[h] = decay[h] * n_prev[h] + jnp.sum(kw[h], axis=0, keepdims=True)
        m_ref[h] = jnp.broadcast_to(m_new[h], (1, LANES))


def _mlstm_chunked(z0, gcol, grow, bias_col, bias_row, c0, n0, m0, nh):
    bsz, t, _ = z0.shape
    width = nh * HEAD_DIM
    ch = MLSTM_CHUNK
    zspec = lambda blk: pl.BlockSpec((None, ch, width), lambda b, c: (b, c, blk))
    state = lambda shape: pl.BlockSpec((None,) + shape, lambda b, c: (b,) + (0,) * len(shape))
    return pl.pallas_call(
        _mlstm_chunk_kernel,
        grid=(bsz, t // ch),
        in_specs=[zspec(1), zspec(2), zspec(3), zspec(4),
                  pl.BlockSpec((None, ch, 2 * nh), lambda b, c: (b, c, 0)),
                  pl.BlockSpec((None, 2 * nh, ch), lambda b, c: (b, 0, c)),
                  pl.BlockSpec((2 * nh, 1), lambda b, c: (0, 0)),
                  pl.BlockSpec((1, 2 * nh), lambda b, c: (0, 0)),
                  state((nh, HEAD_DIM, HEAD_DIM)), state((nh, 1, HEAD_DIM)), state((nh, 1, LANES))],
        out_specs=[pl.BlockSpec((None, ch, width), lambda b, c: (b, c, 0)),
                   state((nh, HEAD_DIM, HEAD_DIM)), state((nh, 1, HEAD_DIM)), state((nh, 1, LANES))],
        out_shape=[jax.ShapeDtypeStruct((bsz, t, width), BF16),
                   jax.ShapeDtypeStruct((bsz, nh, HEAD_DIM, HEAD_DIM), F32),
                   jax.ShapeDtypeStruct((bsz, nh, 1, HEAD_DIM), F32),
                   jax.ShapeDtypeStruct((bsz, nh, 1, LANES), F32)],
        compiler_params=_cparams("parallel", "arbitrary"),
        name="mlstm_chunk",
    )(z0, z0, z0, z0, gcol, grow, bias_col, bias_row, c0, n0, m0)


def _to_column(row):
    n = row.shape[1]
    eye = lax.broadcasted_iota(I32, (n, n), 0) == lax.broadcasted_iota(I32, (n, n), 1)
    return jnp.sum(jnp.where(eye, row, 0.0), axis=1, keepdims=True)


def _mlstm_step_kernel(q_ref, k_ref, v_ref, o_ref, g_ref, brow_ref, c0_ref, n0_ref, m0_ref,
                       y_ref, c_ref, n_ref, m_ref):
    nh = c_ref.shape[0]
    g = g_ref[...] + brow_ref[...]
    for h in range(nh):
        sl = slice(h * HEAD_DIM, (h + 1) * HEAD_DIM)
        q = q_ref[:, sl]
        k = k_ref[:, sl] * (HEAD_DIM ** -0.5)
        v = v_ref[:, sl]
        i_pre = g[:, h:h + 1]
        lf = jax.nn.log_sigmoid(g[:, nh + h:nh + h + 1])
        m_prev = m0_ref[h][:, 0:1]
        c_prev = c0_ref[h]
        n_prev = n0_ref[h]
        a = lf + m_prev
        mj = jnp.maximum(a, i_pre)
        w_intra = jnp.exp(i_pre - mj)
        w_inter = jnp.exp(a - mj)
        s = jnp.sum(q * k, axis=1, keepdims=True) * w_intra
        q_col = _to_column(q)
        k_col = _to_column(k)
        num = s * v + w_inter * jnp.sum(q_col * c_prev, axis=0, keepdims=True)
        den = s + w_inter * jnp.sum(q * n_prev, axis=1, keepdims=True)
        hout = num / jnp.maximum(jnp.abs(den), jnp.exp(-mj))
        y_ref[:, sl] = (jax.nn.sigmoid(o_ref[:, sl]) * hout).astype(BF16)
        w_end = jnp.exp(i_pre - mj)
        decay = jnp.exp(a - mj)
        c_ref[h] = decay * c_prev + (w_end * k_col) * v
        n_ref[h] = decay * n_prev + w_end * k
        m_ref[h] = jnp.broadcast_to(mj, (1, LANES))


def _mlstm_step(z0, g, bias_row, c0, n0, m0, nh):
    bsz = z0.shape[0]
    width = nh * HEAD_DIM
    zspec = lambda blk: pl.BlockSpec((None, 1, width), lambda b: (b, 0, blk))
    state = lambda shape: pl.BlockSpec((None,) + shape, lambda b: (b,) + (0,) * len(shape))
    return pl.pallas_call(
        _mlstm_step_kernel,
        grid=(bsz,),
        in_specs=[zspec(1), zspec(2), zspec(3), zspec(4),
                  pl.BlockSpec((None, 1, 2 * nh), lambda b: (b, 0, 0)),
                  pl.BlockSpec((1, 2 * nh), lambda b: (0, 0)),
                  state((nh, HEAD_DIM, HEAD_DIM)), state((nh, 1, HEAD_DIM)), state((nh, 1, LANES))],
        out_specs=[pl.BlockSpec((None, 1, width), lambda b: (b, 0, 0)),
                   state((nh, HEAD_DIM, HEAD_DIM)), state((nh, 1, HEAD_DIM)), state((nh, 1, LANES))],
        out_shape=[jax.ShapeDtypeStruct((bsz, 1, width), BF16),
                   jax.ShapeDtypeStruct((bsz, nh, HEAD_DIM, HEAD_DIM), F32),
                   jax.ShapeDtypeStruct((bsz, nh, 1, HEAD_DIM), F32),
                   jax.ShapeDtypeStruct((bsz, nh, 1, LANES), F32)],
        compiler_params=_cparams("parallel"),
        name="mlstm_step",
    )(z0, z0, z0, z0, g, bias_row, c0, n0, m0)


def _key_to_float(key):
    bits = key ^ ((key >> 31) & jnp.int32(0x7FFFFFFF))
    return lax.bitcast_convert_type(bits, F32)


def _kth_by_bit_search(count, shape):
    def bit_step(it, key):
        cand = key + (jnp.int32(1) << (31 - it))
        cand_f = _key_to_float(cand)
        cnt = count(lambda tile, col: tile >= cand_f)
        return jnp.where(cnt >= TOPK, cand, key)

    key = lax.fori_loop(0, 32, bit_step, jnp.full(shape, INT_MIN, I32))
    has_thr = key > INT_MIN
    return jnp.where(has_thr, _key_to_float(jnp.where(has_thr, key, 0)), -jnp.inf)


def _select_threshold(count, shape, width, j0_s, thr):
    has_thr = thr > -jnp.inf
    need = TOPK - count(lambda tile, col: tile > thr)
    n_eq = count(lambda tile, col: tile == thr)
    j0_s[...] = jnp.where(has_thr, jnp.int32(width), jnp.int32(-1))
    surplus = jnp.max(jnp.where(has_thr & (n_eq > need), 1, 0))
    nbits = max(1, (width - 1).bit_length())

    @pl.when(surplus > 0)
    def _():
        def idx_step(it, j0):
            cand = j0 | (jnp.int32(1) << (nbits - 1 - it))
            cnt = count(lambda tile, col: (tile == thr) & (col < cand))
            return jnp.where(cnt < need, cand, j0)

        j0 = lax.fori_loop(0, nbits, idx_step, jnp.zeros(shape, I32))
        j0_s[...] = jnp.where(has_thr, j0, jnp.int32(-1))

    return thr, j0_s[...]


def _network_pass(load, store, stages, groups):
    for grp in groups:
        vals = [load(i) for i in grp]
        pos = {gi: n for n, gi in enumerate(grp)}
        for size, dist in stages:
            for gi in grp:
                gl = gi ^ dist
                if gl > gi:
                    a, b = pos[gi], pos[gl]
                    hi, lo = jnp.maximum(vals[a], vals[b]), jnp.minimum(vals[a], vals[b])
                    vals[a], vals[b] = (hi, lo) if (gi & size) == 0 else (lo, hi)
        for n, gi in enumerate(grp):
            store(gi, vals[n])


_NET_GROUP = 16
_LOW_GROUPS = [[m * _NET_GROUP + t for t in range(_NET_GROUP)] for m in range(SORT_TILES // _NET_GROUP)]
_HIGH_GROUPS = [[m + (SORT_TILES // _NET_GROUP) * t for t in range(_NET_GROUP)]
                for m in range(SORT_TILES // _NET_GROUP)]


def _tile(ref, base, i):
    return ref.at[pl.ds(base + 8 * i, 8), :]


def _sort_block_desc(src, src_base, dst):
    assert SORT_TILES == 256 and _NET_GROUP == 16
    low = lambda size: [(size, d) for d in (8, 4, 2, 1) if d < size]
    first = [st for size in (2, 4, 8, 16) for st in low(size)]
    _network_pass(lambda i: _tile(src, src_base, i)[...],
                  lambda i, v: _tile(dst, 0, i).__setitem__(Ellipsis, v), first, _LOW_GROUPS)
    ld = lambda i: _tile(dst, 0, i)[...]
    st = lambda i, v: _tile(dst, 0, i).__setitem__(Ellipsis, v)
    for size in (32, 64, 128, 256):
        _network_pass(ld, st, [(size, d) for d in (128, 64, 32, 16) if d < size], _HIGH_GROUPS)
        _network_pass(ld, st, low(size), _LOW_GROUPS)


def _merge_top(run, other, tmp, shift=None):
    def ld(i):
        o = _tile(other, 0, SORT_TILES - 1 - i)[...]
        if shift is not None:
            o = pltpu.roll(o, shift, axis=0)
        return jnp.maximum(_tile(run, 0, i)[...], o)

    _network_pass(ld, lambda i, v: _tile(tmp, 0, i).__setitem__(Ellipsis, v),
                  [(SORT_TILES, d) for d in (128, 64, 32, 16)], _HIGH_GROUPS)
    _network_pass(lambda i: _tile(tmp, 0, i)[...], lambda i, v: _tile(run, 0, i).__setitem__(Ellipsis, v),
                  [(SORT_TILES, d) for d in (8, 4, 2, 1)], _LOW_GROUPS)


def _mask_bias(scores, cols, thr, j0):
    sel = (scores > thr) | ((scores == thr) & (cols <= j0))
    return jnp.where(sel, 0.0, NEG_BIG)


def _dsa_prompt_kernel(q_ref, qi_ref, wit_ref, ki_ref, k_ref, vt_ref, o_ref, sc_s, j0_s, m_s, acc_s,
                       run_s, blk_s, tmp_s):
    qb = q_ref.shape[0]
    kc_len = KEY_CHUNK
    nheads = q_ref.shape[1] // HEAD_DIM
    nkv = k_ref.shape[1] // HEAD_DIM
    rep = nheads // nkv
    i = pl.program_id(1)
    nchunks = ((i + 1) * qb + kc_len - 1) // kc_len
    qpos = i * qb + lax.broadcasted_iota(I32, (1, qb), 1)
    sub = lax.broadcasted_iota(I32, (kc_len, qb), 0)

    qi = qi_ref[...]
    wit = wit_ref[...]
    qi_all = jnp.concatenate([qi[:, h * IDX_DIM:(h + 1) * IDX_DIM] for h in range(IDX_HEADS)], axis=0)

    def score_chunk(c, _):
        off = pl.multiple_of(c * kc_len, kc_len)
        ki = ki_ref[pl.ds(off, kc_len), :]
        logits = _dot_nt(ki, qi_all)
        sc = jnp.zeros((kc_len, qb), F32)
        for h in range(IDX_HEADS):
            sc = sc + jnp.maximum(logits[:, h * qb:(h + 1) * qb], 0.0) * wit[h:h + 1, :]
        sc_s[pl.ds(off, kc_len), :] = jnp.where(sub + off <= qpos, sc, -jnp.inf)
        return 0

    lax.fori_loop(0, nchunks, score_chunk, 0)

    sort_rows = 8 * SORT_TILES
    nblocks = ((i + 1) * qb + sort_rows - 1) // sort_rows

    @pl.when(nchunks * kc_len < nblocks * sort_rows)
    def _():
        sc_s[pl.ds(pl.multiple_of(nchunks * kc_len, kc_len), kc_len), :] = jnp.full((kc_len, qb), -jnp.inf, F32)

    run_s[...] = jnp.full_like(run_s, -jnp.inf)

    def sort_block(b, _):
        _sort_block_desc(sc_s, pl.multiple_of(b * sort_rows, sort_rows), blk_s)
        _merge_top(run_s, blk_s, tmp_s)
        return 0

    lax.fori_loop(0, nblocks, sort_block, 0)
    for shift in (4, 2):
        _merge_top(run_s, run_s, tmp_s, shift)
    top = [jnp.maximum(_tile(run_s, 0, t)[...], pltpu.roll(_tile(run_s, 0, SORT_TILES - 1 - t)[...], 1, axis=0))
           for t in range(SORT_TILES)]
    while len(top) > 1:
        top = [jnp.minimum(a, b) for a, b in zip(top[0::2], top[1::2])]
    kth = top[0][0:1, :]

    def count(pred):
        def body(c, acc):
            off = pl.multiple_of(c * COUNT_CHUNK, COUNT_CHUNK)
            ind = jnp.where(pred(sc_s[pl.ds(off, COUNT_CHUNK), :], sub[:COUNT_CHUNK] + off), 1, 0)
            return acc + jnp.sum(ind.reshape(COUNT_CHUNK // 8, 8, qb), axis=0)

        ncount = ((i + 1) * qb + COUNT_CHUNK - 1) // COUNT_CHUNK
        acc = lax.fori_loop(0, ncount, body, jnp.zeros((8, qb), I32))
        return jnp.sum(acc, axis=0, keepdims=True)

    thr, j0 = _select_threshold(count, (1, qb), sc_s.shape[0], j0_s, kth)

    m_s[...] = jnp.full_like(m_s, NEG_BIG)
    acc_s[...] = jnp.zeros_like(acc_s)
    q = q_ref[...]
    q_g = [jnp.concatenate([q[:, (g * rep + r) * HEAD_DIM:(g * rep + r + 1) * HEAD_DIM]
                            for r in range(rep)], axis=0) for g in range(nkv)]
    ones_rows = jnp.ones((acc_s.shape[1] - HEAD_DIM, kc_len), BF16)

    def attend_chunk(c, _):
        off = pl.multiple_of(c * kc_len, kc_len)
        bias = _mask_bias(sc_s[pl.ds(off, kc_len), :], sub + off, thr, j0)
        bias = jnp.concatenate([bias] * rep, axis=1)
        groups = range(nkv)
        att = [_dot_nt(k_ref[pl.ds(off, kc_len), g * HEAD_DIM:(g + 1) * HEAD_DIM], q_g[g]) + bias
               for g in groups]
        m_old = [m_s[g] for g in groups]
        m_new = [jnp.maximum(m_old[g], jnp.max(att[g], axis=0, keepdims=True)) for g in groups]
        p = [jnp.exp2(att[g] - m_new[g]).astype(BF16) for g in groups]
        for g in groups:
            vt = jnp.concatenate([vt_ref[g * HEAD_DIM:(g + 1) * HEAD_DIM, pl.ds(off, kc_len)], ones_rows],
                                 axis=0)
            acc_s[g] = jnp.exp2(m_old[g] - m_new[g]) * acc_s[g] + _dot(vt, p[g])
            m_s[g] = m_new[g]
        return 0

    lax.fori_loop(0, nchunks, attend_chunk, 0)

    for g in range(nkv):
        acc = acc_s[g]
        out = acc[:HEAD_DIM, :] / acc[HEAD_DIM:HEAD_DIM + 1, :]
        for r in range(rep):
            hd = g * rep + r
            o_ref[:, hd * HEAD_DIM:(hd + 1) * HEAD_DIM] = out[:, r * qb:(r + 1) * qb].T.astype(BF16)


def _dsa_prompt(q, qi, wit, kidx, kbf, vt):
    bsz, t, width = q.shape
    kvw = kbf.shape[2]
    nkv = kvw // HEAD_DIM
    rep = width // HEAD_DIM // nkv
    qb = Q_BLOCK
    nq = t // qb
    assert t % (8 * SORT_TILES) == 0 and 8 * SORT_TILES == 2 * KEY_CHUNK and TOPK == SORT_TILES
    ones_rows = 16
    return pl.pallas_call(
        _dsa_prompt_kernel,
        grid=(bsz, nq),
        in_specs=[pl.BlockSpec((None, qb, width), lambda b, i: (b, i, 0)),
                  pl.BlockSpec((None, qb, qi.shape[2]), lambda b, i: (b, i, 0)),
                  pl.BlockSpec((IDX_HEADS, qb), lambda b, i: (0, b * nq + i)),
                  pl.BlockSpec((None, t, IDX_DIM), lambda b, i: (b, 0, 0)),
                  pl.BlockSpec((None, t, kvw), lambda b, i: (b, 0, 0)),
                  pl.BlockSpec((kvw, t), lambda b, i: (0, b))],
        out_specs=pl.BlockSpec((None, qb, width), lambda b, i: (b, i, 0)),
        out_shape=jax.ShapeDtypeStruct((bsz, t, width), BF16),
        scratch_shapes=[pltpu.VMEM((t, qb), F32), pltpu.VMEM((1, qb), I32),
                        pltpu.VMEM((nkv, 1, rep * qb), F32),
                        pltpu.VMEM((nkv, HEAD_DIM + ones_rows, rep * qb), F32),
                        pltpu.VMEM((8 * SORT_TILES, qb), F32), pltpu.VMEM((8 * SORT_TILES, qb), F32),
                        pltpu.VMEM((8 * SORT_TILES, qb), F32)],
        compiler_params=_cparams("parallel", "arbitrary"),
        name="dsa_prompt",
    )(q, qi, wit, kidx, kbf, vt)


def _dsa_score_kernel(pt_ref, qi_ref, wi_ref, knew_ref, *rest):
    pages, (keys_ref, newkey_ref) = rest[:PAGES_PER_STEP], rest[PAGES_PER_STEP:]
    qi = qi_ref[...]
    wi = wi_ref[...]
    ki_t = jnp.concatenate([page_ref[...].astype(BF16) for page_ref in pages], axis=1)
    logits = _dot(qi.astype(BF16), ki_t)
    keys_ref[...] = jnp.sum(jnp.maximum(logits, 0.0) * wi, axis=0, keepdims=True)
    logit_new = jnp.sum(qi * knew_ref[...], axis=1, keepdims=True)
    sc_new = jnp.sum(jnp.maximum(logit_new, 0.0) * wi, axis=0, keepdims=True)
    newkey_ref[...] = jnp.broadcast_to(sc_new, (1, LANES))


def _dsa_score(page_table, qi, wi, kidx_new, cache_kidx_t):
    bsz, npages = page_table.shape
    steps = npages // PAGES_PER_STEP
    page_spec = lambda p: pl.BlockSpec(
        (None, IDX_DIM, PAGE), lambda b, j, pt: (pt[b, j * PAGES_PER_STEP + p], 0, 0))
    per_seq = lambda shape: pl.BlockSpec((None,) + shape, lambda b, j, pt: (b, 0, 0))
    return pl.pallas_call(
        _dsa_score_kernel,
        grid_spec=pltpu.PrefetchScalarGridSpec(
            num_scalar_prefetch=1,
            grid=(bsz, steps),
            in_specs=[per_seq((IDX_HEADS, IDX_DIM)), per_seq((IDX_HEADS, 1)), per_seq((1, IDX_DIM))]
                     + [page_spec(p) for p in range(PAGES_PER_STEP)],
            out_specs=[pl.BlockSpec((None, 1, PAGES_PER_STEP * PAGE), lambda b, j, pt: (b, 0, j)),
                       per_seq((1, LANES))]),
        out_shape=[jax.ShapeDtypeStruct((bsz, 1, npages * PAGE), F32),
                   jax.ShapeDtypeStruct((bsz, 1, LANES), F32)],
        compiler_params=_cparams("parallel", "arbitrary"),
        name="dsa_decode_score",
    )(page_table, qi, wi, kidx_new, *([cache_kidx_t] * PAGES_PER_STEP))


def _dsa_select_kernel(keys_ref, thr_ref, j0_ref, j0_s):
    rows, width = keys_ref.shape
    lane = lax.broadcasted_iota(I32, (rows, LANES), 1)

    def count(pred):
        def body(t, acc):
            off = pl.multiple_of(t * LANES, LANES)
            return acc + jnp.where(pred(keys_ref[:, pl.ds(off, LANES)], lane + off), 1, 0)

        acc = lax.fori_loop(0, width // LANES, body, jnp.zeros((rows, LANES), I32))
        return jnp.sum(acc, axis=1, keepdims=True)

    thr, j0 = _select_threshold(count, (rows, 1), width, j0_s, _kth_by_bit_search(count, (rows, 1)))
    thr_ref[...] = jnp.broadcast_to(thr, thr_ref.shape)
    j0_ref[...] = jnp.broadcast_to(j0, j0_ref.shape)


def _dsa_select(keys):
    rows, width = keys.shape
    const = lambda i: (0, 0)
    return pl.pallas_call(
        _dsa_select_kernel,
        grid=(1,),
        in_specs=[pl.BlockSpec((rows, width), const)],
        out_specs=[pl.BlockSpec((rows, LANES), const), pl.BlockSpec((rows, LANES), const)],
        out_shape=[jax.ShapeDtypeStruct((rows, LANES), F32), jax.ShapeDtypeStruct((rows, LANES), I32)],
        scratch_shapes=[pltpu.VMEM((rows, 1), I32)],
        compiler_params=_cparams("arbitrary"),
        name="dsa_decode_select",
    )(keys)


def _dsa_decode_kernel(pt_ref, q_ref, keys_ref, tail_ref, thr_ref, j0_ref, knew_ref, vnew_ref, *rest,
                       n_past, nkv):
    j = pl.program_id(1)
    nheads = q_ref.shape[0]
    rep = nheads // nkv
    kpages, vpages = rest[:PAGES_PER_STEP], rest[PAGES_PER_STEP:2 * PAGES_PER_STEP]
    o_ref, m_s, l_s, acc_s = rest[2 * PAGES_PER_STEP:]

    @pl.when(j == 0)
    def _():
        m_s[...] = jnp.full_like(m_s, NEG_BIG)
        l_s[...] = jnp.zeros_like(l_s)
        acc_s[...] = jnp.zeros_like(acc_s)

    q = q_ref[...]
    thr = thr_ref[:, 0:1]
    j0 = j0_ref[:, 0:1]
    width = PAGES_PER_STEP * PAGE * nkv
    row = lax.broadcasted_iota(I32, (1, width), 1) + j * width
    head_kv = lax.broadcasted_iota(I32, (nheads, 1), 0) // rep

    def update(att, value_fn):
        m_old = m_s[...]
        m_new = jnp.maximum(m_old, jnp.max(att, axis=1, keepdims=True))
        alpha = jnp.exp2(m_old - m_new)
        p = jnp.exp2(att - m_new)
        l_s[...] = alpha * l_s[...] + jnp.sum(p, axis=1, keepdims=True)
        acc_s[...] = alpha * acc_s[...] + value_fn(p)
        m_s[...] = m_new

    kcat = jnp.concatenate([r[...].astype(BF16) for r in kpages], axis=0)
    vcat = jnp.concatenate([r[...].astype(BF16) for r in vpages], axis=0)
    bias = _mask_bias(keys_ref[...], row // nkv, thr, j0)
    bias = jnp.where(row % nkv == head_kv, bias, NEG_BIG)
    update(_dot_nt(q.astype(BF16), kcat) + bias, lambda pr: _dot(pr.astype(BF16), vcat))

    @pl.when(j == pl.num_programs(1) - 1)
    def _():
        k_new = jnp.zeros_like(q)
        v_new = jnp.zeros_like(q)
        for g in range(nkv):
            k_new = jnp.where(head_kv == g, knew_ref[g:g + 1, :], k_new)
            v_new = jnp.where(head_kv == g, vnew_ref[g:g + 1, :], v_new)
        bias_new = _mask_bias(tail_ref[:, 0:1], jnp.int32(n_past), thr, j0)
        att_new = jnp.sum(q * k_new, axis=1, keepdims=True) + bias_new
        update(att_new, lambda pr: pr * v_new)
        o_ref[...] = (acc_s[...] / l_s[...]).astype(BF16)


def _dsa_decode(page_table, q, keys, tail, thr, j0, k_new, v_new, cache_k, cache_v):
    bsz, npages = page_table.shape
    nheads = q.shape[1]
    nkv = k_new.shape[1]
    steps = npages // PAGES_PER_STEP
    page_spec = lambda p: pl.BlockSpec(
        (PAGE * nkv, HEAD_DIM), lambda b, j, pt: (pt[b, j * PAGES_PER_STEP + p], 0))
    per_seq = lambda shape: pl.BlockSpec((None,) + shape, lambda b, j, pt: (b, 0, 0))
    keys_spec = pl.BlockSpec((None, 1, PAGES_PER_STEP * PAGE * nkv), lambda b, j, pt: (b, 0, j))
    return pl.pallas_call(
        functools.partial(_dsa_decode_kernel, n_past=npages * PAGE, nkv=nkv),
        grid_spec=pltpu.PrefetchScalarGridSpec(
            num_scalar_prefetch=1,
            grid=(bsz, steps),
            in_specs=[per_seq((nheads, HEAD_DIM)), keys_spec, per_seq((1, LANES)), per_seq((1, LANES)),
                      per_seq((1, LANES)), per_seq((nkv, HEAD_DIM)), per_seq((nkv, HEAD_DIM))]
                     + [page_spec(p) for p in range(PAGES_PER_STEP)] * 2,
            out_specs=per_seq((nheads, HEAD_DIM)),
            scratch_shapes=[pltpu.VMEM((nheads, 1), F32), pltpu.VMEM((nheads, 1), F32),
                            pltpu.VMEM((nheads, HEAD_DIM), F32)]),
        out_shape=jax.ShapeDtypeStruct((bsz, nheads, HEAD_DIM), BF16),
        compiler_params=_cparams("parallel", "arbitrary"),
        name="dsa_decode_attend",
    )(page_table, q, keys, tail, thr, j0, k_new, v_new,
      *([cache_k] * PAGES_PER_STEP), *([cache_v] * PAGES_PER_STEP))


def _pad_cols(w, width):
    return jnp.pad(w, ((0, 0), (0, width - w.shape[1])))


def _block_diag(blocks):
    g, r, c = blocks.shape
    eye = jnp.eye(g, dtype=blocks.dtype)
    return (blocks[:, :, None, :] * eye[:, None, :, None]).reshape(g * r, g * c)


def _s5_params(lam_re, lam_im, log_dt, b_re, b_im, c_re, c_im, d_skip, w_glu, b_glu):
    dt = jnp.exp(log_dt)[:, None]
    mag = jnp.exp(lam_re * dt)
    bar_re = mag * jnp.cos(lam_im * dt)
    bar_im = mag * jnp.sin(lam_im * dt)
    inv = 1.0 / (lam_re * lam_re + lam_im * lam_im)
    coef_re = (((bar_re - 1.0) * lam_re + bar_im * lam_im) * inv)[..., None]
    coef_im = ((bar_im * lam_re - (bar_re - 1.0) * lam_im) * inv)[..., None]
    bb_re = coef_re * b_re - coef_im * b_im
    bb_im = coef_re * b_im + coef_im * b_re
    lam_rows = jnp.stack([bar_re.reshape(-1), bar_im.reshape(-1)])
    to_in = lambda z: _block_diag(jnp.swapaxes(z, 1, 2))
    wb = jnp.concatenate([to_in(bb_re), to_in(bb_im)], axis=1).astype(BF16)
    to_out = lambda z: _block_diag(jnp.swapaxes(z, 1, 2))
    wc = jnp.concatenate([to_out(c_re), to_out(-c_im)], axis=0).astype(BF16)
    return (lam_rows, wb, wc, d_skip[None, :], w_glu.astype(BF16), b_glu[None, :])


def kernel(x_prompt, x_sample, state_s5_re, state_s5_im, state_mlstm_c, state_mlstm_n, state_mlstm_m,
           cache_k, cache_v, cache_kidx, page_table, norm_mix, norm_mlp, norm_final, w_in0, s5_lam_re,
           s5_lam_im, s5_log_dt, s5_b_re, s5_b_im, s5_c_re, s5_c_im, s5_d, w_glu, b_glu, b_igate,
           b_fgate, w_out0, w_in1, w_out1, w_up, w_down):
    bp, tp, d = x_prompt.shape
    db, ts, _ = x_sample.shape
    assert ts == 1, "the decode path handles one new token per sequence"
    s5_groups, s5_state = s5_lam_re.shape
    s5_width = s5_groups * S5_GROUP
    nstate = s5_groups * s5_state
    nh = b_igate.shape[0]
    ml_width = nh * HEAD_DIM
    assert s5_width == ml_width == 512 and d == 1024
    n_past = page_table.shape[1] * PAGE
    kvw = cache_k.shape[2] * cache_k.shape[3]
    nheads = w_out1.shape[0] // HEAD_DIM

    gate_cols = s5_width + 4 * ml_width
    w0 = jnp.concatenate([w_in0[:, :gate_cols], _pad_cols(w_in0[:, gate_cols:], LANES)], axis=1).astype(BF16)
    s5p = _s5_params(s5_lam_re, s5_lam_im, s5_log_dt, s5_b_re, s5_b_im, s5_c_re, s5_c_im, s5_d, w_glu, b_glu)
    gate_bias = jnp.concatenate([b_igate, b_fgate])
    bias_row, bias_col = gate_bias[None, :], gate_bias[:, None]
    ki0 = 1024 + 2 * kvw + IDX_HEADS * IDX_DIM
    w1 = jnp.concatenate([w_in1[:, :ki0], _pad_cols(w_in1[:, ki0:ki0 + IDX_DIM], LANES)], axis=1).astype(BF16)
    wvt = w_in1[:, 1024 + kvw:1024 + 2 * kvw].T.astype(BF16)
    wwit = w_in1[:, ki0 + IDX_DIM:].T.astype(BF16)
    wo0, wo1 = w_out0.astype(BF16), w_out1.astype(BF16)
    wup, wdn = w_up.astype(BF16), w_down.astype(BF16)
    g_mix, g_mlp, g_fin = norm_mix[:, None, :], norm_mlp[:, None, :], norm_final[None, :]

    def trunk(x2d, bsz, t, s5_h0, c0, n0, m0, attend):
        n = bsz * t
        z0 = _proj0(x2d, g_mix[0], w0)
        gates = z0[:, gate_cols:gate_cols + 2 * nh]
        m0b = jnp.broadcast_to(m0[:, :, None, None], (bsz, nh, 1, LANES))
        n0r = n0[:, :, None, :]
        if t == 1:
            y_s5, h_t = _s5_step(z0, s5_h0, s5p, s5_width)
            y_ml, c_t, n_t, m_t = _mlstm_step(z0.reshape(bsz, 1, -1), gates.reshape(bsz, 1, 2 * nh),
                                              bias_row, c0, n0r, m0b, nh)
        else:
            y_s5, h_t = _s5_seq(z0.reshape(bsz, t, -1), s5_h0.reshape(bsz, 1, -1), s5p, s5_width)
            g3 = gates.reshape(bsz, t, 2 * nh)
            y_ml, c_t, n_t, m_t = _mlstm_chunked(z0.reshape(bsz, t, -1), g3, jnp.swapaxes(g3, 1, 2),
                                                 bias_col, bias_row, c0, n0r, m0b, nh)
        h_t = h_t.reshape(bsz, 2, s5_groups, s5_state)
        states = (h_t[:, 0], h_t[:, 1], c_t, n_t[:, :, 0, :], m_t[:, :, 0, 0])
        h1 = _post(x2d, y_s5.reshape(n, s5_width), (y_ml.reshape(n, ml_width), 0), wo0, g_mlp[0],
                   wup[0], wdn[0], g_fin, final_norm=False)
        q, k, v, kbf, qi, kidx, kidxbf, vt, wit = _proj1(h1, g_mix[1], w1, wvt, wwit)
        o = attend(q, k, v, kbf, qi, kidx, kidxbf, vt, wit)
        y = _post(h1, o, (o, 1), wo1, g_mlp[1], wup[1], wdn[1], g_fin, final_norm=True)
        rows = (k.reshape(bsz, t, -1, HEAD_DIM), v.reshape(bsz, t, -1, HEAD_DIM), kidx.reshape(bsz, t, IDX_DIM))
        return y.reshape(bsz, t, d), states, rows

    def attend_prompt(q, k, v, kbf, qi, kidx, kidxbf, vt, wit):
        r3 = lambda z: z.reshape(bp, tp, -1)
        return _dsa_prompt(r3(q), r3(qi), wit, r3(kidxbf), r3(kbf), vt).reshape(bp * tp, -1)

    def attend_decode(q, k, v, kbf, qi, kidx, kidxbf, vt, wit):
        qi3 = qi.astype(F32).reshape(db, IDX_HEADS, IDX_DIM)
        wi3 = wit.T.reshape(db, IDX_HEADS, 1)
        keys, newkey = _dsa_score(page_table, qi3, wi3, kidx.reshape(db, 1, IDX_DIM),
                                  jnp.swapaxes(cache_kidx, 1, 2))
        tail = jnp.where(lax.broadcasted_iota(I32, (db, LANES), 1) == 0, newkey[:, 0, :], -jnp.inf)
        keys = jnp.concatenate([keys[:, 0, :], tail], axis=1)
        thr, j0 = _dsa_select(keys)
        nkv = kvw // HEAD_DIM
        keys_rows = jnp.repeat(keys[:, :n_past], nkv, axis=1)[:, None, :]
        o = _dsa_decode(page_table, q.astype(F32).reshape(db, nheads, HEAD_DIM), keys_rows,
                        keys[:, None, n_past:], thr[:, None, :], j0[:, None, :],
                        k.reshape(db, nkv, HEAD_DIM), v.reshape(db, nkv, HEAD_DIM),
                        cache_k.reshape(-1, HEAD_DIM), cache_v.reshape(-1, HEAD_DIM))
        return o.reshape(db, nheads * HEAD_DIM)

    zeros = lambda *shape: jnp.zeros(shape, F32)
    y_p, st_p, rows_p = trunk(x_prompt.reshape(bp * tp, d), bp, tp, zeros(bp, 2 * nstate),
                              zeros(bp, nh, HEAD_DIM, HEAD_DIM), zeros(bp, nh, HEAD_DIM), zeros(bp, nh),
                              attend_prompt)
    s5_h0 = jnp.concatenate([state_s5_re.reshape(db, nstate), state_s5_im.reshape(db, nstate)], axis=1)
    y_s, st_s, rows_s = trunk(x_sample.reshape(db, d), db, 1, s5_h0, state_mlstm_c, state_mlstm_n,
                              state_mlstm_m, attend_decode)
    return (y_p, y_s) + st_p + rows_p + st_s + rows_s
```
